```python
import math
import jax, jax.numpy as jnp
from jax import lax
import numpy as np

D_MODEL = 1024
BATCH = 8
SEQ = 2048
DEPTH = 2
DEC_BATCH = 128
DEC_SEQ = 1
PAST_LEN = 16384
PAGE_SIZE = 128

N_META = 16
EPS = 1e-6
N_EVEN = (DEPTH + 1) // 2
N_ODD = DEPTH // 2
D_POOL = D_MODEL // 2
POOL_WINDOWS = (2, 4, 8, 16)
N_POOL_GROUPS = len(POOL_WINDOWS)
POOL_GROUP = D_POOL // N_POOL_GROUPS
POOL_BUF = max(POOL_WINDOWS) - 1
D_SSM = D_MODEL - D_POOL
SSM_GROUP = 16
N_SSM_GROUPS = D_SSM // SSM_GROUP
SSM_STATE = 64
D_FF = 2816
GLA_HEADS = 4
GLA_DK = D_MODEL // 2 // GLA_HEADS
GLA_DV = D_MODEL // GLA_HEADS
GLA_KEY = GLA_HEADS * GLA_DK
GLA_VAL = GLA_HEADS * GLA_DV
GLA_GATE_RANK = 16
GLA_GATE_NORM = 16.0
GLA_CHUNK = 64
META_PAD = (-N_META) % GLA_CHUNK
N_EXPERTS = 8
TOP_K = 2
D_EXPERT = 896

kernel_name = "hybrid_pool_s5_gla_moe_step"


def rmsnorm(x, w):
    xf = x.astype(jnp.float32)
    y = xf * lax.rsqrt(jnp.mean(xf * xf, axis=-1, keepdims=True) + EPS)
    return (y * w.astype(jnp.float32)).astype(x.dtype)


def swiglu(x, w1, w3, w2):
    return (jax.nn.silu(x @ w1) * (x @ w3)) @ w2


def pool_mix(u, start_pos, n_past, w_pool, pool_scale):
    L = u.shape[1]
    uf = u.astype(jnp.float32)
    cs = jnp.concatenate([jnp.zeros_like(uf[:, :1]), jnp.cumsum(uf, axis=1)], axis=1)
    rows = jnp.arange(n_past, L)
    pos = start_pos + rows
    outs = []
    for gi, w in enumerate(POOL_WINDOWS):
        sl = slice(gi * POOL_GROUP, (gi + 1) * POOL_GROUP)
        lo = jnp.maximum(rows + 1 - w, 0)
        cnt = jnp.minimum(pos + 1, w).astype(jnp.float32)
        mean = (cs[:, rows + 1, sl] - cs[:, lo, sl]) / cnt[None, :, None]
        d = (mean - uf[:, rows, sl]).astype(u.dtype)
        outs.append(jnp.einsum('btc,cd->btd', d, w_pool[gi]))
    return jnp.concatenate(outs, axis=-1) * pool_scale


def s5_discretize(a_re, a_im, log_dt, b_re, b_im):
    dt = jnp.exp(log_dt.astype(jnp.float32))[:, None]
    ar = a_re.astype(jnp.float32)
    ai = a_im.astype(jnp.float32)
    mag = jnp.exp(ar * dt)
    lb_re = mag * jnp.cos(ai * dt)
    lb_im = mag * jnp.sin(ai * dt)
    den = ar * ar + ai * ai
    nr = lb_re - 1.0
    f_re = (nr * ar + lb_im * ai) / den
    f_im = (lb_im * ar - nr * ai) / den
    br = b_re.astype(jnp.float32)
    bi = b_im.astype(jnp.float32)
    bb_re = f_re[..., None] * br - f_im[..., None] * bi
    bb_im = f_re[..., None] * bi + f_im[..., None] * br
    return lb_re, lb_im, bb_re, bb_im


def s5_combine(e1, e2):
    a1r, a1i, b1r, b1i = e1
    a2r, a2i, b2r, b2i = e2
    return (a2r * a1r - a2i * a1i, a2r * a1i + a2i * a1r,
            a2r * b1r - a2i * b1i + b2r, a2r * b1i + a2i * b1r + b2i)


def s5_mix(u, h0_re, h0_im, a_re, a_im, log_dt, b_re, b_im, c_re, c_im, d_skip):
    Bn, T, _ = u.shape
    uf = u.astype(jnp.float32)
    ug = uf.reshape(Bn, T, N_SSM_GROUPS, SSM_GROUP)
    lb_re, lb_im, bb_re, bb_im = s5_discretize(a_re, a_im, log_dt, b_re, b_im)
    bu_re = jnp.einsum('gnp,btgp->btgn', bb_re, ug)
    bu_im = jnp.einsum('gnp,btgp->btgn', bb_im, ug)
    el_re = jnp.broadcast_to(lb_re, bu_re.shape)
    el_im = jnp.broadcast_to(lb_im, bu_re.shape)
    ac_re, ac_im, hr, hi = lax.associative_scan(s5_combine, (el_re, el_im, bu_re, bu_im), axis=1)
    if h0_re is not None:
        r0 = h0_re.astype(jnp.float32)[:, None]
        i0 = h0_im.astype(jnp.float32)[:, None]
        hr, hi = hr + ac_re * r0 - ac_im * i0, hi + ac_re * i0 + ac_im * r0
    y = (jnp.einsum('gpn,btgn->btgp', c_re.astype(jnp.float32), hr)
         - jnp.einsum('gpn,btgn->btgp', c_im.astype(jnp.float32), hi))
    y = y.reshape(Bn, T, D_SSM) + d_skip.astype(jnp.float32) * uf
    return y.astype(u.dtype), hr[:, -1].astype(u.dtype), hi[:, -1].astype(u.dtype)


def even_layer(h, pool_past, h0_re, h0_im, start_pos, pe):
    (norm_mix, w_in, w_pool, pool_scale, a_re, a_im, log_dt, b_re, b_im, c_re, c_im,
     d_skip, w_glu, b_glu, w_out, norm_ffn, w1, w3, w2) = pe
    xn = rmsnorm(h, norm_mix)
    u = xn @ w_in
    u_a, u_b = u[..., :D_POOL], u[..., D_POOL:]
    if pool_past is None:
        u_full, n_past = u_a, 0
    else:
        u_full, n_past = jnp.concatenate([pool_past.astype(u_a.dtype), u_a], axis=1), POOL_BUF
    y_a = pool_mix(u_full, start_pos, n_past, w_pool, pool_scale)
    y_b, hr, hi = s5_mix(u_b, h0_re, h0_im, a_re, a_im, log_dt, b_re, b_im, c_re, c_im, d_skip)
    z = jax.nn.gelu(y_b)
    y_b = z * jax.nn.sigmoid(z @ w_glu + b_glu)
    h = h + jnp.concatenate([y_a, y_b], axis=-1) @ w_out
    h = h + swiglu(rmsnorm(h, norm_ffn), w1, w3, w2)
    return h, u_full[:, -POOL_BUF:], hr, hi


def gla_chunked(q, k, v, gk):
    Bn, H, L, _ = q.shape
    padw = ((0, 0), (0, 0), (META_PAD, 0), (0, 0))
    q, k, v, gk = (jnp.pad(t, padw) for t in (q, k, v, gk))
    nc = (L + META_PAD) // GLA_CHUNK
    def chunks(t):
        return jnp.moveaxis(t.reshape(Bn, H, nc, GLA_CHUNK, t.shape[-1]), 2, 0)
    causal = jnp.tril(jnp.ones((GLA_CHUNK, GLA_CHUNK), dtype=bool))[:, :, None]
    def step(S, inp):
        qc, kc, vc, gc = inp
        bc = jnp.cumsum(gc, axis=2)
        diff = bc[:, :, :, None, :] - bc[:, :, None, :, :]
        decay = jnp.where(causal, jnp.exp(jnp.where(causal, diff, 0.0)), 0.0)
        att = jnp.einsum('bhtk,bhsk,bhtsk->bhts', qc, kc, decay)
        o = (jnp.einsum('bhtk,bhkv->bhtv', qc * jnp.exp(bc), S)
             + jnp.einsum('bhts,bhsv->bhtv', att, vc))
        blast = bc[:, :, -1:, :]
        S = (jnp.exp(blast[:, :, 0, :, None]) * S
             + jnp.einsum('bhsk,bhsv->bhkv', kc * jnp.exp(blast - bc), vc))
        return S, o
    S0 = jnp.zeros((Bn, H, GLA_DK, GLA_DV), jnp.float32)
    S, o = lax.scan(step, S0, (chunks(q), chunks(k), chunks(v), chunks(gk)))
    o = jnp.moveaxis(o, 0, 2).reshape(Bn, H, nc * GLA_CHUNK, GLA_DV)[:, :, META_PAD:]
    return o, S


def gla_recurrent(q, k, v, gk, S0):
    def step(S, inp):
        qt, kt, vt, gt = inp
        S = jnp.exp(gt)[..., None] * S + kt[..., :, None] * vt[..., None, :]
        return S, jnp.einsum('bhk,bhkv->bhv', qt, S)
    S, o = lax.scan(step, S0, tuple(jnp.moveaxis(t, 2, 0) for t in (q, k, v, gk)))
    return jnp.moveaxis(o, 0, 2), S


def moe_ffn(x, w_router, w1, w3, w2):
    probs = jax.nn.softmax((x @ w_router).astype(jnp.float32), axis=-1)
    top_p, top_i = lax.top_k(probs, TOP_K)
    top_p = top_p / jnp.sum(top_p, axis=-1, keepdims=True)
    gates = jnp.sum(jax.nn.one_hot(top_i, N_EXPERTS, dtype=jnp.float32) * top_p[..., None], axis=-2)
    y = jnp.zeros(x.shape, jnp.float32)
    for e in range(N_EXPERTS):
        y = y + gates[..., e:e + 1] * swiglu(x, w1[e], w3[e], w2[e]).astype(jnp.float32)
    return y.astype(x.dtype)


def odd_layer(h, S0, po):
    (norm_mix, w_in, w_gk1, w_gk2, b_gk, gla_norm, w_out, norm_ffn, w_router, w1, w3, w2) = po
    Bn, T, _ = h.shape
    xn = rmsnorm(h, norm_mix)
    proj = xn @ w_in
    q, k, v, g = jnp.split(proj, [GLA_KEY, 2 * GLA_KEY, 2 * GLA_KEY + GLA_VAL], axis=-1)
    gk = jax.nn.log_sigmoid(((xn @ w_gk1) @ w_gk2 + b_gk).astype(jnp.float32)) / GLA_GATE_NORM
    def heads(t, dh):
        return t.astype(jnp.float32).reshape(Bn, T, GLA_HEADS, dh).transpose(0, 2, 1, 3)
    q = heads(q, GLA_DK) * (GLA_DK ** -0.5)
    k, v, gk = heads(k, GLA_DK), heads(v, GLA_DV), heads(gk, GLA_DK)
    if S0 is None:
        o, S = gla_chunked(q, k, v, gk)
    else:
        o, S = gla_recurrent(q, k, v, gk, S0.astype(jnp.float32))
    o = o.transpose(0, 2, 1, 3)
    o = rmsnorm(o, gla_norm) * jax.nn.silu(g.astype(jnp.float32).reshape(Bn, T, GLA_HEADS, GLA_DV))
    h = h + o.reshape(Bn, T, GLA_VAL).astype(h.dtype) @ w_out
    h = h + moe_ffn(rmsnorm(h, norm_ffn), w_router, w1, w3, w2)
    return h, S.astype(h.dtype)


def setup_inputs(seed: int = 0) -> dict:
    key = jax.random.key(seed)
    ks = iter(jax.random.split(key, 64))
    f32 = jnp.float32
    def nrm(shape, scale):
        return jax.random.normal(next(ks), shape, f32) * scale
    def gain(shape):
        return 1.0 + 0.02 * jax.random.normal(next(ks), shape, f32)
    G, N, P = N_SSM_GROUPS, SSM_STATE, SSM_GROUP
    d = {}
    d["x_prompt"] = nrm((BATCH, SEQ, D_MODEL), 1.0)
    d["x_sample"] = nrm((DEC_BATCH, DEC_SEQ, D_MODEL), 1.0)
    d["state_pool"] = nrm((N_EVEN, DEC_BATCH, POOL_BUF, D_POOL), 1.0)
    d["state_s5_re"] = nrm((N_EVEN, DEC_BATCH, G, N), 0.5)
    d["state_s5_im"] = nrm((N_EVEN, DEC_BATCH, G, N), 0.5)
    d["state_gla"] = nrm((N_ODD, DEC_BATCH, GLA_HEADS, GLA_DK, GLA_DV), 1.0)
    d["meta_tokens"] = nrm((N_META, D_MODEL), 1.0)
    d["norm_mix_e"] = gain((N_EVEN, D_MODEL))
    d["w_in_e"] = nrm((N_EVEN, D_MODEL, D_POOL + D_SSM), D_MODEL ** -0.5)
    d["w_pool"] = nrm((N_EVEN, N_POOL_GROUPS, POOL_GROUP, POOL_GROUP), POOL_GROUP ** -0.5)
    d["pool_scale"] = gain((N_EVEN, D_POOL))
    d["s5_a_re"] = -0.5 + 0.01 * jax.random.normal(next(ks), (N_EVEN, G, N), f32)
    d["s5_a_im"] = (math.pi * jnp.broadcast_to(jnp.arange(N, dtype=f32), (N_EVEN, G, N))
                    + 0.01 * jax.random.normal(next(ks), (N_EVEN, G, N), f32))
    d["s5_log_dt"] = jax.random.uniform(next(ks), (N_EVEN, G), f32, math.log(1e-3), math.log(1e-1))
    d["s5_b_re"] = nrm((N_EVEN, G, N, P), (2 * P) ** -0.5)
    d["s5_b_im"] = nrm((N_EVEN, G, N, P), (2 * P) ** -0.5)
    d["s5_c_re"] = nrm((N_EVEN, G, P, N), N ** -0.5)
    d["s5_c_im"] = nrm((N_EVEN, G, P, N), N ** -0.5)
    d["s5_d"] = nrm((N_EVEN, D_SSM), 1.0)
    d["w_glu"] = nrm((N_EVEN, D_SSM, D_SSM), D_SSM ** -0.5)
    d["b_glu"] = nrm((N_EVEN, D_SSM), 0.02)
    d["w_out_e"] = nrm((N_EVEN, D_MODEL, D_MODEL), D_MODEL ** -0.5)
    d["norm_ffn_e"] = gain((N_EVEN, D_MODEL))
    d["ffn_w1"] = nrm((N_EVEN, D_MODEL, D_FF), D_MODEL ** -0.5)
    d["ffn_w3"] = nrm((N_EVEN, D_MODEL, D_FF), D_MODEL ** -0.5)
    d["ffn_w2"] = nrm((N_EVEN, D_FF, D_MODEL), D_FF ** -0.5)
    d["norm_mix_o"] = gain((N_ODD, D_MODEL))
    d["w_in_o"] = nrm((N_ODD, D_MODEL, 2 * GLA_KEY + 2 * GLA_VAL), D_MODEL ** -0.5)
    d["w_gk1"] = nrm((N_ODD, D_MODEL, GLA_GATE_RANK), D_MODEL ** -0.5)
    d["w_gk2"] = nrm((N_ODD, GLA_GATE_RANK, GLA_KEY), GLA_GATE_RANK ** -0.5)
    d["b_gk"] = nrm((N_ODD, GLA_KEY), 0.02)
    d["gla_norm"] = gain((N_ODD, GLA_DV))
    d["w_out_o"] = nrm((N_ODD, GLA_VAL, D_MODEL), GLA_VAL ** -0.5)
    d["norm_ffn_o"] = gain((N_ODD, D_MODEL))
    d["w_router"] = nrm((N_ODD, D_MODEL, N_EXPERTS), D_MODEL ** -0.5)
    d["moe_w1"] = nrm((N_ODD, N_EXPERTS, D_MODEL, D_EXPERT), D_MODEL ** -0.5)
    d["moe_w3"] = nrm((N_ODD, N_EXPERTS, D_MODEL, D_EXPERT), D_MODEL ** -0.5)
    d["moe_w2"] = nrm((N_ODD, N_EXPERTS, D_EXPERT, D_MODEL), D_EXPERT ** -0.5)
    d["norm_final"] = gain((D_MODEL,))
    return d


def reference(x_prompt, x_sample, state_pool, state_s5_re, state_s5_im, state_gla, meta_tokens,
              norm_mix_e, w_in_e, w_pool, pool_scale, s5_a_re, s5_a_im, s5_log_dt, s5_b_re, s5_b_im,
              s5_c_re, s5_c_im, s5_d, w_glu, b_glu, w_out_e, norm_ffn_e, ffn_w1, ffn_w3, ffn_w2,
              norm_mix_o, w_in_o, w_gk1, w_gk2, b_gk, gla_norm, w_out_o, norm_ffn_o, w_router,
              moe_w1, moe_w3, moe_w2, norm_final):
    meta = jnp.broadcast_to(meta_tokens.astype(x_prompt.dtype)[None], (x_prompt.shape[0], N_META, D_MODEL))
    h_p = jnp.concatenate([meta, x_prompt], axis=1)
    h_s = x_sample
    pool_p, pool_s, re_p, re_s, im_p, im_s, gla_p, gla_s = [], [], [], [], [], [], [], []
    for layer in range(DEPTH):
        j = layer // 2
        if layer % 2 == 0:
            pe = (norm_mix_e[j], w_in_e[j], w_pool[j], pool_scale[j], s5_a_re[j], s5_a_im[j],
                  s5_log_dt[j], s5_b_re[j], s5_b_im[j], s5_c_re[j], s5_c_im[j], s5_d[j],
                  w_glu[j], b_glu[j], w_out_e[j], norm_ffn_e[j], ffn_w1[j], ffn_w3[j], ffn_w2[j])
            h_p, bp, rp, ip = even_layer(h_p, None, None, None, 0, pe)
            h_s, bs, rs, is_ = even_layer(h_s, state_pool[j], state_s5_re[j], state_s5_im[j],
                                          PAST_LEN - POOL_BUF, pe)
            pool_p.append(bp); pool_s.append(bs)
            re_p.append(rp); re_s.append(rs)
            im_p.append(ip); im_s.append(is_)
        else:
            po = (norm_mix_o[j], w_in_o[j], w_gk1[j], w_gk2[j], b_gk[j], gla_norm[j], w_out_o[j],
                  norm_ffn_o[j], w_router[j], moe_w1[j], moe_w3[j], moe_w2[j])
            h_p, sp = odd_layer(h_p, None, po)
            h_s, ss = odd_layer(h_s, state_gla[j], po)
            gla_p.append(sp); gla_s.append(ss)
    y_prompt = rmsnorm(h_p, norm_final)[:, N_META:]
    y_sample = rmsnorm(h_s, norm_final)
    return (y_prompt, y_sample, jnp.stack(pool_p), jnp.stack(pool_s), jnp.stack(re_p), jnp.stack(re_s),
            jnp.stack(im_p), jnp.stack(im_s), jnp.stack(gla_p), jnp.stack(gla_s))
```

```python
import functools

import jax
import jax.numpy as jnp
from jax import lax
from jax.experimental import pallas as pl
from jax.experimental.pallas import tpu as pltpu

F32 = jnp.float32
BF16 = jnp.bfloat16

EPS = 1e-6
N_META = 16
PAST_LEN = 16384
POOL_WINDOWS = (2, 4, 8, 16)
POOL_BUF = max(POOL_WINDOWS) - 1
GLA_HEADS = 4
GLA_GATE_NORM = 16.0
CHUNK = 64
SUB = 16
MXU_DIM = 256
LANES = 128
VMEM_LIMIT = 56 * 1024 * 1024


def _cp(*sem, vmem=VMEM_LIMIT):
    return pltpu.CompilerParams(dimension_semantics=sem, vmem_limit_bytes=vmem)


def _rms(x, g):
    return x * lax.rsqrt(jnp.mean(x * x, axis=-1, keepdims=True) + EPS) * g


def _dot(a, b):
    return jnp.dot(a, b, preferred_element_type=F32)


def _dot_nt(a, b):
    return lax.dot_general(a, b, (((1,), (1,)), ((), ())), preferred_element_type=F32)


def _dot_tn(a, b):
    return lax.dot_general(a, b, (((0,), (0,)), ((), ())), preferred_element_type=F32)


def _full(shape):
    return pl.BlockSpec(shape, lambda *_: (0,) * len(shape))


def _resident(shape):
    return pl.BlockSpec(shape, lambda *_: (0,) * len(shape), pipeline_mode=pl.Buffered(1))


def _row_tile(r, pref):
    for t in pref:
        if r % t == 0:
            return t
    return r


def _norm_proj_kernel(x_ref, g_ref, w_ref, o_ref):
    x = x_ref[...]
    x = x.reshape(x.shape[-2], x.shape[-1])
    o_ref[...] = _dot(_rms(x, g_ref[...]).astype(BF16), w_ref[...])


def _norm_proj_flat(x, g, w, tm):
    r, d = x.shape
    n = w.shape[1]
    return pl.pallas_call(
        _norm_proj_kernel,
        grid=(r // tm,),
        in_specs=[pl.BlockSpec((tm, d), lambda i: (i, 0)), _full((1, d)), _resident((d, n))],
        out_specs=pl.BlockSpec((tm, n), lambda i: (i, 0)),
        out_shape=jax.ShapeDtypeStruct((r, n), F32),
        compiler_params=_cp("parallel"),
        name="norm_proj_flat",
    )(x, g, w)


def _norm_proj_time_major(x, g, w, tt):
    b, l, d = x.shape
    n = w.shape[1]
    out = pl.pallas_call(
        _norm_proj_kernel,
        grid=(b, l // tt),
        in_specs=[pl.BlockSpec((1, tt, d), lambda bi, ti: (bi, ti, 0)), _full((1, d)), _resident((d, n))],
        out_specs=pl.BlockSpec((tt, n), lambda bi, ti: (ti, bi)),
        out_shape=jax.ShapeDtypeStruct((l, b * n), F32),
        compiler_params=_cp("parallel", "parallel"),
        name="norm_proj_tm",
    )(x, g, w)
    return out.reshape(l * b, n)


def _pool_kernel(u_ref, init_ref, wp_ref, scale_ref, y_ref, st_ref, ext_ref, *, tt, nb, pos0):
    i = pl.program_id(0)
    rows = tt * nb
    halo = (POOL_BUF + 1) * nb
    shift = nb.bit_length() - 1

    @pl.when(i == 0)
    def _():
        ext_ref[0:halo, :] = init_ref[...]

    ext_ref[halo:halo + rows, :] = u_ref[...]
    t_in = lax.shift_right_logical(lax.broadcasted_iota(jnp.int32, (rows, LANES), 0), shift)
    pos1 = t_in + (pos0 + 1 + i * tt)
    group = u_ref.shape[1] // len(POOL_WINDOWS)
    for gi, w in enumerate(POOL_WINDOWS):
        lo = gi * group
        s = ext_ref[(POOL_BUF + 2 - w) * nb:halo + rows, lo:lo + group]
        k = 1
        while k < w:
            n = s.shape[0]
            s = s[k * nb:, :] + s[:n - k * nb, :]
            k *= 2
        cnt = jnp.clip(pos1, 1, w).astype(F32)
        d = s / cnt - u_ref[:, lo:lo + group]
        y = _dot(d.astype(BF16), wp_ref[gi])
        y_ref[:, lo:lo + group] = y * scale_ref[:, lo:lo + group]
    tail = ext_ref[rows:rows + halo, :]
    ext_ref[0:halo, :] = tail

    @pl.when(i == pl.num_programs(0) - 1)
    def _():
        st_ref[...] = tail[nb:, :]


def _pool_mix(u, init, wp, scale, *, tt, nb, nblk, rot, pos0):
    rows = tt * nb
    dp = wp.shape[0] * wp.shape[1]
    halo = (POOL_BUF + 1) * nb
    blk = lambda i: ((i + rot) % nblk, 0)
    return pl.pallas_call(
        functools.partial(_pool_kernel, tt=tt, nb=nb, pos0=pos0),
        grid=(nblk,),
        in_specs=[pl.BlockSpec((rows, dp), blk), _full((halo, dp)), _full(wp.shape), _full((1, dp))],
        out_specs=[pl.BlockSpec((rows, dp), blk), _full((POOL_BUF * nb, dp))],
        out_shape=[jax.ShapeDtypeStruct((nblk * rows, dp), F32),
                   jax.ShapeDtypeStruct((POOL_BUF * nb, dp), F32)],
        scratch_shapes=[pltpu.VMEM((halo + rows, dp), F32)],
        compiler_params=_cp("arbitrary"),
        name="pool_mix",
    )(u, init, wp, scale)


def _s5_prep_kernel(ar_ref, ai_ref, ldt_ref, br_ref, bi_ref, lbr_ref, lbi_ref, bbr_ref, bbi_ref):
    dt = jnp.exp(ldt_ref[...])
    ar = ar_ref[...]
    ai = ai_ref[...]
    mag = jnp.exp(ar * dt)
    lb_re = mag * jnp.cos(ai * dt)
    lb_im = mag * jnp.sin(ai * dt)
    den = ar * ar + ai * ai
    nr = lb_re - 1.0
    f_re = (nr * ar + lb_im * ai) / den
    f_im = (lb_im * ar - nr * ai) / den
    lbr_ref[...] = lb_re
    lbi_ref[...] = lb_im
    br = br_ref[...]
    bi = bi_ref[...]
    bbr_ref[...] = f_re * br - f_im * bi
    bbi_ref[...] = f_re * bi + f_im * br


def _s5_prep(a_re, a_im, log_dt, b_re, b_im):
    g, n, p = b_re.shape
    gn = g * n
    row = lambda t: t.reshape(1, gn)
    to_pgn = lambda t: t.transpose(2, 0, 1).reshape(p, gn)
    ldt = jnp.broadcast_to(log_dt[:, None], (g, n))
    return pl.pallas_call(
        _s5_prep_kernel,
        out_shape=[jax.ShapeDtypeStruct((1, gn), F32)] * 2 + [jax.ShapeDtypeStruct((p, gn), F32)] * 2,
        name="s5_prep",
    )(row(a_re), row(a_im), row(ldt), to_pgn(b_re), to_pgn(b_im))


def _block_diag_in(bb_pgn, g, n, p):
    bb = bb_pgn.reshape(p, g, n).transpose(1, 0, 2)
    eye = jnp.eye(g, dtype=bb.dtype)
    full = (bb[:, :, None, :] * eye[:, None, :, None]).reshape(g * p, g * n)
    nblk = g * p // MXU_DIM
    cols = g * n // nblk
    return jnp.stack([full[k * MXU_DIM:(k + 1) * MXU_DIM, k * cols:(k + 1) * cols] for k in range(nblk)])


def _block_diag_out(c_gpn):
    g, p, n = c_gpn.shape
    eye = jnp.eye(g, dtype=c_gpn.dtype)
    full = (c_gpn.transpose(0, 2, 1)[:, :, None, :] * eye[:, None, :, None]).reshape(g * n, g * p)
    nblk = g * p // MXU_DIM
    rows = g * n // nblk
    return jnp.stack([full[k * rows:(k + 1) * rows, k * MXU_DIM:(k + 1) * MXU_DIM] for k in range(nblk)])


def _s5_kernel(u_ref, h0r_ref, h0i_ref, lbr_ref, lbi_ref, wbr_ref, wbi_ref, wcr_ref, wci_ref,
               dsk_ref, wg_ref, bg_ref, y_ref, str_ref, sti_ref, bur_ref, bui_ref, hr_ref, hi_ref,
               *, tt, nb):
    i = pl.program_id(0)
    gn = bur_ref.shape[1]
    nkb = wbr_ref.shape[0]
    sb = gn // nkb

    @pl.when(i == 0)
    def _():
        hr_ref[...] = h0r_ref[...]
        hi_ref[...] = h0i_ref[...]

    u = u_ref[...]
    ub = u.astype(BF16)
    for kb in range(nkb):
        uk = ub[:, kb * MXU_DIM:(kb + 1) * MXU_DIM]
        bur_ref[:, kb * sb:(kb + 1) * sb] = _dot(uk, wbr_ref[kb])
        bui_ref[:, kb * sb:(kb + 1) * sb] = _dot(uk, wbi_ref[kb])

    cw = 4 * LANES
    for c in range(gn // cw):
        cols = slice(c * cw, (c + 1) * cw)
        lr = jnp.broadcast_to(lbr_ref[:, cols], (nb, cw))
        li = jnp.broadcast_to(lbi_ref[:, cols], (nb, cw))

        def step(t, carry, cols=cols, lr=lr, li=li):
            h_re, h_im = carry
            r = pl.multiple_of(t * nb, nb)
            n_re = lr * h_re - li * h_im + bur_ref[pl.ds(r, nb), cols]
            n_im = lr * h_im + li * h_re + bui_ref[pl.ds(r, nb), cols]
            bur_ref[pl.ds(r, nb), cols] = n_re
            bui_ref[pl.ds(r, nb), cols] = n_im
            return n_re, n_im

        h_re, h_im = lax.fori_loop(0, tt, step, (hr_ref[:, cols], hi_ref[:, cols]),
                                   unroll=min(tt, 8))
        hr_ref[:, cols] = h_re
        hi_ref[:, cols] = h_im

    zs = []
    for kb in range(nkb):
        hrb = bur_ref[:, kb * sb:(kb + 1) * sb].astype(BF16)
        hib = bui_ref[:, kb * sb:(kb + 1) * sb].astype(BF16)
        ch = slice(kb * MXU_DIM, (kb + 1) * MXU_DIM)
        y = _dot(hrb, wcr_ref[kb]) - _dot(hib, wci_ref[kb]) + dsk_ref[:, ch] * u[:, ch]
        zs.append(jax.nn.gelu(y))
    z = jnp.concatenate(zs, axis=1)
    gate = jax.nn.sigmoid(_dot(z.astype(BF16), wg_ref[...]) + bg_ref[...])
    y_ref[...] = z * gate

    @pl.when(i == pl.num_programs(0) - 1)
    def _():
        str_ref[...] = hr_ref[...]
        sti_ref[...] = hi_ref[...]


def _s5_mix(u, h0r, h0i, lbr, lbi, wbr, wbi, wcr, wci, dsk, wg, bg, *, tt, nb, nblk, rot):
    rows = tt * nb
    gn = lbr.shape[1]
    ds_ = wg.shape[0]
    blk_in = lambda i: ((i + rot) % nblk, 1)
    blk_out = lambda i: ((i + rot) % nblk, 0)
    return pl.pallas_call(
        functools.partial(_s5_kernel, tt=tt, nb=nb),
        grid=(nblk,),
        in_specs=[pl.BlockSpec((rows, ds_), blk_in), _full((nb, gn)), _full((nb, gn)),
                  _full((1, gn)), _full((1, gn)), _full(wbr.shape), _full(wbi.shape),
                  _full(wcr.shape), _full(wci.shape), _full((1, ds_)), _full(wg.shape), _full((1, ds_))],
        out_specs=[pl.BlockSpec((rows, ds_), blk_out), _full((nb, gn)), _full((nb, gn))],
        out_shape=[jax.ShapeDtypeStruct((nblk * rows, ds_), F32),
                   jax.ShapeDtypeStruct((nb, gn), F32), jax.ShapeDtypeStruct((nb, gn), F32)],
        scratch_shapes=[pltpu.VMEM((rows, gn), F32), pltpu.VMEM((rows, gn), F32),
                        pltpu.VMEM((nb, gn), F32), pltpu.VMEM((nb, gn), F32)],
        compiler_params=_cp("arbitrary"),
        name="s5_mix",
    )(u, h0r, h0i, lbr, lbi, wbr, wbi, wcr, wci, dsk, wg, bg)


def _out_proj_kernel(ya_ref, yb_ref, h_ref, w_ref, o_ref):
    da = ya_ref.shape[1]
    h = h_ref[...]
    y = _dot(ya_ref[...].astype(BF16), w_ref[0:da, :]) + _dot(yb_ref[...].astype(BF16), w_ref[da:, :])
    o_ref[...] = h + y.reshape(h.shape)


def _out_proj_flat(ya, yb, h, w, tm):
    r, d = h.shape
    da, db = ya.shape[1], yb.shape[1]
    return pl.pallas_call(
        _out_proj_kernel,
        grid=(r // tm,),
        in_specs=[pl.BlockSpec((tm, da), lambda i: (i, 0)), pl.BlockSpec((tm, db), lambda i: (i, 0)),
                  pl.BlockSpec((tm, d), lambda i: (i, 0)), _resident(w.shape)],
        out_specs=pl.BlockSpec((tm, d), lambda i: (i, 0)),
        out_shape=jax.ShapeDtypeStruct((r, d), F32),
        compiler_params=_cp("parallel"),
        name="out_proj_flat",
    )(ya, yb, h, w)


def _out_proj_from_time_major(ya, yb, h, w, tt):
    b, l, d = h.shape
    da, db = ya.shape[1], yb.shape[1]
    return pl.pallas_call(
        _out_proj_kernel,
        grid=(b, l // tt),
        in_specs=[pl.BlockSpec((tt, da), lambda bi, ti: (ti, bi)), pl.BlockSpec((tt, db), lambda bi, ti: (ti, bi)),
                  pl.BlockSpec((1, tt, d), lambda bi, ti: (bi, ti, 0)), _resident(w.shape)],
        out_specs=pl.BlockSpec((1, tt, d), lambda bi, ti: (bi, ti, 0)),
        out_shape=jax.ShapeDtypeStruct((b, l, d), F32),
        compiler_params=_cp("parallel", "parallel"),
        name="out_proj_tm",
    )(ya.reshape(l, b * da), yb.reshape(l, b * db), h, w)


def _ffn_kernel(h_ref, g_ref, w1_ref, w3_ref, w2_ref, o_ref, *, fc):
    h = h_ref[...]
    xn = _rms(h, g_ref[...]).astype(BF16)
    acc = h
    for c in range(w1_ref.shape[1] // fc):
        cs = slice(c * fc, (c + 1) * fc)
        a = _dot(xn, w1_ref[:, cs])
        b = _dot(xn, w3_ref[:, cs])
        acc = acc + _dot((jax.nn.silu(a) * b).astype(BF16), w2_ref[cs, :])
    o_ref[...] = acc


def _ffn(h, g, w1, w3, w2, tm):
    r, d = h.shape
    f = w1.shape[1]
    fc = MXU_DIM if f % MXU_DIM == 0 else f
    return pl.pallas_call(
        functools.partial(_ffn_kernel, fc=fc),
        grid=(r // tm,),
        in_specs=[pl.BlockSpec((tm, d), lambda i: (i, 0)), _full((1, d)),
                  _resident((d, f)), _resident((d, f)), _resident((f, d))],
        out_specs=pl.BlockSpec((tm, d), lambda i: (i, 0)),
        out_shape=jax.ShapeDtypeStruct((r, d), F32),
        compiler_params=_cp("parallel"),
        name="ffn",
    )(h, g, w1, w3, w2)


def _gla_proj_kernel(h_ref, g_ref, win_ref, wg1_ref, wg2_ref, bgk_ref,
                     q_ref, k_ref, gk_ref, v_ref, gt_ref, *, q_scale):
    xn = _rms(h_ref[...], g_ref[...]).astype(BF16)
    kd = q_ref.shape[1]
    vd = v_ref.shape[1]
    q_ref[...] = _dot(xn, win_ref[:, 0:kd]) * q_scale
    k_ref[...] = _dot(xn, win_ref[:, kd:2 * kd])
    v_ref[...] = _dot(xn, win_ref[:, 2 * kd:2 * kd + vd])
    gt_ref[...] = _dot(xn, win_ref[:, 2 * kd + vd:])
    low = _dot(xn, wg1_ref[...]).astype(BF16)
    z = _dot(low, wg2_ref[...]) + bgk_ref[...]
    log_sig = jnp.minimum(z, 0.0) - jnp.log1p(jnp.exp(-jnp.abs(z)))
    gk_ref[...] = log_sig / GLA_GATE_NORM


def _gla_proj(h, g, w_in, wg1, wg2, bgk, kd, vd, tm):
    r, d = h.shape
    rows = lambda n: pl.BlockSpec((tm, n), lambda i: (i, 0))
    return pl.pallas_call(
        functools.partial(_gla_proj_kernel, q_scale=float((kd // GLA_HEADS) ** -0.5)),
        grid=(r // tm,),
        in_specs=[rows(d), _full((1, d)), _resident(w_in.shape), _full(wg1.shape), _full(wg2.shape),
                  _full((1, kd))],
        out_specs=[rows(kd), rows(kd), rows(kd), rows(vd), rows(vd)],
        out_shape=[jax.ShapeDtypeStruct((r, kd), F32)] * 3 + [jax.ShapeDtypeStruct((r, vd), F32)] * 2,
        compiler_params=_cp("parallel"),
        name="gla_proj",
    )(h, g, w_in, wg1, wg2, bgk)


def _split3(x):
    a = x.astype(BF16)
    r = x - a.astype(F32)
    b = r.astype(BF16)
    c = (r - b.astype(F32)).astype(BF16)
    return a, b, c


def _gla_chunk_kernel(q_ref, k_ref, gk_ref, v_ref, o_ref, s_ref, st_ref, *, nchunk, rot):
    c_rows = CHUNK
    dk = q_ref.shape[2]
    nsub = c_rows // SUB
    row_i = lax.broadcasted_iota(jnp.int32, (c_rows, c_rows), 0)
    col_i = lax.broadcasted_iota(jnp.int32, (c_rows, c_rows), 1)
    tri = (row_i >= col_i).astype(BF16)
    sub_row = lax.broadcasted_iota(jnp.int32, (SUB, c_rows), 0)
    sub_col = lax.broadcasted_iota(jnp.int32, (SUB, c_rows), 1)
    key_row = lax.broadcasted_iota(jnp.int32, (c_rows, dk), 0)
    st_ref[...] = jnp.zeros_like(st_ref)

    def chunk(c, carry):
        mem = lax.rem(c + rot, nchunk)
        r0 = pl.multiple_of(mem * c_rows, c_rows)
        rs = pl.ds(r0, c_rows)
        q = q_ref[0, rs, :]
        k = k_ref[0, rs, :]
        v = v_ref[0, rs, :]
        g1, g2, g3 = _split3(gk_ref[0, rs, :])
        bc = _dot(tri, g1) + _dot(tri, g2) + _dot(tri, g3)
        blast = bc[c_rows - 1:c_rows, :]
        vb = v.astype(BF16)
        st = st_ref[...]

        o = _dot_nt((q * jnp.exp(bc)).astype(BF16), st.astype(BF16))

        q_sub, k_hat = [], []
        att_rows = []
        for i in range(nsub):
            lo = i * SUB
            bs = bc[lo - 1:lo, :] if i > 0 else jnp.zeros((1, dk), F32)
            bc_i = bc[lo:lo + SUB, :]
            q_i = q[lo:lo + SUB, :]
            k_i = k[lo:lo + SUB, :]
            q_sub.append(q_i * jnp.exp(bc_i - bs))
            if i > 0:
                k_hat.append(jnp.where(key_row < lo, k * jnp.exp(jnp.minimum(bs - bc, 0.0)), 0.0))
            diag = jnp.zeros((SUB, c_rows), F32)
            for s in range(SUB):
                e = jnp.exp(jnp.minimum(bc_i - bc_i[s:s + 1, :], 0.0))
                col = jnp.sum(q_i * k_i[s:s + 1, :] * e, axis=-1, keepdims=True)
                diag = jnp.where(sub_col == lo + s, col, diag)
            att_rows.append(jnp.where(sub_row + lo >= sub_col, diag, 0.0))
        att = jnp.concatenate(att_rows, axis=0)
        zero = jnp.zeros((SUB, dk), F32)
        lhs = jnp.concatenate(
            [jnp.concatenate([q_sub[i] if j == i else zero for j in range(1, nsub)], axis=1)
             for i in range(nsub)], axis=0)
        rhs = jnp.concatenate(k_hat, axis=1)
        att = att + _dot_nt(lhs.astype(BF16), rhs.astype(BF16))
        o = o + _dot(att.astype(BF16), vb)
        o_ref[0, rs, :] = o

        k_dec = (k * jnp.exp(blast - bc)).astype(BF16)
        st_ref[...] = st * jnp.exp(blast) + _dot_tn(vb, k_dec)
        return carry

    lax.fori_loop(0, nchunk, chunk, 0)
    s_ref[0, 0] = st_ref[...].T


def _gla_chunked(q, k, gk, v, *, rot):
    b, l, kd = q.shape
    vd = v.shape[2]
    dk, dv = kd // GLA_HEADS, vd // GLA_HEADS
    seq = lambda n: pl.BlockSpec((1, l, n), lambda bi, hi: (bi, 0, hi))
    return pl.pallas_call(
        functools.partial(_gla_chunk_kernel, nchunk=l // CHUNK, rot=rot),
        grid=(b, GLA_HEADS),
        in_specs=[seq(dk), seq(dk), seq(dk), seq(dv)],
        out_specs=[seq(dv), pl.BlockSpec((1, 1, dk, dv), lambda bi, hi: (bi, hi, 0, 0))],
        out_shape=[jax.ShapeDtypeStruct((b, l, vd), F32),
                   jax.ShapeDtypeStruct((b, GLA_HEADS, dk, dv), F32)],
        scratch_shapes=[pltpu.VMEM((dv, dk), F32)],
        compiler_params=_cp("parallel", "parallel"),
        name="gla_chunked",
    )(q, k, gk, v)


def _gla_step_kernel(qt_ref, kt_ref, gt_ref, v_ref, s0_ref, o_ref, s_ref):
    bt = v_ref.shape[1]
    dec = jnp.exp(gt_ref[0])
    kt = kt_ref[0]
    qt = qt_ref[0]
    for j in range(bt):
        s_new = dec[:, j:j + 1] * s0_ref[j, 0] + kt[:, j:j + 1] * v_ref[0, j:j + 1, :]
        s_ref[j, 0] = s_new
        o_ref[0, j:j + 1, :] = jnp.sum(qt[:, j:j + 1] * s_new, axis=0, keepdims=True)


def _gla_step(q, k, gk, v, s0, bt=8):
    b, kd = q.shape
    vd = v.shape[1]
    dk, dv = kd // GLA_HEADS, vd // GLA_HEADS
    nt = b // bt
    cols = lambda t: t.reshape(nt, bt, kd).transpose(0, 2, 1)
    col_spec = pl.BlockSpec((1, dk, bt), lambda ti, hi: (ti, hi, 0))
    v_spec = pl.BlockSpec((1, bt, dv), lambda ti, hi: (ti, 0, hi))
    s_spec = pl.BlockSpec((bt, 1, dk, dv), lambda ti, hi: (ti, hi, 0, 0))
    o, s = pl.pallas_call(
        _gla_step_kernel,
        grid=(nt, GLA_HEADS),
        in_specs=[col_spec, col_spec, col_spec, v_spec, s_spec],
        out_specs=[v_spec, s_spec],
        out_shape=[jax.ShapeDtypeStruct((nt, bt, vd), F32), jax.ShapeDtypeStruct(s0.shape, F32)],
        compiler_params=_cp("parallel", "parallel"),
        name="gla_step",
    )(cols(q), cols(k), cols(gk), v.reshape(nt, bt, vd), s0)
    return o.reshape(b, vd), s


def _gla_out_kernel(o_ref, gt_ref, h_ref, gn_ref, w_ref, out_ref):
    dv = gn_ref.shape[1]
    parts = []
    for hd in range(o_ref.shape[1] // dv):
        sl = slice(hd * dv, (hd + 1) * dv)
        parts.append((_rms(o_ref[:, sl], gn_ref[...]) * jax.nn.silu(gt_ref[:, sl])).astype(BF16))
    out_ref[...] = h_ref[...] + _dot(jnp.concatenate(parts, axis=1), w_ref[...])


def _gla_out(o, gt, h, gnorm, w, tm):
    r, d = h.shape
    vd = o.shape[1]
    rows = lambda n: pl.BlockSpec((tm, n), lambda i: (i, 0))
    return pl.pallas_call(
        _gla_out_kernel,
        grid=(r // tm,),
        in_specs=[rows(vd), rows(vd), rows(d), _full(gnorm.shape), _resident(w.shape)],
        out_specs=rows(d),
        out_shape=jax.ShapeDtypeStruct((r, d), F32),
        compiler_params=_cp("parallel"),
        name="gla_out",
    )(o, gt, h, gnorm, w)


def _moe_kernel(h_ref, g_ref, wr_ref, w1_ref, w3_ref, w2_ref, gf_ref, o_ref, xn_ref, gate_ref):
    e = pl.program_id(1)
    ne = pl.num_programs(1)

    @pl.when(e == 0)
    def _():
        h = h_ref[...]
        xn = _rms(h, g_ref[...])
        xb = xn.astype(BF16)
        xn_ref[...] = xb
        logits = _dot(xb, wr_ref[...])
        lane = lax.broadcasted_iota(jnp.int32, logits.shape, 1)
        valid = lane < ne
        logits = jnp.where(valid, logits, -jnp.inf)
        p = jnp.exp(logits - jnp.max(logits, axis=-1, keepdims=True))
        p = p / jnp.sum(p, axis=-1, keepdims=True)
        p = jnp.where(valid, p, -1.0)
        m1 = jnp.max(p, axis=-1, keepdims=True)
        i1 = jnp.min(jnp.where(p == m1, lane, LANES), axis=-1, keepdims=True)
        rest = jnp.where(lane == i1, -1.0, p)
        m2 = jnp.max(rest, axis=-1, keepdims=True)
        i2 = jnp.min(jnp.where(rest == m2, lane, LANES), axis=-1, keepdims=True)
        tot = m1 + m2
        gate_ref[...] = jnp.where(lane == i1, m1 / tot, jnp.where(lane == i2, m2 / tot, 0.0))
        o_ref[...] = h

    xb = xn_ref[...]
    lane = lax.broadcasted_iota(jnp.int32, gate_ref.shape, 1)
    gcol = jnp.sum(jnp.where(lane == e, gate_ref[...], 0.0), axis=-1, keepdims=True)
    a = _dot(xb, w1_ref[0])
    b = _dot(xb, w3_ref[0])
    y = _dot((jax.nn.silu(a) * b).astype(BF16), w2_ref[0])
    o_ref[...] += gcol * y

    @pl.when(e == ne - 1)
    def _():
        o_ref[...] = _rms(o_ref[...], gf_ref[...])


def _moe_final(h, g, w_router, w1, w3, w2, g_final, tm):
    r, d = h.shape
    ne, _, f = w1.shape
    rows = pl.BlockSpec((tm, d), lambda i, e: (i, 0))
    return pl.pallas_call(
        _moe_kernel,
        grid=(r // tm, ne),
        in_specs=[rows, _full((1, d)), _full(w_router.shape),
                  pl.BlockSpec((1, d, f), lambda i, e: (e, 0, 0)),
                  pl.BlockSpec((1, d, f), lambda i, e: (e, 0, 0)),
                  pl.BlockSpec((1, f, d), lambda i, e: (e, 0, 0)), _full((1, d))],
        out_specs=rows,
        out_shape=jax.ShapeDtypeStruct((r, d), F32),
        scratch_shapes=[pltpu.VMEM((tm, d), BF16), pltpu.VMEM((tm, LANES), F32)],
        compiler_params=_cp("parallel", "arbitrary"),
        name="moe_final",
    )(h, g, w_router, w1, w3, w2, g_final)


def kernel(x_prompt, x_sample, state_pool, state_s5_re, state_s5_im, state_gla, meta_tokens, norm_mix_e, w_in_e, w_pool, pool_scale, s5_a_re, s5_a_im, s5_log_dt, s5_b_re, s5_b_im, s5_c_re, s5_c_im, s5_d, w_glu, b_glu, w_out_e, norm_ffn_e, ffn_w1, ffn_w3, ffn_w2, norm_mix_o, w_in_o, w_gk1, w_gk2, b_gk, gla_norm, w_out_o, norm_ffn_o, w_router, moe_w1, moe_w3, moe_w2, norm_final):
    bp, seq, d = x_prompt.shape
    bs = x_sample.shape[0]
    lp = seq + CHUNK
    nblk = lp // CHUNK
    rot = nblk - 1
    dp = w_pool.shape[1] * w_pool.shape[2]
    g_ssm, n_ssm, p_ssm = s5_b_re.shape[1:]
    gn = g_ssm * n_ssm
    kd = w_gk2.shape[2]
    vd = w_out_o.shape[1]
    row = lambda t: t.reshape(1, -1)
    bf = lambda t: t.astype(BF16)

    j = 0
    w_in_e_b, w_out_e_b = bf(w_in_e[j]), bf(w_out_e[j])
    w_pool_b, w_glu_b = bf(w_pool[j]), bf(w_glu[j])
    w1_b, w3_b, w2_b = bf(ffn_w1[j]), bf(ffn_w3[j]), bf(ffn_w2[j])
    lbr, lbi, bbr, bbi = _s5_prep(s5_a_re[j], s5_a_im[j], s5_log_dt[j], s5_b_re[j], s5_b_im[j])
    wbr = bf(_block_diag_in(bbr, g_ssm, n_ssm, p_ssm))
    wbi = bf(_block_diag_in(bbi, g_ssm, n_ssm, p_ssm))
    wcr = bf(_block_diag_out(s5_c_re[j]))
    wci = bf(_block_diag_out(s5_c_im[j]))
    w_in_o_b, w_out_o_b = bf(w_in_o[j]), bf(w_out_o[j])
    rank = w_gk1.shape[2]
    wg1 = bf(jnp.pad(w_gk1[j], ((0, 0), (0, LANES - rank))))
    wg2 = bf(jnp.pad(w_gk2[j], ((0, LANES - rank), (0, 0))))
    w_router_b = bf(jnp.pad(w_router[j], ((0, 0), (0, LANES - w_router.shape[2]))))
    mw1, mw3, mw2 = bf(moe_w1[j]), bf(moe_w3[j]), bf(moe_w2[j])

    s5_args = (lbr, lbi, wbr, wbi, wcr, wci, row(s5_d[j]), w_glu_b, row(b_glu[j]))

    tail = jnp.concatenate([jnp.zeros((CHUNK - N_META, d), F32), meta_tokens.astype(F32)], axis=0)
    h0 = jnp.concatenate([x_prompt, jnp.broadcast_to(tail[None], (bp, CHUNK, d))], axis=1)
    tt_seq = _row_tile(lp, (704, 192, CHUNK))
    tm = _row_tile(bp * lp, (768, 512, 256, 128))

    u = _norm_proj_time_major(h0, row(norm_mix_e[j]), w_in_e_b, tt_seq)
    ya, pool_p = _pool_mix(u, jnp.zeros(((POOL_BUF + 1) * bp, dp), F32), w_pool_b, row(pool_scale[j]),
                           tt=CHUNK, nb=bp, nblk=nblk, rot=rot, pos0=-(CHUNK - N_META))
    zero_state = jnp.zeros((bp, gn), F32)
    yb, re_p, im_p = _s5_mix(u, zero_state, zero_state, *s5_args, tt=CHUNK, nb=bp, nblk=nblk, rot=rot)
    h1 = _out_proj_from_time_major(ya, yb, h0, w_out_e_b, tt_seq).reshape(bp * lp, d)
    h2 = _ffn(h1, row(norm_ffn_e[j]), w1_b, w3_b, w2_b, tm)
    q, k, gk, v, gt = _gla_proj(h2, row(norm_mix_o[j]), w_in_o_b, wg1, wg2, row(b_gk[j]), kd, vd, tm)
    seq3 = lambda t: t.reshape(bp, lp, t.shape[1])
    o, gla_p = _gla_chunked(seq3(q), seq3(k), seq3(gk), seq3(v), rot=rot)
    h3 = _gla_out(o.reshape(bp * lp, vd), gt, h2, row(gla_norm[j]), w_out_o_b, tm)
    yp = _moe_final(h3, row(norm_ffn_o[j]), w_router_b, mw1, mw3, mw2, row(norm_final), tm)
    y_prompt = yp.reshape(bp, lp, d)[:, :seq]

    xs = x_sample.reshape(bs, d)
    us = _norm_proj_flat(xs, row(norm_mix_e[j]), w_in_e_b, bs)
    pool_init = jnp.pad(state_pool[j].transpose(1, 0, 2), ((1, 0), (0, 0), (0, 0))).reshape((POOL_BUF + 1) * bs, dp)
    yas, pool_s = _pool_mix(us, pool_init, w_pool_b, row(pool_scale[j]),
                            tt=1, nb=bs, nblk=1, rot=0, pos0=PAST_LEN)
    ybs, re_s, im_s = _s5_mix(us, state_s5_re[j].reshape(bs, gn), state_s5_im[j].reshape(bs, gn), *s5_args,
                              tt=1, nb=bs, nblk=1, rot=0)
    h1s = _out_proj_flat(yas, ybs, xs, w_out_e_b, bs)
    h2s = _ffn(h1s, row(norm_ffn_e[j]), w1_b, w3_b, w2_b, bs)
    qs, ks, gks, vs, gts = _gla_proj(h2s, row(norm_mix_o[j]), w_in_o_b, wg1, wg2, row(b_gk[j]), kd, vd, bs)
    os_, gla_s = _gla_step(qs, ks, gks, vs, state_gla[j])
    h3s = _gla_out(os_, gts, h2s, row(gla_norm[j]), w_out_o_b, bs)
    y_sample = _moe_final(h3s, row(norm_ffn_o[j]), w_router_b, mw1, mw3, mw2, row(norm_final), bs)

    tb = lambda t, nb: t.reshape(POOL_BUF, nb, dp).transpose(1, 0, 2)[None]
    ssm = lambda t, nb: t.reshape(1, nb, g_ssm, n_ssm)
    return (y_prompt, y_sample.reshape(bs, 1, d), tb(pool_p, bp), tb(pool_s, bs),
            ssm(re_p, bp), ssm(re_s, bs), ssm(im_p, bp), ssm(im_s, bs), gla_p[None], gla_s[None])
```

```python
import functools

import jax
import jax.numpy as jnp
from jax import lax
from jax.experimental import pallas as pl
from jax.experimental.pallas import tpu as pltpu

F32 = jnp.float32
BF16 = jnp.bfloat16

EPS = 1e-6
N_META = 16
PAST_LEN = 16384
POOL_WINDOWS = (2, 4, 8, 16)
POOL_BUF = max(POOL_WINDOWS) - 1
GLA_HEADS = 4
GLA_GATE_NORM = 16.0
CHUNK = 64
SUB = 16
MXU_DIM = 256
LANES = 128
VMEM_LIMIT = 56 * 1024 * 1024


def _cp(*sem, vmem=VMEM_LIMIT):
    return pltpu.CompilerParams(dimension_semantics=sem, vmem_limit_bytes=vmem)


def _rms(x, g):
    return x * lax.rsqrt(jnp.mean(x * x, axis=-1, keepdims=True) + EPS) * g


def _dot(a, b):
    return jnp.dot(a, b, preferred_element_type=F32)


def _dot_nt(a, b):
    return lax.dot_general(a, b, (((1,), (1,)), ((), ())), preferred_element_type=F32)


def _dot_tn(a, b):
    return lax.dot_general(a, b, (((0,), (0,)), ((), ())), preferred_element_type=F32)


def _full(shape):
    return pl.BlockSpec(shape, lambda *_: (0,) * len(shape))


def _resident(shape):
    return pl.BlockSpec(shape, lambda *_: (0,) * len(shape), pipeline_mode=pl.Buffered(1))


def _first_divisor(n, candidates):
    return next(c for c in candidates if n % c == 0)


def _s5_prep_kernel(ar_ref, ai_ref, ldt_ref, br_ref, bi_ref, lbr_ref, lbi_ref, bbr_ref, bbi_ref):
    dt = jnp.exp(ldt_ref[...])
    ar = ar_ref[...]
    ai = ai_ref[...]
    mag = jnp.exp(ar * dt)
    lb_re = mag * jnp.cos(ai * dt)
    lb_im = mag * jnp.sin(ai * dt)
    den = ar * ar + ai * ai
    nr = lb_re - 1.0
    f_re = (nr * ar + lb_im * ai) / den
    f_im = (lb_im * ar - nr * ai) / den
    lbr_ref[...] = lb_re
    lbi_ref[...] = lb_im
    br = br_ref[...]
    bi = bi_ref[...]
    bbr_ref[...] = f_re * br - f_im * bi
    bbi_ref[...] = f_re * bi + f_im * br


def _s5_prep(a_re, a_im, log_dt, b_re, b_im):
    g, n, p = b_re.shape
    gn = g * n
    row = lambda t: t.reshape(1, gn)
    to_pgn = lambda t: t.transpose(2, 0, 1).reshape(p, gn)
    ldt = jnp.broadcast_to(log_dt[:, None], (g, n))
    return pl.pallas_call(
        _s5_prep_kernel,
        out_shape=[jax.ShapeDtypeStruct((1, gn), F32)] * 2 + [jax.ShapeDtypeStruct((p, gn), F32)] * 2,
        name="s5_prep",
    )(row(a_re), row(a_im), row(ldt), to_pgn(b_re), to_pgn(b_im))


def _block_diag_in(bb_pgn, g, n, p):
    bb = bb_pgn.reshape(p, g, n).transpose(1, 0, 2)
    eye = jnp.eye(g, dtype=bb.dtype)
    full = (bb[:, :, None, :] * eye[:, None, :, None]).reshape(g * p, g * n)
    nblk = g * p // MXU_DIM
    cols = g * n // nblk
    return jnp.stack([full[k * MXU_DIM:(k + 1) * MXU_DIM, k * cols:(k + 1) * cols] for k in range(nblk)])


def _block_diag_out(c_gpn):
    g, p, n = c_gpn.shape
    eye = jnp.eye(g, dtype=c_gpn.dtype)
    full = (c_gpn.transpose(0, 2, 1)[:, :, None, :] * eye[:, None, :, None]).reshape(g * n, g * p)
    nblk = g * p // MXU_DIM
    rows = g * n // nblk
    return jnp.stack([full[k * rows:(k + 1) * rows, k * MXU_DIM:(k + 1) * MXU_DIM] for k in range(nblk)])


def _pool_block(u_a, i, ext_ref, wp_ref, scale_ref, *, tt, nb, pos0):
    rows = tt * nb
    halo = (POOL_BUF + 1) * nb
    shift = nb.bit_length() - 1
    ext_ref[halo:halo + rows, :] = u_a
    t_in = lax.shift_right_logical(lax.broadcasted_iota(jnp.int32, (rows, LANES), 0), shift)
    pos1 = t_in + (pos0 + 1 + i * tt)
    group = u_a.shape[1] // len(POOL_WINDOWS)
    ys = []
    for gi, w in enumerate(POOL_WINDOWS):
        lo = gi * group
        s = ext_ref[(POOL_BUF + 2 - w) * nb:halo + rows, lo:lo + group]
        k = 1
        while k < w:
            n = s.shape[0]
            s = s[k * nb:, :] + s[:n - k * nb, :]
            k *= 2
        cnt = jnp.clip(pos1, 1, w).astype(F32)
        d = s / cnt - u_a[:, lo:lo + group]
        ys.append(_dot(d.astype(BF16), wp_ref[gi]) * scale_ref[:, lo:lo + group])
    tail = ext_ref[rows:rows + halo, :]
    ext_ref[0:halo, :] = tail
    return jnp.concatenate(ys, axis=1), tail[nb:, :]


def _s5_block(u_b, lbr_ref, lbi_ref, wbr_ref, wbi_ref, wcr_ref, wci_ref, dsk_ref, wg_ref, bg_ref,
              bur_ref, bui_ref, hr_ref, hi_ref, *, tt, nb):
    gn = bur_ref.shape[1]
    nkb = wbr_ref.shape[0]
    sb = gn // nkb
    ub = u_b.astype(BF16)
    for kb in range(nkb):
        uk = ub[:, kb * MXU_DIM:(kb + 1) * MXU_DIM]
        bur_ref[:, kb * sb:(kb + 1) * sb] = _dot(uk, wbr_ref[kb])
        bui_ref[:, kb * sb:(kb + 1) * sb] = _dot(uk, wbi_ref[kb])

    cw = 4 * LANES
    for c in range(gn // cw):
        cols = slice(c * cw, (c + 1) * cw)
        lr = jnp.broadcast_to(lbr_ref[:, cols], (nb, cw))
        li = jnp.broadcast_to(lbi_ref[:, cols], (nb, cw))

        def step(t, carry, cols=cols, lr=lr, li=li):
            h_re, h_im = carry
            r = pl.multiple_of(t * nb, nb)
            n_re = lr * h_re - li * h_im + bur_ref[pl.ds(r, nb), cols]
            n_im = lr * h_im + li * h_re + bui_ref[pl.ds(r, nb), cols]
            bur_ref[pl.ds(r, nb), cols] = n_re
            bui_ref[pl.ds(r, nb), cols] = n_im
            return n_re, n_im

        h_re, h_im = lax.fori_loop(0, tt, step, (hr_ref[:, cols], hi_ref[:, cols]),
                                   unroll=min(tt, 8))
        hr_ref[:, cols] = h_re
        hi_ref[:, cols] = h_im

    zs = []
    for kb in range(nkb):
        hrb = bur_ref[:, kb * sb:(kb + 1) * sb].astype(BF16)
        hib = bui_ref[:, kb * sb:(kb + 1) * sb].astype(BF16)
        ch = slice(kb * MXU_DIM, (kb + 1) * MXU_DIM)
        y = _dot(hrb, wcr_ref[kb]) - _dot(hib, wci_ref[kb]) + dsk_ref[:, ch] * u_b[:, ch]
        zs.append(jax.nn.gelu(y))
    z = jnp.concatenate(zs, axis=1)
    return z * jax.nn.sigmoid(_dot(z.astype(BF16), wg_ref[...]) + bg_ref[...])


def _mixer0_kernel(*refs, tt, nb, pos0, prompt):
    if prompt:
        x_ref, tail_ref, perm_ref, permt_ref = refs[:4]
        refs = refs[4:]
    else:
        x_ref = refs[0]
        refs = refs[1:]
    (g_ref, win_ref, pinit_ref, wp_ref, scale_ref, h0r_ref, h0i_ref, lbr_ref, lbi_ref, wbr_ref, wbi_ref,
     wcr_ref, wci_ref, dsk_ref, wg_ref, bg_ref, wout_ref,
     h_ref, pst_ref, str_ref, sti_ref, ext_ref, bur_ref, bui_ref, hr_ref, hi_ref) = refs
    i = pl.program_id(0)
    rows = tt * nb
    d = x_ref.shape[-1]
    dp = ext_ref.shape[1]

    @pl.when(i == 0)
    def _():
        ext_ref[0:(POOL_BUF + 1) * nb, :] = pinit_ref[...]
        hr_ref[...] = h0r_ref[...]
        hi_ref[...] = h0i_ref[...]

    x = x_ref[...]
    if prompt:
        x = jnp.where(i == 0, jnp.broadcast_to(tail_ref[...][None], x.shape), x)
    x2 = x.reshape(rows, d)
    xn = _rms(x2, g_ref[...]).astype(BF16)
    if prompt:
        xn = _dot(perm_ref[...], xn).astype(BF16)
    u = _dot(xn, win_ref[...])
    ya, pool_tail = _pool_block(u[:, :dp], i, ext_ref, wp_ref, scale_ref, tt=tt, nb=nb, pos0=pos0)
    yb = _s5_block(u[:, dp:], lbr_ref, lbi_ref, wbr_ref, wbi_ref, wcr_ref, wci_ref, dsk_ref, wg_ref, bg_ref,
                   bur_ref, bui_ref, hr_ref, hi_ref, tt=tt, nb=nb)
    y = jnp.concatenate([ya, yb], axis=1).astype(BF16)
    if prompt:
        y = _dot(permt_ref[...], y).astype(BF16)
    h_ref[...] = x + _dot(y, wout_ref[...]).reshape(x.shape)

    @pl.when(i == pl.num_programs(0) - 1)
    def _():
        pst_ref[...] = pool_tail
        str_ref[...] = hr_ref[...]
        sti_ref[...] = hi_ref[...]


def _mixer0(x, tail, g, w_in, pool_init, wp, scale, h0r, h0i, s5, w_out, *, tt, nb, pos0):
    prompt = tail is not None
    d = x.shape[-1]
    dp = wp.shape[0] * wp.shape[1]
    gn = h0r.shape[1]
    rows = tt * nb
    halo = (POOL_BUF + 1) * nb
    lbr, lbi, wbr, wbi, wcr, wci, dsk, wg, bg = s5
    if prompt:
        nblk = x.shape[1] // tt + 1
        out_rows = x.shape[1] + tt
        r = jnp.arange(rows)
        perm = (r[:, None] % nb * tt + r[:, None] // nb == r[None, :]).astype(BF16)
        lead = [x, tail, perm, perm.T]
        lead_specs = [pl.BlockSpec((nb, tt, d), lambda i: (0, jnp.maximum(i - 1, 0), 0)),
                      _full((tt, d)), _full((rows, rows)), _full((rows, rows))]
        h_spec = pl.BlockSpec((nb, tt, d), lambda i: (0, (i + nblk - 1) % nblk, 0))
        h_shape = (nb, out_rows, d)
    else:
        nblk = 1
        lead = [x]
        lead_specs = [_full(x.shape)]
        h_spec = _full(x.shape)
        h_shape = x.shape
    args = lead + [g, w_in, pool_init, wp, scale, h0r, h0i, lbr, lbi, wbr, wbi, wcr, wci, dsk, wg, bg, w_out]
    in_specs = lead_specs + [_full(a.shape) for a in args[len(lead):]]
    return pl.pallas_call(
        functools.partial(_mixer0_kernel, tt=tt, nb=nb, pos0=pos0, prompt=prompt),
        grid=(nblk,),
        in_specs=in_specs,
        out_specs=[h_spec, _full((POOL_BUF * nb, dp)), _full((nb, gn)), _full((nb, gn))],
        out_shape=[jax.ShapeDtypeStruct(h_shape, F32), jax.ShapeDtypeStruct((POOL_BUF * nb, dp), F32),
                   jax.ShapeDtypeStruct((nb, gn), F32), jax.ShapeDtypeStruct((nb, gn), F32)],
        scratch_shapes=[pltpu.VMEM((halo + rows, dp), F32), pltpu.VMEM((rows, gn), F32),
                        pltpu.VMEM((rows, gn), F32), pltpu.VMEM((nb, gn), F32), pltpu.VMEM((nb, gn), F32)],
        compiler_params=_cp("arbitrary"),
        name="mixer0",
    )(*args)


def _ffn_kernel(h_ref, g_ref, w1_ref, w3_ref, w2_ref, o_ref, *, fc):
    h = h_ref[...]
    xn = _rms(h, g_ref[...]).astype(BF16)
    acc = h
    for c in range(w1_ref.shape[1] // fc):
        cs = slice(c * fc, (c + 1) * fc)
        a = _dot(xn, w1_ref[:, cs])
        b = _dot(xn, w3_ref[:, cs])
        acc = acc + _dot((jax.nn.silu(a) * b).astype(BF16), w2_ref[cs, :])
    o_ref[...] = acc


def _ffn(h, g, w1, w3, w2, tm):
    r, d = h.shape
    f = w1.shape[1]
    fc = MXU_DIM if f % MXU_DIM == 0 else f
    return pl.pallas_call(
        functools.partial(_ffn_kernel, fc=fc),
        grid=(r // tm,),
        in_specs=[pl.BlockSpec((tm, d), lambda i: (i, 0)), _full((1, d)),
                  _resident((d, f)), _resident((d, f)), _resident((f, d))],
        out_specs=pl.BlockSpec((tm, d), lambda i: (i, 0)),
        out_shape=jax.ShapeDtypeStruct((r, d), F32),
        compiler_params=_cp("parallel"),
        name="ffn",
    )(h, g, w1, w3, w2)


def _gla_proj_kernel(h_ref, g_ref, win_ref, wg1_ref, wg2_ref, bgk_ref,
                     q_ref, k_ref, gk_ref, v_ref, gt_ref, *, q_scale):
    xn = _rms(h_ref[...], g_ref[...]).astype(BF16)
    kd = q_ref.shape[1]
    vd = v_ref.shape[1]
    q_ref[...] = _dot(xn, win_ref[:, 0:kd]) * q_scale
    k_ref[...] = _dot(xn, win_ref[:, kd:2 * kd])
    v_ref[...] = _dot(xn, win_ref[:, 2 * kd:2 * kd + vd]).astype(v_ref.dtype)
    gt_ref[...] = _dot(xn, win_ref[:, 2 * kd + vd:])
    low = _dot(xn, wg1_ref[...]).astype(BF16)
    z = _dot(low, wg2_ref[...]) + bgk_ref[...]
    log_sig = jnp.minimum(z, 0.0) - jnp.log1p(jnp.exp(-jnp.abs(z)))
    gk_ref[...] = log_sig / GLA_GATE_NORM


def _gla_proj(h, g, w_in, wg1, wg2, bgk, kd, vd, tm):
    r, d = h.shape
    rows = lambda n: pl.BlockSpec((tm, n), lambda i: (i, 0))
    return pl.pallas_call(
        functools.partial(_gla_proj_kernel, q_scale=float((kd // GLA_HEADS) ** -0.5)),
        grid=(r // tm,),
        in_specs=[rows(d), _full((1, d)), _resident(w_in.shape), _full(wg1.shape), _full(wg2.shape),
                  _full((1, kd))],
        out_specs=[rows(kd), rows(kd), rows(kd), rows(vd), rows(vd)],
        out_shape=[jax.ShapeDtypeStruct((r, kd), F32)] * 3 + [jax.ShapeDtypeStruct((r, vd), BF16),
                                                              jax.ShapeDtypeStruct((r, vd), F32)],
        compiler_params=_cp("parallel"),
        name="gla_proj",
    )(h, g, w_in, wg1, wg2, bgk)


def _split3(x):
    a = x.astype(BF16)
    r = x - a.astype(F32)
    b = r.astype(BF16)
    c = (r - b.astype(F32)).astype(BF16)
    return a, b, c


def _gla_chunk_kernel(q_ref, k_ref, gk_ref, v_ref, o_ref, s_ref, *, nchunk, rot, unroll):
    c_rows = CHUNK
    dk = q_ref.shape[2]
    nsub = c_rows // SUB
    row_i = lax.broadcasted_iota(jnp.int32, (c_rows, c_rows), 0)
    col_i = lax.broadcasted_iota(jnp.int32, (c_rows, c_rows), 1)
    tri = (row_i >= col_i).astype(BF16)
    sub_row = lax.broadcasted_iota(jnp.int32, (SUB, c_rows), 0)
    sub_col = lax.broadcasted_iota(jnp.int32, (SUB, c_rows), 1)
    key_row = lax.broadcasted_iota(jnp.int32, (c_rows, dk), 0)

    def intra(rs):
        q = q_ref[0, rs, :]
        k = k_ref[0, rs, :]
        vb = v_ref[0, rs, :]
        g1, g2, g3 = _split3(gk_ref[0, rs, :])
        bc = _dot(tri, g1) + _dot(tri, g2) + _dot(tri, g3)
        blast = bc[c_rows - 1:c_rows, :]
        q_sub, k_hat = [], []
        att_rows = []
        for i in range(nsub):
            lo = i * SUB
            bs = bc[lo - 1:lo, :] if i > 0 else jnp.zeros((1, dk), F32)
            bc_i = bc[lo:lo + SUB, :]
            q_i = q[lo:lo + SUB, :]
            k_i = k[lo:lo + SUB, :]
            q_sub.append(q_i * jnp.exp(bc_i - bs))
            if i > 0:
                k_hat.append(jnp.where(key_row < lo, k * jnp.exp(jnp.minimum(bs - bc, 0.0)), 0.0))
            diag = jnp.zeros((SUB, c_rows), F32)
            for s in range(SUB):
                e = jnp.exp(jnp.minimum(bc_i - bc_i[s:s + 1, :], 0.0))
                col = jnp.sum(q_i * k_i[s:s + 1, :] * e, axis=-1, keepdims=True)
                diag = jnp.where(sub_col == lo + s, col, diag)
            att_rows.append(jnp.where(sub_row + lo >= sub_col, diag, 0.0))
        att = jnp.concatenate(att_rows, axis=0)
        zero = jnp.zeros((SUB, dk), F32)
        lhs = jnp.concatenate(
            [jnp.concatenate([q_sub[i] if j == i else zero for j in range(1, nsub)], axis=1)
             for i in range(nsub)], axis=0)
        rhs = jnp.concatenate(k_hat, axis=1)
        att = att + _dot_nt(lhs.astype(BF16), rhs.astype(BF16))
        o_intra = _dot(att.astype(BF16), vb)
        q_in = (q * jnp.exp(bc)).astype(BF16)
        k_dec = (k * jnp.exp(blast - bc)).astype(BF16)
        return o_intra, q_in, k_dec, jnp.exp(blast), vb

    def body(it, st):
        spans = []
        for j in range(unroll):
            mem = lax.rem(it * unroll + j + rot, nchunk)
            spans.append(pl.ds(pl.multiple_of(mem * c_rows, c_rows), c_rows))
        parts = [intra(rs) for rs in spans]
        for rs, (o_intra, q_in, k_dec, decay, vb) in zip(spans, parts):
            o_ref[0, rs, :] = o_intra + _dot_nt(q_in, st.astype(BF16))
            st = st * decay + _dot_tn(vb, k_dec)
        return st

    st = lax.fori_loop(0, nchunk // unroll, body, jnp.zeros((v_ref.shape[2], dk), F32))
    s_ref[0, 0] = st.T


def _gla_chunked(q, k, gk, v, *, rot):
    b, l, kd = q.shape
    vd = v.shape[2]
    dk, dv = kd // GLA_HEADS, vd // GLA_HEADS
    seq = lambda n: pl.BlockSpec((1, l, n), lambda bi, hi: (bi, 0, hi))
    nchunk = l // CHUNK
    unroll = _first_divisor(nchunk, (3, 2, 1))
    return pl.pallas_call(
        functools.partial(_gla_chunk_kernel, nchunk=nchunk, rot=rot, unroll=unroll),
        grid=(b, GLA_HEADS),
        in_specs=[seq(dk), seq(dk), seq(dk), seq(dv)],
        out_specs=[seq(dv), pl.BlockSpec((1, 1, dk, dv), lambda bi, hi: (bi, hi, 0, 0))],
        out_shape=[jax.ShapeDtypeStruct((b, l, vd), F32),
                   jax.ShapeDtypeStruct((b, GLA_HEADS, dk, dv), F32)],
        compiler_params=_cp("parallel", "parallel"),
        name="gla_chunked",
    )(q, k, gk, v)


def _gla_step_kernel(qt_ref, kt_ref, gt_ref, v_ref, s0_ref, o_ref, s_ref):
    bt = v_ref.shape[1]
    dec = jnp.exp(gt_ref[0])
    kt = kt_ref[0]
    qt = qt_ref[0]
    vf = v_ref[0].astype(F32)
    for j in range(bt):
        v_row = vf[j:j + 1, :]
        s_new = dec[:, j:j + 1] * s0_ref[j, 0] + kt[:, j:j + 1] * v_row
        s_ref[j, 0] = s_new
        o_ref[0, j:j + 1, :] = jnp.sum(qt[:, j:j + 1] * s_new, axis=0, keepdims=True)


def _gla_step(q, k, gk, v, s0, bt=8):
    b, kd = q.shape
    vd = v.shape[1]
    dk, dv = kd // GLA_HEADS, vd // GLA_HEADS
    nt = b // bt
    cols = lambda t: t.reshape(nt, bt, kd).transpose(0, 2, 1)
    col_spec = pl.BlockSpec((1, dk, bt), lambda ti, hi: (ti, hi, 0))
    v_spec = pl.BlockSpec((1, bt, dv), lambda ti, hi: (ti, 0, hi))
    s_spec = pl.BlockSpec((bt, 1, dk, dv), lambda ti, hi: (ti, hi, 0, 0))
    o, s = pl.pallas_call(
        _gla_step_kernel,
        grid=(nt, GLA_HEADS),
        in_specs=[col_spec, col_spec, col_spec, v_spec, s_spec],
        out_specs=[v_spec, s_spec],
        out_shape=[jax.ShapeDtypeStruct((nt, bt, vd), F32), jax.ShapeDtypeStruct(s0.shape, F32)],
        compiler_params=_cp("parallel", "parallel"),
        name="gla_step",
    )(cols(q), cols(k), cols(gk), v.reshape(nt, bt, vd), s0)
    return o.reshape(b, vd), s


def _gla_out_kernel(o_ref, gt_ref, h_ref, gn_ref, w_ref, out_ref):
    dv = gn_ref.shape[1]
    parts = []
    for hd in range(o_ref.shape[2] // dv):
        sl = slice(hd * dv, (hd + 1) * dv)
        parts.append((_rms(o_ref[0, :, sl], gn_ref[...]) * jax.nn.silu(gt_ref[0, :, sl])).astype(BF16))
    out_ref[0] = h_ref[0] + _dot(jnp.concatenate(parts, axis=1), w_ref[...])


def _gla_out(o, gt, h, gnorm, w, n_rows, tm):
    b, _, d = h.shape
    vd = o.shape[2]
    rows = lambda n: pl.BlockSpec((1, tm, n), lambda bi, i: (bi, i, 0))
    return pl.pallas_call(
        _gla_out_kernel,
        grid=(b, n_rows // tm),
        in_specs=[rows(vd), rows(vd), rows(d), _full(gnorm.shape), _resident(w.shape)],
        out_specs=rows(d),
        out_shape=jax.ShapeDtypeStruct((b, n_rows, d), F32),
        compiler_params=_cp("parallel", "parallel"),
        name="gla_out",
    )(o, gt, h, gnorm, w)


def _moe_kernel(h_ref, g_ref, wr_ref, w1_ref, w3_ref, w2_ref, gf_ref, o_ref, xn_ref, gate_ref):
    e = pl.program_id(2)
    ne = pl.num_programs(2)

    @pl.when(e == 0)
    def _():
        h = h_ref[0]
        xn = _rms(h, g_ref[...])
        xb = xn.astype(BF16)
        xn_ref[...] = xb
        logits = _dot(xb, wr_ref[...])
        lane = lax.broadcasted_iota(jnp.int32, logits.shape, 1)
        valid = lane < ne
        logits = jnp.where(valid, logits, -jnp.inf)
        p = jnp.exp(logits - jnp.max(logits, axis=-1, keepdims=True))
        p = p / jnp.sum(p, axis=-1, keepdims=True)
        p = jnp.where(valid, p, -1.0)
        m1 = jnp.max(p, axis=-1, keepdims=True)
        i1 = jnp.min(jnp.where(p == m1, lane, LANES), axis=-1, keepdims=True)
        rest = jnp.where(lane == i1, -1.0, p)
        m2 = jnp.max(rest, axis=-1, keepdims=True)
        i2 = jnp.min(jnp.where(rest == m2, lane, LANES), axis=-1, keepdims=True)
        tot = m1 + m2
        gate_ref[...] = jnp.where(lane == i1, m1 / tot, jnp.where(lane == i2, m2 / tot, 0.0))
        o_ref[0] = h

    xb = xn_ref[...]
    lane = lax.broadcasted_iota(jnp.int32, gate_ref.shape, 1)
    gcol = jnp.sum(jnp.where(lane == e, gate_ref[...], 0.0), axis=-1, keepdims=True)
    a = _dot(xb, w1_ref[0])
    b = _dot(xb, w3_ref[0])
    y = _dot((jax.nn.silu(a) * b).astype(BF16), w2_ref[0])
    o_ref[0] += gcol * y

    @pl.when(e == ne - 1)
    def _():
        o_ref[0] = _rms(o_ref[0], gf_ref[...])


def _moe_final(h, g, w_router, w1, w3, w2, g_final, tm):
    b, l, d = h.shape
    ne, _, f = w1.shape
    rows = pl.BlockSpec((1, tm, d), lambda bi, i, e: (bi, i, 0))
    return pl.pallas_call(
        _moe_kernel,
        grid=(b, l // tm, ne),
        in_specs=[rows, _full((1, d)), _full(w_router.shape),
                  pl.BlockSpec((1, d, f), lambda bi, i, e: (e, 0, 0)),
                  pl.BlockSpec((1, d, f), lambda bi, i, e: (e, 0, 0)),
                  pl.BlockSpec((1, f, d), lambda bi, i, e: (e, 0, 0)), _full((1, d))],
        out_specs=rows,
        out_shape=jax.ShapeDtypeStruct((b, l, d), F32),
        scratch_shapes=[pltpu.VMEM((tm, d), BF16), pltpu.VMEM((tm, LANES), F32)],
        compiler_params=_cp("parallel", "parallel", "arbitrary"),
        name="moe_final",
    )(h, g, w_router, w1, w3, w2, g_final)


def kernel(x_prompt, x_sample, state_pool, state_s5_re, state_s5_im, state_gla, meta_tokens, norm_mix_e, w_in_e, w_pool, pool_scale, s5_a_re, s5_a_im, s5_log_dt, s5_b_re, s5_b_im, s5_c_re, s5_c_im, s5_d, w_glu, b_glu, w_out_e, norm_ffn_e, ffn_w1, ffn_w3, ffn_w2, norm_mix_o, w_in_o, w_gk1, w_gk2, b_gk, gla_norm, w_out_o, norm_ffn_o, w_router, moe_w1, moe_w3, moe_w2, norm_final):
    bp, seq, d = x_prompt.shape
    bs = x_sample.shape[0]
    lp = seq + CHUNK
    dp = w_pool.shape[1] * w_pool.shape[2]
    g_ssm, n_ssm, p_ssm = s5_b_re.shape[1:]
    gn = g_ssm * n_ssm
    kd = w_gk2.shape[2]
    vd = w_out_o.shape[1]
    row = lambda t: t.reshape(1, -1)
    bf = lambda t: t.astype(BF16)

    j = 0
    w_in_e_b, w_out_e_b = bf(w_in_e[j]), bf(w_out_e[j])
    w_pool_b, w_glu_b = bf(w_pool[j]), bf(w_glu[j])
    w1_b, w3_b, w2_b = bf(ffn_w1[j]), bf(ffn_w3[j]), bf(ffn_w2[j])
    lbr, lbi, bbr, bbi = _s5_prep(s5_a_re[j], s5_a_im[j], s5_log_dt[j], s5_b_re[j], s5_b_im[j])
    s5_args = (lbr, lbi, bf(_block_diag_in(bbr, g_ssm, n_ssm, p_ssm)), bf(_block_diag_in(bbi, g_ssm, n_ssm, p_ssm)),
               bf(_block_diag_out(s5_c_re[j])), bf(_block_diag_out(s5_c_im[j])),
               row(s5_d[j]), w_glu_b, row(b_glu[j]))
    w_in_o_b, w_out_o_b = bf(w_in_o[j]), bf(w_out_o[j])
    rank = w_gk1.shape[2]
    wg1 = bf(jnp.pad(w_gk1[j], ((0, 0), (0, LANES - rank))))
    wg2 = bf(jnp.pad(w_gk2[j], ((0, LANES - rank), (0, 0))))
    w_router_b = bf(jnp.pad(w_router[j], ((0, 0), (0, LANES - w_router.shape[2]))))
    mw1, mw3, mw2 = bf(moe_w1[j]), bf(moe_w3[j]), bf(moe_w2[j])
    mix_w = (row(norm_mix_e[j]), w_in_e_b)
    pool_w = (w_pool_b, row(pool_scale[j]))

    tail = jnp.concatenate([jnp.zeros((CHUNK - N_META, d), F32), meta_tokens.astype(F32)], axis=0)
    zero_state = jnp.zeros((bp, gn), F32)
    h1, pool_p, re_p, im_p = _mixer0(
        x_prompt, tail, *mix_w, jnp.zeros(((POOL_BUF + 1) * bp, dp), F32), *pool_w,
        zero_state, zero_state, s5_args, w_out_e_b, tt=CHUNK, nb=bp, pos0=-(CHUNK - N_META))
    tm = _first_divisor(bp * lp, (768, 512, 256, 128, 8))
    h2 = _ffn(h1.reshape(bp * lp, d), row(norm_ffn_e[j]), w1_b, w3_b, w2_b, tm)
    q, k, gk, v, gt = _gla_proj(h2, row(norm_mix_o[j]), w_in_o_b, wg1, wg2, row(b_gk[j]), kd, vd, tm)
    seq3 = lambda t: t.reshape(bp, lp, t.shape[1])
    o, gla_p = _gla_chunked(seq3(q), seq3(k), seq3(gk), seq3(v), rot=lp // CHUNK - 1)
    ts = _first_divisor(seq, (1024, 512, 256, 128, 8))
    h3 = _gla_out(o, seq3(gt), seq3(h2), row(gla_norm[j]), w_out_o_b, seq, ts)
    y_prompt = _moe_final(h3, row(norm_ffn_o[j]), w_router_b, mw1, mw3, mw2, row(norm_final), ts)

    pool_init = jnp.pad(state_pool[j].transpose(1, 0, 2), ((1, 0), (0, 0), (0, 0))).reshape((POOL_BUF + 1) * bs, dp)
    h1s, pool_s, re_s, im_s = _mixer0(
        x_sample.reshape(1, bs, d), None, *mix_w, pool_init, *pool_w,
        state_s5_re[j].reshape(bs, gn), state_s5_im[j].reshape(bs, gn), s5_args, w_out_e_b,
        tt=1, nb=bs, pos0=PAST_LEN)
    h2s = _ffn(h1s.reshape(bs, d), row(norm_ffn_e[j]), w1_b, w3_b, w2_b, bs)
    qs, ks, gks, vs, gts = _gla_proj(h2s, row(norm_mix_o[j]), w_in_o_b, wg1, wg2, row(b_gk[j]), kd, vd, bs)
    os_, gla_s = _gla_step(qs, ks, gks, vs, state_gla[j])
    h3s = _gla_out(os_[None], gts[None], h2s[None], row(gla_norm[j]), w_out_o_b, bs, bs)
    y_sample = _moe_final(h3s, row(norm_ffn_o[j]), w_router_b, mw1, mw3, mw2, row(norm_final), bs)

    tb = lambda t, nb: t.reshape(POOL_BUF, nb, dp).transpose(1, 0, 2)[None]
    ssm = lambda t, nb: t.reshape(1, nb, g_ssm, n_ssm)
    return (y_prompt, y_sample.reshape(bs, 1, d), tb(pool_p, bp), tb(pool_s, bs),
            ssm(re_p, bp), ssm(re_s, bs), ssm(im_p, bp), ssm(im_s, bs), gla_p[None], gla_s[None])
```

```python
import functools

import jax
import jax.numpy as jnp
from jax import lax
from jax.experimental import pallas as pl
from jax.experimental.pallas import tpu as pltpu

F32 = jnp.float32
BF16 = jnp.bfloat16

EPS = 1e-6
N_META = 16
PAST_LEN = 16384
POOL_WINDOWS = (2, 4, 8, 16)
POOL_BUF = max(POOL_WINDOWS) - 1
GLA_HEADS = 4
GLA_GATE_NORM = 16.0
CHUNK = 64
SUB = 16
MXU_DIM = 256
MOE_BLK = 128
MOE_CHUNK = MXU_DIM
LANES = 128
VMEM_LIMIT = 56 * 1024 * 1024


def _cp(*sem, vmem=VMEM_LIMIT):
    return pltpu.CompilerParams(dimension_semantics=sem, vmem_limit_bytes=vmem)


def _rms(x, g):
    return x * lax.rsqrt(jnp.mean(x * x, axis=-1, keepdims=True) + EPS) * g


def _dot(a, b):
    return jnp.dot(a, b, preferred_element_type=F32)


def _dot_nt(a, b):
    return lax.dot_general(a, b, (((1,), (1,)), ((), ())), preferred_element_type=F32)


def _dot_tn(a, b):
    return lax.dot_general(a, b, (((0,), (0,)), ((), ())), preferred_element_type=F32)


def _full(shape):
    return pl.BlockSpec(shape, lambda *_: (0,) * len(shape))


def _resident(shape):
    return pl.BlockSpec(shape, lambda *_: (0,) * len(shape), pipeline_mode=pl.Buffered(1))


def _first_divisor(n, candidates):
    return next(c for c in candidates if n % c == 0)


def _s5_prep_kernel(ar_ref, ai_ref, ldt_ref, br_ref, bi_ref, lbr_ref, lbi_ref, bbr_ref, bbi_ref):
    dt = jnp.exp(ldt_ref[...])
    ar = ar_ref[...]
    ai = ai_ref[...]
    mag = jnp.exp(ar * dt)
    lb_re = mag * jnp.cos(ai * dt)
    lb_im = mag * jnp.sin(ai * dt)
    den = ar * ar + ai * ai
    nr = lb_re - 1.0
    f_re = (nr * ar + lb_im * ai) / den
    f_im = (lb_im * ar - nr * ai) / den
    lbr_ref[...] = lb_re
    lbi_ref[...] = lb_im
    br = br_ref[...]
    bi = bi_ref[...]
    bbr_ref[...] = f_re * br - f_im * bi
    bbi_ref[...] = f_re * bi + f_im * br


def _s5_prep(a_re, a_im, log_dt, b_re, b_im):
    g, n, p = b_re.shape
    gn = g * n
    row = lambda t: t.reshape(1, gn)
    to_pgn = lambda t: t.transpose(2, 0, 1).reshape(p, gn)
    ldt = jnp.broadcast_to(log_dt[:, None], (g, n))
    return pl.pallas_call(
        _s5_prep_kernel,
        out_shape=[jax.ShapeDtypeStruct((1, gn), F32)] * 2 + [jax.ShapeDtypeStruct((p, gn), F32)] * 2,
        name="s5_prep",
    )(row(a_re), row(a_im), row(ldt), to_pgn(b_re), to_pgn(b_im))


def _block_diag_in(bb_pgn, g, n, p):
    bb = bb_pgn.reshape(p, g, n).transpose(1, 0, 2)
    eye = jnp.eye(g, dtype=bb.dtype)
    full = (bb[:, :, None, :] * eye[:, None, :, None]).reshape(g * p, g * n)
    nblk = g * p // MXU_DIM
    cols = g * n // nblk
    return jnp.stack([full[k * MXU_DIM:(k + 1) * MXU_DIM, k * cols:(k + 1) * cols] for k in range(nblk)])


def _block_diag_out(c_gpn):
    g, p, n = c_gpn.shape
    eye = jnp.eye(g, dtype=c_gpn.dtype)
    full = (c_gpn.transpose(0, 2, 1)[:, :, None, :] * eye[:, None, :, None]).reshape(g * n, g * p)
    nblk = g * p // MXU_DIM
    rows = g * n // nblk
    return jnp.stack([full[k * rows:(k + 1) * rows, k * MXU_DIM:(k + 1) * MXU_DIM] for k in range(nblk)])


def _pool_block(u_a, i, ext_ref, wp_ref, scale_ref, *, tt, nb, pos0):
    rows = tt * nb
    halo = (POOL_BUF + 1) * nb
    shift = nb.bit_length() - 1
    ext_ref[halo:halo + rows, :] = u_a
    t_in = lax.shift_right_logical(lax.broadcasted_iota(jnp.int32, (rows, LANES), 0), shift)
    pos1 = t_in + (pos0 + 1 + i * tt)
    group = u_a.shape[1] // len(POOL_WINDOWS)
    ys = []
    for gi, w in enumerate(POOL_WINDOWS):
        lo = gi * group
        s = ext_ref[(POOL_BUF + 2 - w) * nb:halo + rows, lo:lo + group]
        k = 1
        while k < w:
            n = s.shape[0]
            s = s[k * nb:, :] + s[:n - k * nb, :]
            k *= 2
        cnt = jnp.clip(pos1, 1, w).astype(F32)
        d = s / cnt - u_a[:, lo:lo + group]
        ys.append(_dot(d.astype(BF16), wp_ref[gi]) * scale_ref[:, lo:lo + group])
    tail = ext_ref[rows:rows + halo, :]
    ext_ref[0:halo, :] = tail
    return jnp.concatenate(ys, axis=1), tail[nb:, :]


def _s5_block(u_b, lbr_ref, lbi_ref, wbr_ref, wbi_ref, wcr_ref, wci_ref, dsk_ref, wg_ref, bg_ref,
              bur_ref, bui_ref, hr_ref, hi_ref, *, tt, nb):
    gn = bur_ref.shape[1]
    nkb = wbr_ref.shape[0]
    sb = gn // nkb
    ub = u_b.astype(BF16)
    for kb in range(nkb):
        uk = ub[:, kb * MXU_DIM:(kb + 1) * MXU_DIM]
        bur_ref[:, kb * sb:(kb + 1) * sb] = _dot(uk, wbr_ref[kb])
        bui_ref[:, kb * sb:(kb + 1) * sb] = _dot(uk, wbi_ref[kb])

    cw = 4 * LANES
    for c in range(gn // cw):
        cols = slice(c * cw, (c + 1) * cw)
        lr = jnp.broadcast_to(lbr_ref[:, cols], (nb, cw))
        li = jnp.broadcast_to(lbi_ref[:, cols], (nb, cw))

        def step(t, carry, cols=cols, lr=lr, li=li):
            h_re, h_im = carry
            r = pl.multiple_of(t * nb, nb)
            n_re = lr * h_re - li * h_im + bur_ref[pl.ds(r, nb), cols]
            n_im = lr * h_im + li * h_re + bui_ref[pl.ds(r, nb), cols]
            bur_ref[pl.ds(r, nb), cols] = n_re
            bui_ref[pl.ds(r, nb), cols] = n_im
            return n_re, n_im

        h_re, h_im = lax.fori_loop(0, tt, step, (hr_ref[:, cols], hi_ref[:, cols]),
                                   unroll=min(tt, 8))
        hr_ref[:, cols] = h_re
        hi_ref[:, cols] = h_im

    zs = []
    for kb in range(nkb):
        hrb = bur_ref[:, kb * sb:(kb + 1) * sb].astype(BF16)
        hib = bui_ref[:, kb * sb:(kb + 1) * sb].astype(BF16)
        ch = slice(kb * MXU_DIM, (kb + 1) * MXU_DIM)
        y = _dot(hrb, wcr_ref[kb]) - _dot(hib, wci_ref[kb]) + dsk_ref[:, ch] * u_b[:, ch]
        zs.append(jax.nn.gelu(y))
    z = jnp.concatenate(zs, axis=1)
    return z * jax.nn.sigmoid(_dot(z.astype(BF16), wg_ref[...]) + bg_ref[...])


def _mixer0_kernel(*refs, tt, nb, pos0, prompt):
    if prompt:
        x_ref, tail_ref, perm_ref, permt_ref = refs[:4]
        refs = refs[4:]
    else:
        x_ref = refs[0]
        refs = refs[1:]
    (g_ref, win_ref, pinit_ref, wp_ref, scale_ref, h0r_ref, h0i_ref, lbr_ref, lbi_ref, wbr_ref, wbi_ref,
     wcr_ref, wci_ref, dsk_ref, wg_ref, bg_ref, wout_ref,
     h_ref, pst_ref, str_ref, sti_ref, ext_ref, bur_ref, bui_ref, hr_ref, hi_ref) = refs
    i = pl.program_id(0)
    rows = tt * nb
    d = x_ref.shape[-1]
    dp = ext_ref.shape[1]

    @pl.when(i == 0)
    def _():
        ext_ref[0:(POOL_BUF + 1) * nb, :] = pinit_ref[...]
        hr_ref[...] = h0r_ref[...]
        hi_ref[...] = h0i_ref[...]

    x = x_ref[...]
    if prompt:
        x = jnp.where(i == 0, jnp.broadcast_to(tail_ref[...][None], x.shape), x)
    x2 = x.reshape(rows, d)
    xn = _rms(x2, g_ref[...]).astype(BF16)
    if prompt:
        xn = _dot(perm_ref[...], xn).astype(BF16)
    u = _dot(xn, win_ref[...])
    ya, pool_tail = _pool_block(u[:, :dp], i, ext_ref, wp_ref, scale_ref, tt=tt, nb=nb, pos0=pos0)
    yb = _s5_block(u[:, dp:], lbr_ref, lbi_ref, wbr_ref, wbi_ref, wcr_ref, wci_ref, dsk_ref, wg_ref, bg_ref,
                   bur_ref, bui_ref, hr_ref, hi_ref, tt=tt, nb=nb)
    y = jnp.concatenate([ya, yb], axis=1).astype(BF16)
    if prompt:
        y = _dot(permt_ref[...], y).astype(BF16)
    h_ref[...] = x + _dot(y, wout_ref[...]).reshape(x.shape)

    @pl.when(i == pl.num_programs(0) - 1)
    def _():
        pst_ref[...] = pool_tail
        str_ref[...] = hr_ref[...]
        sti_ref[...] = hi_ref[...]


def _mixer0(x, tail, g, w_in, pool_init, wp, scale, h0r, h0i, s5, w_out, *, tt, nb, pos0):
    prompt = tail is not None
    d = x.shape[-1]
    dp = wp.shape[0] * wp.shape[1]
    gn = h0r.shape[1]
    rows = tt * nb
    halo = (POOL_BUF + 1) * nb
    lbr, lbi, wbr, wbi, wcr, wci, dsk, wg, bg = s5
    if prompt:
        nblk = x.shape[1] // tt + 1
        out_rows = x.shape[1] + tt
        r = jnp.arange(rows)
        perm = (r[:, None] % nb * tt + r[:, None] // nb == r[None, :]).astype(BF16)
        lead = [x, tail, perm, perm.T]
        lead_specs = [pl.BlockSpec((nb, tt, d), lambda i: (0, jnp.maximum(i - 1, 0), 0)),
                      _full((tt, d)), _full((rows, rows)), _full((rows, rows))]
        h_spec = pl.BlockSpec((nb, tt, d), lambda i: (0, (i + nblk - 1) % nblk, 0))
        h_shape = (nb, out_rows, d)
    else:
        nblk = 1
        lead = [x]
        lead_specs = [_full(x.shape)]
        h_spec = _full(x.shape)
        h_shape = x.shape
    args = lead + [g, w_in, pool_init, wp, scale, h0r, h0i, lbr, lbi, wbr, wbi, wcr, wci, dsk, wg, bg, w_out]
    in_specs = lead_specs + [_full(a.shape) for a in args[len(lead):]]
    return pl.pallas_call(
        functools.partial(_mixer0_kernel, tt=tt, nb=nb, pos0=pos0, prompt=prompt),
        grid=(nblk,),
        in_specs=in_specs,
        out_specs=[h_spec, _full((POOL_BUF * nb, dp)), _full((nb, gn)), _full((nb, gn))],
        out_shape=[jax.ShapeDtypeStruct(h_shape, F32), jax.ShapeDtypeStruct((POOL_BUF * nb, dp), F32),
                   jax.ShapeDtypeStruct((nb, gn), F32), jax.ShapeDtypeStruct((nb, gn), F32)],
        scratch_shapes=[pltpu.VMEM((halo + rows, dp), F32), pltpu.VMEM((rows, gn), F32),
                        pltpu.VMEM((rows, gn), F32), pltpu.VMEM((nb, gn), F32), pltpu.VMEM((nb, gn), F32)],
        compiler_params=_cp("arbitrary"),
        name="mixer0",
    )(*args)


def _ffn_kernel(h_ref, g_ref, w1_ref, w3_ref, w2_ref, o_ref, *, fc):
    h = h_ref[...]
    xn = _rms(h, g_ref[...]).astype(BF16)
    acc = h
    for c in range(w1_ref.shape[1] // fc):
        cs = slice(c * fc, (c + 1) * fc)
        a = _dot(xn, w1_ref[:, cs])
        b = _dot(xn, w3_ref[:, cs])
        acc = acc + _dot((jax.nn.silu(a) * b).astype(BF16), w2_ref[cs, :])
    o_ref[...] = acc


def _ffn(h, g, w1, w3, w2, tm):
    r, d = h.shape
    f = w1.shape[1]
    fc = MXU_DIM if f % MXU_DIM == 0 else f
    return pl.pallas_call(
        functools.partial(_ffn_kernel, fc=fc),
        grid=(r // tm,),
        in_specs=[pl.BlockSpec((tm, d), lambda i: (i, 0)), _full((1, d)),
                  _resident((d, f)), _resident((d, f)), _resident((f, d))],
        out_specs=pl.BlockSpec((tm, d), lambda i: (i, 0)),
        out_shape=jax.ShapeDtypeStruct((r, d), F32),
        compiler_params=_cp("parallel"),
        name="ffn",
    )(h, g, w1, w3, w2)


def _gla_proj_kernel(h_ref, g_ref, win_ref, wg1_ref, wg2_ref, bgk_ref,
                     q_ref, k_ref, gk_ref, v_ref, gt_ref, *, q_scale):
    xn = _rms(h_ref[...], g_ref[...]).astype(BF16)
    kd = q_ref.shape[1]
    vd = v_ref.shape[1]
    q_ref[...] = _dot(xn, win_ref[:, 0:kd]) * q_scale
    k_ref[...] = _dot(xn, win_ref[:, kd:2 * kd])
    v_ref[...] = _dot(xn, win_ref[:, 2 * kd:2 * kd + vd]).astype(v_ref.dtype)
    gt_ref[...] = _dot(xn, win_ref[:, 2 * kd + vd:])
    low = _dot(xn, wg1_ref[...]).astype(BF16)
    z = _dot(low, wg2_ref[...]) + bgk_ref[...]
    log_sig = jnp.minimum(z, 0.0) - jnp.log1p(jnp.exp(-jnp.abs(z)))
    gk_ref[...] = log_sig / GLA_GATE_NORM


def _gla_proj(h, g, w_in, wg1, wg2, bgk, kd, vd, tm):
    r, d = h.shape
    rows = lambda n: pl.BlockSpec((tm, n), lambda i: (i, 0))
    return pl.pallas_call(
        functools.partial(_gla_proj_kernel, q_scale=float((kd // GLA_HEADS) ** -0.5)),
        grid=(r // tm,),
        in_specs=[rows(d), _full((1, d)), _resident(w_in.shape), _full(wg1.shape), _full(wg2.shape),
                  _full((1, kd))],
        out_specs=[rows(kd), rows(kd), rows(kd), rows(vd), rows(vd)],
        out_shape=[jax.ShapeDtypeStruct((r, kd), F32)] * 3 + [jax.ShapeDtypeStruct((r, vd), BF16),
                                                              jax.ShapeDtypeStruct((r, vd), F32)],
        compiler_params=_cp("parallel"),
        name="gla_proj",
    )(h, g, w_in, wg1, wg2, bgk)


def _split3(x):
    a = x.astype(BF16)
    r = x - a.astype(F32)
    b = r.astype(BF16)
    c = (r - b.astype(F32)).astype(BF16)
    return a, b, c


def _gla_chunk_kernel(q_ref, k_ref, gk_ref, v_ref, o_ref, s_ref, *, nchunk, rot, unroll):
    c_rows = CHUNK
    dk = q_ref.shape[2]
    nsub = c_rows // SUB
    row_i = lax.broadcasted_iota(jnp.int32, (c_rows, c_rows), 0)
    col_i = lax.broadcasted_iota(jnp.int32, (c_rows, c_rows), 1)
    tri = (row_i >= col_i).astype(BF16)
    sub_row = lax.broadcasted_iota(jnp.int32, (SUB, c_rows), 0)
    sub_col = lax.broadcasted_iota(jnp.int32, (SUB, c_rows), 1)
    key_row = lax.broadcasted_iota(jnp.int32, (c_rows, dk), 0)

    def intra(rs):
        q = q_ref[0, rs, :]
        k = k_ref[0, rs, :]
        vb = v_ref[0, rs, :]
        g1, g2, g3 = _split3(gk_ref[0, rs, :])
        bc = _dot(tri, g1) + _dot(tri, g2) + _dot(tri, g3)
        blast = bc[c_rows - 1:c_rows, :]
        q_sub, k_hat = [], []
        att_rows = []
        for i in range(nsub):
            lo = i * SUB
            bs = bc[lo - 1:lo, :] if i > 0 else jnp.zeros((1, dk), F32)
            bc_i = bc[lo:lo + SUB, :]
            q_i = q[lo:lo + SUB, :]
            k_i = k[lo:lo + SUB, :]
            q_sub.append(q_i * jnp.exp(bc_i - bs))
            if i > 0:
                k_hat.append(jnp.where(key_row < lo, k * jnp.exp(jnp.minimum(bs - bc, 0.0)), 0.0))
            diag = jnp.zeros((SUB, c_rows), F32)
            for s in range(SUB):
                e = jnp.exp(jnp.minimum(bc_i - bc_i[s:s + 1, :], 0.0))
                col = jnp.sum(q_i * k_i[s:s + 1, :] * e, axis=-1, keepdims=True)
                diag = jnp.where(sub_col == lo + s, col, diag)
            att_rows.append(jnp.where(sub_row + lo >= sub_col, diag, 0.0))
        att = jnp.concatenate(att_rows, axis=0)
        zero = jnp.zeros((SUB, dk), F32)
        lhs = jnp.concatenate(
            [jnp.concatenate([q_sub[i] if j == i else zero for j in range(1, nsub)], axis=1)
             for i in range(nsub)], axis=0)
        rhs = jnp.concatenate(k_hat, axis=1)
        att = att + _dot_nt(lhs.astype(BF16), rhs.astype(BF16))
        o_intra = _dot(att.astype(BF16), vb)
        q_in = (q * jnp.exp(bc)).astype(BF16)
        k_dec = (k * jnp.exp(blast - bc)).astype(BF16)
        return o_intra, q_in, k_dec, jnp.exp(blast), vb

    def body(it, st):
        spans = []
        for j in range(unroll):
            mem = lax.rem(it * unroll + j + rot, nchunk)
            spans.append(pl.ds(pl.multiple_of(mem * c_rows, c_rows), c_rows))
        parts = [intra(rs) for rs in spans]
        for rs, (o_intra, q_in, k_dec, decay, vb) in zip(spans, parts):
            o_ref[0, rs, :] = o_intra + _dot_nt(q_in, st.astype(BF16))
            st = st * decay + _dot_tn(vb, k_dec)
        return st

    st = lax.fori_loop(0, nchunk // unroll, body, jnp.zeros((v_ref.shape[2], dk), F32))
    s_ref[0, 0] = st.T


def _gla_chunked(q, k, gk, v, *, rot):
    b, l, kd = q.shape
    vd = v.shape[2]
    dk, dv = kd // GLA_HEADS, vd // GLA_HEADS
    seq = lambda n: pl.BlockSpec((1, l, n), lambda bi, hi: (bi, 0, hi))
    nchunk = l // CHUNK
    unroll = _first_divisor(nchunk, (3, 2, 1))
    return pl.pallas_call(
        functools.partial(_gla_chunk_kernel, nchunk=nchunk, rot=rot, unroll=unroll),
        grid=(b, GLA_HEADS),
        in_specs=[seq(dk), seq(dk), seq(dk), seq(dv)],
        out_specs=[seq(dv), pl.BlockSpec((1, 1, dk, dv), lambda bi, hi: (bi, hi, 0, 0))],
        out_shape=[jax.ShapeDtypeStruct((b, l, vd), F32),
                   jax.ShapeDtypeStruct((b, GLA_HEADS, dk, dv), F32)],
        compiler_params=_cp("parallel", "parallel"),
        name="gla_chunked",
    )(q, k, gk, v)


def _gla_step_kernel(qt_ref, kt_ref, gt_ref, v_ref, s0_ref, o_ref, s_ref):
    bt = v_ref.shape[1]
    dec = jnp.exp(gt_ref[0])
    kt = kt_ref[0]
    qt = qt_ref[0]
    vf = v_ref[0].astype(F32)
    for j in range(bt):
        v_row = vf[j:j + 1, :]
        s_new = dec[:, j:j + 1] * s0_ref[j, 0] + kt[:, j:j + 1] * v_row
        s_ref[j, 0] = s_new
        o_ref[0, j:j + 1, :] = jnp.sum(qt[:, j:j + 1] * s_new, axis=0, keepdims=True)


def _gla_step(q, k, gk, v, s0, bt=8):
    b, kd = q.shape
    vd = v.shape[1]
    dk, dv = kd // GLA_HEADS, vd // GLA_HEADS
    nt = b // bt
    cols = lambda t: t.reshape(nt, bt, kd).transpose(0, 2, 1)
    col_spec = pl.BlockSpec((1, dk, bt), lambda ti, hi: (ti, hi, 0))
    v_spec = pl.BlockSpec((1, bt, dv), lambda ti, hi: (ti, 0, hi))
    s_spec = pl.BlockSpec((bt, 1, dk, dv), lambda ti, hi: (ti, hi, 0, 0))
    o, s = pl.pallas_call(
        _gla_step_kernel,
        grid=(nt, GLA_HEADS),
        in_specs=[col_spec, col_spec, col_spec, v_spec, s_spec],
        out_specs=[v_spec, s_spec],
        out_shape=[jax.ShapeDtypeStruct((nt, bt, vd), F32), jax.ShapeDtypeStruct(s0.shape, F32)],
        compiler_params=_cp("parallel", "parallel"),
        name="gla_step",
    )(cols(q), cols(k), cols(gk), v.reshape(nt, bt, vd), s0)
    return o.reshape(b, vd), s


def _gla_out_kernel(o_ref, gt_ref, h_ref, gn_ref, w_ref, out_ref):
    dv = gn_ref.shape[1]
    parts = []
    for hd in range(o_ref.shape[2] // dv):
        sl = slice(hd * dv, (hd + 1) * dv)
        parts.append((_rms(o_ref[0, :, sl], gn_ref[...]) * jax.nn.silu(gt_ref[0, :, sl])).astype(BF16))
    out_ref[0] = h_ref[0] + _dot(jnp.concatenate(parts, axis=1), w_ref[...])


def _gla_out(o, gt, h, gnorm, w, n_rows, tm):
    b, _, d = h.shape
    vd = o.shape[2]
    rows = lambda n: pl.BlockSpec((1, tm, n), lambda bi, i: (bi, i, 0))
    return pl.pallas_call(
        _gla_out_kernel,
        grid=(b, n_rows // tm),
        in_specs=[rows(vd), rows(vd), rows(d), _full(gnorm.shape), _resident(w.shape)],
        out_specs=rows(d),
        out_shape=jax.ShapeDtypeStruct((b, n_rows, d), F32),
        compiler_params=_cp("parallel", "parallel"),
        name="gla_out",
    )(o, gt, h, gnorm, w)


def _moe_kernel(h_ref, g_ref, wr_ref, tri_ref, w1_ref, w3_ref, w2_ref, gf_ref, o_ref,
                pt_ref, sl_ref, gg_ref, meta_ref, *, ne):
    e = pl.program_id(2)
    t = h_ref.shape[1]
    nchunk_max = pt_ref.shape[0]

    @pl.when(e == 0)
    def _():
        h = h_ref[0]
        xn = _rms(h, g_ref[...])
        xb = xn.astype(BF16)
        logits = _dot(xb, wr_ref[...])
        lane = lax.broadcasted_iota(jnp.int32, logits.shape, 1)
        valid = lane < ne
        logits = jnp.where(valid, logits, -jnp.inf)
        p = jnp.exp(logits - jnp.max(logits, axis=-1, keepdims=True))
        p = p / jnp.sum(p, axis=-1, keepdims=True)
        p = jnp.where(valid, p, -1.0)
        m1 = jnp.max(p, axis=-1, keepdims=True)
        i1 = jnp.min(jnp.where(p == m1, lane, LANES), axis=-1, keepdims=True)
        rest = jnp.where(lane == i1, -1.0, p)
        m2 = jnp.max(rest, axis=-1, keepdims=True)
        i2 = jnp.min(jnp.where(rest == m2, lane, LANES), axis=-1, keepdims=True)
        tot = m1 + m2
        sel = (lane == i1) | (lane == i2)
        gates = jnp.where(lane == i1, m1 / tot, jnp.where(lane == i2, m2 / tot, 0.0))

        incl = _dot(tri_ref[...], jnp.where(sel, 1.0, 0.0).astype(BF16))
        blocks = jnp.floor((incl[t - 1:t, :] + (MOE_BLK - 1)) * (1.0 / MOE_BLK))
        upper = (lax.broadcasted_iota(jnp.int32, (LANES, LANES), 0)
                 < lax.broadcasted_iota(jnp.int32, (LANES, LANES), 1)).astype(BF16)
        off = _dot(jnp.broadcast_to(blocks, (8, LANES)).astype(BF16), upper)[0:1, :]
        slot = jnp.where(sel, off * MOE_BLK + incl - 1.0, -1.0)
        s1 = jnp.sum(jnp.where(lane == i1, slot, 0.0), axis=-1, keepdims=True)
        s2 = jnp.sum(jnp.where(lane == i2, slot, 0.0), axis=-1, keepdims=True)
        g_hi, g_mid, g_lo = _split3(gates)
        nchunks = jnp.floor((jnp.sum(blocks) + 1.0) * 0.5).astype(jnp.int32)
        for c in range(nchunk_max):
            @pl.when(c < nchunks)
            def _(c=c):
                ids = (lax.broadcasted_iota(jnp.int32, (t, MOE_CHUNK), 1) + c * MOE_CHUNK).astype(F32)
                ptc = (jnp.where(s1 == ids, 1.0, 0.0) + jnp.where(s2 == ids, 1.0, 0.0)).astype(BF16)
                rs = slice(c * MOE_CHUNK, (c + 1) * MOE_CHUNK)
                pt_ref[c] = ptc
                sl_ref[rs, :] = _dot_tn(ptc, xb).astype(BF16)
                gg_ref[rs, :] = _dot_tn(ptc, g_hi) + _dot_tn(ptc, g_mid) + _dot_tn(ptc, g_lo)
        o_ref[0] = h
        for ee in range(ne):
            meta_ref[0, ee] = blocks[0, ee].astype(jnp.int32)
            meta_ref[1, ee] = off[0, ee].astype(jnp.int32)
        meta_ref[2, 0] = nchunks

    lane_b = lax.broadcasted_iota(jnp.int32, (MOE_BLK, LANES), 1)
    first = meta_ref[1, e]

    def expert_block(jb, carry):
        rs = pl.ds(pl.multiple_of((first + jb) * MOE_BLK, MOE_BLK), MOE_BLK)
        xg = sl_ref[rs, :]
        a = _dot(xg, w1_ref[0])
        b = _dot(xg, w3_ref[0])
        y = _dot((jax.nn.silu(a) * b).astype(BF16), w2_ref[0])
        gate = jnp.sum(jnp.where(lane_b == e, gg_ref[rs, :], 0.0), axis=-1, keepdims=True)
        sl_ref[rs, :] = (gate * y).astype(BF16)
        return carry

    lax.fori_loop(0, meta_ref[0, e], expert_block, 0)

    @pl.when(e == ne - 1)
    def _():
        def combine(c, carry):
            rs = pl.ds(pl.multiple_of(c * MOE_CHUNK, MOE_CHUNK), MOE_CHUNK)
            o_ref[0] += _dot(pt_ref[c], sl_ref[rs, :])
            return carry

        lax.fori_loop(0, meta_ref[2, 0], combine, 0)
        o_ref[0] = _rms(o_ref[0], gf_ref[...])


def _moe_final(h, g, w_router, w1, w3, w2, g_final, tm):
    b, l, d = h.shape
    ne, _, f = w1.shape
    slots = -(-(2 * tm + ne * MOE_BLK) // MOE_CHUNK) * MOE_CHUNK
    r = jnp.arange(tm)
    tri = (r[:, None] >= r[None, :]).astype(BF16)
    rows = pl.BlockSpec((1, tm, d), lambda bi, i, e: (bi, i, 0))
    return pl.pallas_call(
        functools.partial(_moe_kernel, ne=ne),
        grid=(b, l // tm, ne),
        in_specs=[rows, _full((1, d)), _full(w_router.shape), _resident((tm, tm)),
                  pl.BlockSpec((1, d, f), lambda bi, i, e: (e, 0, 0)),
                  pl.BlockSpec((1, d, f), lambda bi, i, e: (e, 0, 0)),
                  pl.BlockSpec((1, f, d), lambda bi, i, e: (e, 0, 0)), _full((1, d))],
        out_specs=rows,
        out_shape=jax.ShapeDtypeStruct((b, l, d), F32),
        scratch_shapes=[pltpu.VMEM((slots // MOE_CHUNK, tm, MOE_CHUNK), BF16), pltpu.VMEM((slots, d), BF16),
                        pltpu.VMEM((slots, LANES), F32),
                        pltpu.SMEM((3, ne), jnp.int32)],
        compiler_params=_cp("parallel", "parallel", "arbitrary"),
        name="moe_final",
    )(h, g, w_router, tri, w1, w3, w2, g_final)


def kernel(x_prompt, x_sample, state_pool, state_s5_re, state_s5_im, state_gla, meta_tokens, norm_mix_e, w_in_e, w_pool, pool_scale, s5_a_re, s5_a_im, s5_log_dt, s5_b_re, s5_b_im, s5_c_re, s5_c_im, s5_d, w_glu, b_glu, w_out_e, norm_ffn_e, ffn_w1, ffn_w3, ffn_w2, norm_mix_o, w_in_o, w_gk1, w_gk2, b_gk, gla_norm, w_out_o, norm_ffn_o, w_router, moe_w1, moe_w3, moe_w2, norm_final):
    bp, seq, d = x_prompt.shape
    bs = x_sample.shape[0]
    lp = seq + CHUNK
    dp = w_pool.shape[1] * w_pool.shape[2]
    g_ssm, n_ssm, p_ssm = s5_b_re.shape[1:]
    gn = g_ssm * n_ssm
    kd = w_gk2.shape[2]
    vd = w_out_o.shape[1]
    row = lambda t: t.reshape(1, -1)
    bf = lambda t: t.astype(BF16)

    j = 0
    w_in_e_b, w_out_e_b = bf(w_in_e[j]), bf(w_out_e[j])
    w_pool_b, w_glu_b = bf(w_pool[j]), bf(w_glu[j])
    w1_b, w3_b, w2_b = bf(ffn_w1[j]), bf(ffn_w3[j]), bf(ffn_w2[j])
    lbr, lbi, bbr, bbi = _s5_prep(s5_a_re[j], s5_a_im[j], s5_log_dt[j], s5_b_re[j], s5_b_im[j])
    s5_args = (lbr, lbi, bf(_block_diag_in(bbr, g_ssm, n_ssm, p_ssm)), bf(_block_diag_in(bbi, g_ssm, n_ssm, p_ssm)),
               bf(_block_diag_out(s5_c_re[j])), bf(_block_diag_out(s5_c_im[j])),
               row(s5_d[j]), w_glu_b, row(b_glu[j]))
    w_in_o_b, w_out_o_b = bf(w_in_o[j]), bf(w_out_o[j])
    rank = w_gk1.shape[2]
    wg1 = bf(jnp.pad(w_gk1[j], ((0, 0), (0, LANES - rank))))
    wg2 = bf(jnp.pad(w_gk2[j], ((0, LANES - rank), (0, 0))))
    w_router_b = bf(jnp.pad(w_router[j], ((0, 0), (0, LANES - w_router.shape[2]))))
    mw1, mw3, mw2 = bf(moe_w1[j]), bf(moe_w3[j]), bf(moe_w2[j])
    mix_w = (row(norm_mix_e[j]), w_in_e_b)
    pool_w = (w_pool_b, row(pool_scale[j]))

    tail = jnp.concatenate([jnp.zeros((CHUNK - N_META, d), F32), meta_tokens.astype(F32)], axis=0)
    zero_state = jnp.zeros((bp, gn), F32)
    h1, pool_p, re_p, im_p = _mixer0(
        x_prompt, tail, *mix_w, jnp.zeros(((POOL_BUF + 1) * bp, dp), F32), *pool_w,
        zero_state, zero_state, s5_args, w_out_e_b, tt=CHUNK, nb=bp, pos0=-(CHUNK - N_META))
    tm = _first_divisor(bp * lp, (768, 512, 256, 128, 8))
    h2 = _ffn(h1.reshape(bp * lp, d), row(norm_ffn_e[j]), w1_b, w3_b, w2_b, tm)
    q, k, gk, v, gt = _gla_proj(h2, row(norm_mix_o[j]), w_in_o_b, wg1, wg2, row(b_gk[j]), kd, vd, tm)
    seq3 = lambda t: t.reshape(bp, lp, t.shape[1])
    o, gla_p = _gla_chunked(seq3(q), seq3(k), seq3(gk), seq3(v), rot=lp // CHUNK - 1)
    ts = _first_divisor(seq, (1024, 512, 256, 128, 8))
    h3 = _gla_out(o, seq3(gt), seq3(h2), row(gla_norm[j]), w_out_o_b, seq, ts)
    y_prompt = _moe_final(h3, row(norm_ffn_o[j]), w_router_b, mw1, mw3, mw2, row(norm_final), ts)

    pool_init = jnp.pad(state_pool[j].transpose(1, 0, 2), ((1, 0), (0, 0), (0, 0))).reshape((POOL_BUF + 1) * bs, dp)
    h1s, pool_s, re_s, im_s = _mixer0(
        x_sample.reshape(1, bs, d), None, *mix_w, pool_init, *pool_w,
        state_s5_re[j].reshape(bs, gn), state_s5_im[j].reshape(bs, gn), s5_args, w_out_e_b,
        tt=1, nb=bs, pos0=PAST_LEN)
    h2s = _ffn(h1s.reshape(bs, d), row(norm_ffn_e[j]), w1_b, w3_b, w2_b, bs)
    qs, ks, gks, vs, gts = _gla_proj(h2s, row(norm_mix_o[j]), w_in_o_b, wg1, wg2, row(b_gk[j]), kd, vd, bs)
    os_, gla_s = _gla_step(qs, ks, gks, vs, state_gla[j])
    h3s = _gla_out(os_[None], gts[None], h2s[None], row(gla_norm[j]), w_out_o_b, bs, bs)
    y_sample = _moe_final(h3s, row(norm_ffn_o[j]), w_router_b, mw1, mw3, mw2, row(norm_final), bs)

    tb = lambda t, nb: t.reshape(POOL_BUF, nb, dp).transpose(1, 0, 2)[None]
    ssm = lambda t, nb: t.reshape(1, nb, g_ssm, n_ssm)
    return (y_prompt, y_sample.reshape(bs, 1, d), tb(pool_p, bp), tb(pool_s, bs),
            ssm(re_p, bp), ssm(re_s, bs), ssm(im_p, bp), ssm(im_s, bs), gla_p[None], gla_s[None])
```

```python
import functools

import jax
import jax.numpy as jnp
from jax import lax
from jax.experimental import pallas as pl
from jax.experimental.pallas import tpu as pltpu

F32 = jnp.float32
BF16 = jnp.bfloat16

EPS = 1e-6
LOG2_E = 1.4426950408889634
N_META = 16
PAST_LEN = 16384
POOL_WINDOWS = (2, 4, 8, 16)
POOL_BUF = max(POOL_WINDOWS) - 1
GLA_HEADS = 4
GLA_GATE_NORM = 16.0
CHUNK = 64
SUB = 16
MXU_DIM = 256
MOE_BLK = 128
MOE_CHUNK = MXU_DIM
LANES = 128
VMEM_LIMIT = 56 * 1024 * 1024


def _cp(*sem, vmem=VMEM_LIMIT):
    return pltpu.CompilerParams(dimension_semantics=sem, vmem_limit_bytes=vmem)


def _rms(x, g):
    return x * lax.rsqrt(jnp.mean(x * x, axis=-1, keepdims=True) + EPS) * g


def _dot(a, b):
    return jnp.dot(a, b, preferred_element_type=F32)


def _dot_nt(a, b):
    return lax.dot_general(a, b, (((1,), (1,)), ((), ())), preferred_element_type=F32)


def _dot_tn(a, b):
    return lax.dot_general(a, b, (((0,), (0,)), ((), ())), preferred_element_type=F32)


def _full(shape):
    return pl.BlockSpec(shape, lambda *_: (0,) * len(shape))


def _resident(shape):
    return pl.BlockSpec(shape, lambda *_: (0,) * len(shape), pipeline_mode=pl.Buffered(1))


def _first_divisor(n, candidates):
    return next(c for c in candidates if n % c == 0)


def _s5_prep_kernel(ar_ref, ai_ref, ldt_ref, br_ref, bi_ref, lbr_ref, lbi_ref, bbr_ref, bbi_ref):
    dt = jnp.exp(ldt_ref[...])
    ar = ar_ref[...]
    ai = ai_ref[...]
    mag = jnp.exp(ar * dt)
    lb_re = mag * jnp.cos(ai * dt)
    lb_im = mag * jnp.sin(ai * dt)
    den = ar * ar + ai * ai
    nr = lb_re - 1.0
    f_re = (nr * ar + lb_im * ai) / den
    f_im = (lb_im * ar - nr * ai) / den
    lbr_ref[...] = lb_re
    lbi_ref[...] = lb_im
    br = br_ref[...]
    bi = bi_ref[...]
    bbr_ref[...] = f_re * br - f_im * bi
    bbi_ref[...] = f_re * bi + f_im * br


def _s5_prep(a_re, a_im, log_dt, b_re, b_im):
    g, n, p = b_re.shape
    gn = g * n
    row = lambda t: t.reshape(1, gn)
    to_pgn = lambda t: t.transpose(2, 0, 1).reshape(p, gn)
    ldt = jnp.broadcast_to(log_dt[:, None], (g, n))
    return pl.pallas_call(
        _s5_prep_kernel,
        out_shape=[jax.ShapeDtypeStruct((1, gn), F32)] * 2 + [jax.ShapeDtypeStruct((p, gn), F32)] * 2,
        name="s5_prep",
    )(row(a_re), row(a_im), row(ldt), to_pgn(b_re), to_pgn(b_im))


def _block_diag_in(bb_pgn, g, n, p):
    bb = bb_pgn.reshape(p, g, n).transpose(1, 0, 2)
    eye = jnp.eye(g, dtype=bb.dtype)
    full = (bb[:, :, None, :] * eye[:, None, :, None]).reshape(g * p, g * n)
    nblk = g * p // MXU_DIM
    cols = g * n // nblk
    return jnp.stack([full[k * MXU_DIM:(k + 1) * MXU_DIM, k * cols:(k + 1) * cols] for k in range(nblk)])


def _block_diag_out(c_gpn):
    g, p, n = c_gpn.shape
    eye = jnp.eye(g, dtype=c_gpn.dtype)
    full = (c_gpn.transpose(0, 2, 1)[:, :, None, :] * eye[:, None, :, None]).reshape(g * n, g * p)
    nblk = g * p // MXU_DIM
    rows = g * n // nblk
    return jnp.stack([full[k * rows:(k + 1) * rows, k * MXU_DIM:(k + 1) * MXU_DIM] for k in range(nblk)])


def _pool_block(u_a, i, ext_ref, wp_ref, scale_ref, *, tt, nb, pos0):
    rows = tt * nb
    halo = (POOL_BUF + 1) * nb
    shift = nb.bit_length() - 1
    ext_ref[halo:halo + rows, :] = u_a
    t_in = lax.shift_right_logical(lax.broadcasted_iota(jnp.int32, (rows, LANES), 0), shift)
    pos1 = t_in + (pos0 + 1 + i * tt)
    group = u_a.shape[1] // len(POOL_WINDOWS)
    ys = []
    for gi, w in enumerate(POOL_WINDOWS):
        lo = gi * group
        s = ext_ref[(POOL_BUF + 2 - w) * nb:halo + rows, lo:lo + group]
        k = 1
        while k < w:
            n = s.shape[0]
            s = s[k * nb:, :] + s[:n - k * nb, :]
            k *= 2
        cnt = jnp.clip(pos1, 1, w).astype(F32)
        d = s / cnt - u_a[:, lo:lo + group]
        ys.append(_dot(d.astype(BF16), wp_ref[gi]) * scale_ref[:, lo:lo + group])
    tail = ext_ref[rows:rows + halo, :]
    ext_ref[0:halo, :] = tail
    return jnp.concatenate(ys, axis=1), tail[nb:, :]


def _s5_block(u_b, lbr_ref, lbi_ref, wbr_ref, wbi_ref, wcr_ref, wci_ref, dsk_ref, wg_ref, bg_ref,
              bur_ref, bui_ref, hr_ref, hi_ref, *, tt, nb):
    gn = bur_ref.shape[1]
    nkb = wbr_ref.shape[0]
    sb = gn // nkb
    ub = u_b.astype(BF16)
    for kb in range(nkb):
        uk = ub[:, kb * MXU_DIM:(kb + 1) * MXU_DIM]
        bur_ref[:, kb * sb:(kb + 1) * sb] = _dot(uk, wbr_ref[kb])
        bui_ref[:, kb * sb:(kb + 1) * sb] = _dot(uk, wbi_ref[kb])

    cw = 4 * LANES
    for c in range(gn // cw):
        cols = slice(c * cw, (c + 1) * cw)
        lr = jnp.broadcast_to(lbr_ref[:, cols], (nb, cw))
        li = jnp.broadcast_to(lbi_ref[:, cols], (nb, cw))

        def step(t, carry, cols=cols, lr=lr, li=li):
            h_re, h_im = carry
            r = pl.multiple_of(t * nb, nb)
            n_re = lr * h_re - li * h_im + bur_ref[pl.ds(r, nb), cols]
            n_im = lr * h_im + li * h_re + bui_ref[pl.ds(r, nb), cols]
            bur_ref[pl.ds(r, nb), cols] = n_re
            bui_ref[pl.ds(r, nb), cols] = n_im
            return n_re, n_im

        h_re, h_im = lax.fori_loop(0, tt, step, (hr_ref[:, cols], hi_ref[:, cols]),
                                   unroll=min(tt, 8))
        hr_ref[:, cols] = h_re
        hi_ref[:, cols] = h_im

    zs = []
    for kb in range(nkb):
        hrb = bur_ref[:, kb * sb:(kb + 1) * sb].astype(BF16)
        hib = bui_ref[:, kb * sb:(kb + 1) * sb].astype(BF16)
        ch = slice(kb * MXU_DIM, (kb + 1) * MXU_DIM)
        y = _dot(hrb, wcr_ref[kb]) - _dot(hib, wci_ref[kb]) + dsk_ref[:, ch] * u_b[:, ch]
        zs.append(jax.nn.gelu(y))
    z = jnp.concatenate(zs, axis=1)
    return z * jax.nn.sigmoid(_dot(z.astype(BF16), wg_ref[...]) + bg_ref[...])


def _mixer0_kernel(*refs, tt, nb, pos0, prompt):
    if prompt:
        x_ref, tail_ref, perm_ref, permt_ref = refs[:4]
        refs = refs[4:]
    else:
        x_ref = refs[0]
        refs = refs[1:]
    (g_ref, win_ref, pinit_ref, wp_ref, scale_ref, h0r_ref, h0i_ref, lbr_ref, lbi_ref, wbr_ref, wbi_ref,
     wcr_ref, wci_ref, dsk_ref, wg_ref, bg_ref, wout_ref,
     h_ref, pst_ref, str_ref, sti_ref, ext_ref, bur_ref, bui_ref, hr_ref, hi_ref) = refs
    i = pl.program_id(0)
    rows = tt * nb
    d = x_ref.shape[-1]
    dp = ext_ref.shape[1]

    @pl.when(i == 0)
    def _():
        ext_ref[0:(POOL_BUF + 1) * nb, :] = pinit_ref[...]
        hr_ref[...] = h0r_ref[...]
        hi_ref[...] = h0i_ref[...]

    x = x_ref[...]
    if prompt:
        x = jnp.where(i == 0, jnp.broadcast_to(tail_ref[...][None], x.shape), x)
    x2 = x.reshape(rows, d)
    xn = _rms(x2, g_ref[...]).astype(BF16)
    if prompt:
        xn = _dot(perm_ref[...], xn).astype(BF16)
    u = _dot(xn, win_ref[...])
    ya, pool_tail = _pool_block(u[:, :dp], i, ext_ref, wp_ref, scale_ref, tt=tt, nb=nb, pos0=pos0)
    yb = _s5_block(u[:, dp:], lbr_ref, lbi_ref, wbr_ref, wbi_ref, wcr_ref, wci_ref, dsk_ref, wg_ref, bg_ref,
                   bur_ref, bui_ref, hr_ref, hi_ref, tt=tt, nb=nb)
    y = jnp.concatenate([ya, yb], axis=1).astype(BF16)
    if prompt:
        y = _dot(permt_ref[...], y).astype(BF16)
    h_ref[...] = x + _dot(y, wout_ref[...]).reshape(x.shape)

    @pl.when(i == pl.num_programs(0) - 1)
    def _():
        pst_ref[...] = pool_tail
        str_ref[...] = hr_ref[...]
        sti_ref[...] = hi_ref[...]


def _mixer0(x, tail, g, w_in, pool_init, wp, scale, h0r, h0i, s5, w_out, *, tt, nb, pos0):
    prompt = tail is not None
    d = x.shape[-1]
    dp = wp.shape[0] * wp.shape[1]
    gn = h0r.shape[1]
    rows = tt * nb
    halo = (POOL_BUF + 1) * nb
    lbr, lbi, wbr, wbi, wcr, wci, dsk, wg, bg = s5
    if prompt:
        nblk = x.shape[1] // tt + 1
        out_rows = x.shape[1] + tt
        r = jnp.arange(rows)
        perm = (r[:, None] % nb * tt + r[:, None] // nb == r[None, :]).astype(BF16)
        lead = [x, tail, perm, perm.T]
        lead_specs = [pl.BlockSpec((nb, tt, d), lambda i: (0, jnp.maximum(i - 1, 0), 0)),
                      _full((tt, d)), _full((rows, rows)), _full((rows, rows))]
        h_spec = pl.BlockSpec((nb, tt, d), lambda i: (0, (i + nblk - 1) % nblk, 0))
        h_shape = (nb, out_rows, d)
    else:
        nblk = 1
        lead = [x]
        lead_specs = [_full(x.shape)]
        h_spec = _full(x.shape)
        h_shape = x.shape
    args = lead + [g, w_in, pool_init, wp, scale, h0r, h0i, lbr, lbi, wbr, wbi, wcr, wci, dsk, wg, bg, w_out]
    in_specs = lead_specs + [_full(a.shape) for a in args[len(lead):]]
    return pl.pallas_call(
        functools.partial(_mixer0_kernel, tt=tt, nb=nb, pos0=pos0, prompt=prompt),
        grid=(nblk,),
        in_specs=in_specs,
        out_specs=[h_spec, _full((POOL_BUF * nb, dp)), _full((nb, gn)), _full((nb, gn))],
        out_shape=[jax.ShapeDtypeStruct(h_shape, F32), jax.ShapeDtypeStruct((POOL_BUF * nb, dp), F32),
                   jax.ShapeDtypeStruct((nb, gn), F32), jax.ShapeDtypeStruct((nb, gn), F32)],
        scratch_shapes=[pltpu.VMEM((halo + rows, dp), F32), pltpu.VMEM((rows, gn), F32),
                        pltpu.VMEM((rows, gn), F32), pltpu.VMEM((nb, gn), F32), pltpu.VMEM((nb, gn), F32)],
        compiler_params=_cp("arbitrary"),
        name="mixer0",
    )(*args)


def _ffn_kernel(h_ref, g_ref, w1_ref, w3_ref, w2_ref, o_ref, *, fc):
    h = h_ref[...]
    xn = _rms(h, g_ref[...]).astype(BF16)
    acc = h
    for c in range(w1_ref.shape[1] // fc):
        cs = slice(c * fc, (c + 1) * fc)
        a = _dot(xn, w1_ref[:, cs])
        b = _dot(xn, w3_ref[:, cs])
        acc = acc + _dot((jax.nn.silu(a) * b).astype(BF16), w2_ref[cs, :])
    o_ref[...] = acc


def _ffn(h, g, w1, w3, w2, tm):
    r, d = h.shape
    f = w1.shape[1]
    fc = MXU_DIM if f % MXU_DIM == 0 else f
    return pl.pallas_call(
        functools.partial(_ffn_kernel, fc=fc),
        grid=(r // tm,),
        in_specs=[pl.BlockSpec((tm, d), lambda i: (i, 0)), _full((1, d)),
                  _resident((d, f)), _resident((d, f)), _resident((f, d))],
        out_specs=pl.BlockSpec((tm, d), lambda i: (i, 0)),
        out_shape=jax.ShapeDtypeStruct((r, d), F32),
        compiler_params=_cp("parallel"),
        name="ffn",
    )(h, g, w1, w3, w2)


def _gla_proj_kernel(h_ref, g_ref, win_ref, wg1_ref, wg2_ref, bgk_ref,
                     q_ref, k_ref, gk_ref, v_ref, gt_ref, *, q_scale):
    xn = _rms(h_ref[...], g_ref[...]).astype(BF16)
    kd = q_ref.shape[1]
    vd = v_ref.shape[1]
    q_ref[...] = _dot(xn, win_ref[:, 0:kd]) * q_scale
    k_ref[...] = _dot(xn, win_ref[:, kd:2 * kd])
    v_ref[...] = _dot(xn, win_ref[:, 2 * kd:2 * kd + vd]).astype(v_ref.dtype)
    gt_ref[...] = _dot(xn, win_ref[:, 2 * kd + vd:])
    low = _dot(xn, wg1_ref[...]).astype(BF16)
    z = _dot(low, wg2_ref[...]) + bgk_ref[...]
    log_sig = jnp.minimum(z, 0.0) - jnp.log1p(jnp.exp(-jnp.abs(z)))
    gk_ref[...] = log_sig / GLA_GATE_NORM


def _gla_proj(h, g, w_in, wg1, wg2, bgk, kd, vd, tm):
    r, d = h.shape
    rows = lambda n: pl.BlockSpec((tm, n), lambda i: (i, 0))
    return pl.pallas_call(
        functools.partial(_gla_proj_kernel, q_scale=float((kd // GLA_HEADS) ** -0.5)),
        grid=(r // tm,),
        in_specs=[rows(d), _full((1, d)), _resident(w_in.shape), _full(wg1.shape), _full(wg2.shape),
                  _full((1, kd))],
        out_specs=[rows(kd), rows(kd), rows(kd), rows(vd), rows(vd)],
        out_shape=[jax.ShapeDtypeStruct((r, kd), F32)] * 3 + [jax.ShapeDtypeStruct((r, vd), BF16),
                                                              jax.ShapeDtypeStruct((r, vd), F32)],
        compiler_params=_cp("parallel"),
        name="gla_proj",
    )(h, g, w_in, wg1, wg2, bgk)


def _split3(x):
    a = x.astype(BF16)
    r = x - a.astype(F32)
    b = r.astype(BF16)
    c = (r - b.astype(F32)).astype(BF16)
    return a, b, c


def _gla_chunk_kernel(q_ref, k_ref, gk_ref, v_ref, o_ref, s_ref, *, nchunk, rot, unroll):
    c_rows = CHUNK
    dk = q_ref.shape[2]
    nsub = c_rows // SUB
    row_i = lax.broadcasted_iota(jnp.int32, (c_rows, c_rows), 0)
    col_i = lax.broadcasted_iota(jnp.int32, (c_rows, c_rows), 1)
    tri = (row_i >= col_i).astype(BF16)
    sub_row = lax.broadcasted_iota(jnp.int32, (SUB, c_rows), 0)
    sub_col = lax.broadcasted_iota(jnp.int32, (SUB, c_rows), 1)
    half_col = lax.broadcasted_iota(jnp.int32, (SUB // 2, c_rows), 1)

    def intra(rs):
        q = q_ref[0, rs, :]
        k = k_ref[0, rs, :]
        vb = v_ref[0, rs, :]
        g1, g2, g3 = _split3(gk_ref[0, rs, :])
        bc = (_dot(tri, g1) + _dot(tri, g2) + _dot(tri, g3)) * LOG2_E
        blast = bc[c_rows - 1:c_rows, :]
        q_sub, k_hat = [], []
        att_rows = []
        half = SUB // 2
        for i in range(nsub):
            lo = i * SUB
            bs = bc[lo - 1:lo, :] if i > 0 else jnp.zeros((1, dk), F32)
            bc_i = bc[lo:lo + SUB, :]
            q_i = q[lo:lo + SUB, :]
            k_i = k[lo:lo + SUB, :]
            q_sub.append(q_i * jnp.exp2(bc_i - bs))
            if i > 0:
                k_hat.append(jnp.concatenate(
                    [k[:lo, :] * jnp.exp2(bs - bc[:lo, :]), jnp.zeros((c_rows - lo, dk), F32)], axis=0))
            top = jnp.zeros((half, c_rows), F32)
            bot = jnp.zeros((half, c_rows), F32)
            for s in range(SUB):
                r0 = 0 if s < half else half
                e = jnp.exp2(jnp.minimum(bc_i[r0:, :] - bc_i[s:s + 1, :], 0.0))
                col = jnp.sum(q_i[r0:, :] * k_i[s:s + 1, :] * e, axis=-1, keepdims=True)
                if s < half:
                    top = jnp.where(half_col == lo + s, col[:half, :], top)
                bot = jnp.where(half_col == lo + s, col[half - r0:, :], bot)
            diag = jnp.concatenate([top, bot], axis=0)
            att_rows.append(jnp.where(sub_row + lo >= sub_col, diag, 0.0))
        att = jnp.concatenate(att_rows, axis=0)
        zero = jnp.zeros((SUB, dk), F32)
        lhs = jnp.concatenate(
            [jnp.concatenate([q_sub[i] if j == i else zero for j in range(1, nsub)], axis=1)
             for i in range(nsub)], axis=0)
        rhs = jnp.concatenate(k_hat, axis=1)
        att = att + _dot_nt(lhs.astype(BF16), rhs.astype(BF16))
        o_intra = _dot(att.astype(BF16), vb)
        q_in = (q * jnp.exp2(bc)).astype(BF16)
        k_dec = (k * jnp.exp2(blast - bc)).astype(BF16)
        return o_intra, q_in, k_dec, jnp.exp2(blast), vb

    def body(it, st):
        spans = []
        for j in range(unroll):
            mem = lax.rem(it * unroll + j + rot, nchunk)
            spans.append(pl.ds(pl.multiple_of(mem * c_rows, c_rows), c_rows))
        parts = [intra(rs) for rs in spans]
        for rs, (o_intra, q_in, k_dec, decay, vb) in zip(spans, parts):
            o_ref[0, rs, :] = o_intra + _dot_nt(q_in, st.astype(BF16))
            st = st * decay + _dot_tn(vb, k_dec)
        return st

    st = lax.fori_loop(0, nchunk // unroll, body, jnp.zeros((v_ref.shape[2], dk), F32))
    s_ref[0, 0] = st.T


def _gla_chunked(q, k, gk, v, *, rot):
    b, l, kd = q.shape
    vd = v.shape[2]
    dk, dv = kd // GLA_HEADS, vd // GLA_HEADS
    seq = lambda n: pl.BlockSpec((1, l, n), lambda bi, hi: (bi, 0, hi))
    nchunk = l // CHUNK
    unroll = _first_divisor(nchunk, (11, 3, 2, 1))
    return pl.pallas_call(
        functools.partial(_gla_chunk_kernel, nchunk=nchunk, rot=rot, unroll=unroll),
        grid=(b, GLA_HEADS),
        in_specs=[seq(dk), seq(dk), seq(dk), seq(dv)],
        out_specs=[seq(dv), pl.BlockSpec((1, 1, dk, dv), lambda bi, hi: (bi, hi, 0, 0))],
        out_shape=[jax.ShapeDtypeStruct((b, l, vd), F32),
                   jax.ShapeDtypeStruct((b, GLA_HEADS, dk, dv), F32)],
        compiler_params=_cp("parallel", "parallel"),
        name="gla_chunked",
    )(q, k, gk, v)


def _gla_step_kernel(qt_ref, kt_ref, gt_ref, v_ref, s0_ref, o_ref, s_ref):
    bt = v_ref.shape[1]
    dec = jnp.exp(gt_ref[0])
    kt = kt_ref[0]
    qt = qt_ref[0]
    vf = v_ref[0].astype(F32)
    for j in range(bt):
        v_row = vf[j:j + 1, :]
        s_new = dec[:, j:j + 1] * s0_ref[j, 0] + kt[:, j:j + 1] * v_row
        s_ref[j, 0] = s_new
        o_ref[0, j:j + 1, :] = jnp.sum(qt[:, j:j + 1] * s_new, axis=0, keepdims=True)


def _gla_step(q, k, gk, v, s0, bt=32):
    b, kd = q.shape
    vd = v.shape[1]
    dk, dv = kd // GLA_HEADS, vd // GLA_HEADS
    nt = b // bt
    cols = lambda t: t.reshape(nt, bt, kd).transpose(0, 2, 1)
    col_spec = pl.BlockSpec((1, dk, bt), lambda ti, hi: (ti, hi, 0))
    v_spec = pl.BlockSpec((1, bt, dv), lambda ti, hi: (ti, 0, hi))
    s_spec = pl.BlockSpec((bt, 1, dk, dv), lambda ti, hi: (ti, hi, 0, 0))
    o, s = pl.pallas_call(
        _gla_step_kernel,
        grid=(nt, GLA_HEADS),
        in_specs=[col_spec, col_spec, col_spec, v_spec, s_spec],
        out_specs=[v_spec, s_spec],
        out_shape=[jax.ShapeDtypeStruct((nt, bt, vd), F32), jax.ShapeDtypeStruct(s0.shape, F32)],
        compiler_params=_cp("parallel", "parallel"),
        name="gla_step",
    )(cols(q), cols(k), cols(gk), v.reshape(nt, bt, vd), s0)
    return o.reshape(b, vd), s


def _gla_out_kernel(o_ref, gt_ref, h_ref, gn_ref, w_ref, out_ref):
    dv = gn_ref.shape[1]
    parts = []
    for hd in range(o_ref.shape[2] // dv):
        sl = slice(hd * dv, (hd + 1) * dv)
        parts.append((_rms(o_ref[0, :, sl], gn_ref[...]) * jax.nn.silu(gt_ref[0, :, sl])).astype(BF16))
    out_ref[0] = h_ref[0] + _dot(jnp.concatenate(parts, axis=1), w_ref[...])


def _gla_out(o, gt, h, gnorm, w, n_rows, tm):
    b, _, d = h.shape
    vd = o.shape[2]
    rows = lambda n: pl.BlockSpec((1, tm, n), lambda bi, i: (bi, i, 0))
    return pl.pallas_call(
        _gla_out_kernel,
        grid=(b, n_rows // tm),
        in_specs=[rows(vd), rows(vd), rows(d), _full(gnorm.shape), _resident(w.shape)],
        out_specs=rows(d),
        out_shape=jax.ShapeDtypeStruct((b, n_rows, d), F32),
        compiler_params=_cp("parallel", "parallel"),
        name="gla_out",
    )(o, gt, h, gnorm, w)


def _moe_kernel(h_ref, g_ref, wr_ref, tri_ref, w1_ref, w3_ref, w2_ref, gf_ref, o_ref,
                pt_ref, sl_ref, gg_ref, meta_ref, *, ne):
    e = pl.program_id(2)
    t = h_ref.shape[1]
    nchunk_max = pt_ref.shape[0]

    @pl.when(e == 0)
    def _():
        h = h_ref[0]
        xn = _rms(h, g_ref[...])
        xb = xn.astype(BF16)
        logits = _dot(xb, wr_ref[...])
        lane = lax.broadcasted_iota(jnp.int32, logits.shape, 1)
        valid = lane < ne
        logits = jnp.where(valid, logits, -jnp.inf)
        p = jnp.exp(logits - jnp.max(logits, axis=-1, keepdims=True))
        p = p / jnp.sum(p, axis=-1, keepdims=True)
        p = jnp.where(valid, p, -1.0)
        m1 = jnp.max(p, axis=-1, keepdims=True)
        i1 = jnp.min(jnp.where(p == m1, lane, LANES), axis=-1, keepdims=True)
        rest = jnp.where(lane == i1, -1.0, p)
        m2 = jnp.max(rest, axis=-1, keepdims=True)
        i2 = jnp.min(jnp.where(rest == m2, lane, LANES), axis=-1, keepdims=True)
        tot = m1 + m2
        sel = (lane == i1) | (lane == i2)
        gates = jnp.where(lane == i1, m1 / tot, jnp.where(lane == i2, m2 / tot, 0.0))

        incl = _dot(tri_ref[...], jnp.where(sel, 1.0, 0.0).astype(BF16))
        blocks = jnp.floor((incl[t - 1:t, :] + (MOE_BLK - 1)) * (1.0 / MOE_BLK))
        upper = (lax.broadcasted_iota(jnp.int32, (LANES, LANES), 0)
                 < lax.broadcasted_iota(jnp.int32, (LANES, LANES), 1)).astype(BF16)
        off = _dot(jnp.broadcast_to(blocks, (8, LANES)).astype(BF16), upper)[0:1, :]
        slot = jnp.where(sel, off * MOE_BLK + incl - 1.0, -1.0)
        s1 = jnp.sum(jnp.where(lane == i1, slot, 0.0), axis=-1, keepdims=True)
        s2 = jnp.sum(jnp.where(lane == i2, slot, 0.0), axis=-1, keepdims=True)
        d = xb.shape[1]
        src = jnp.concatenate([xb, *_split3(gates)], axis=1)
        nchunks = jnp.floor((jnp.sum(blocks) + 1.0) * 0.5).astype(jnp.int32)
        for c in range(nchunk_max):
            @pl.when(c < nchunks)
            def _(c=c):
                ids = (lax.broadcasted_iota(jnp.int32, (t, MOE_CHUNK), 1) + c * MOE_CHUNK).astype(F32)
                ptc = (jnp.where(s1 == ids, 1.0, 0.0) + jnp.where(s2 == ids, 1.0, 0.0)).astype(BF16)
                rs = slice(c * MOE_CHUNK, (c + 1) * MOE_CHUNK)
                pt_ref[c] = ptc
                got = _dot_tn(ptc, src)
                sl_ref[rs, :] = got[:, :d].astype(BF16)
                gg_ref[rs, :] = (got[:, d:d + LANES] + got[:, d + LANES:d + 2 * LANES]
                                 + got[:, d + 2 * LANES:])
        o_ref[0] = h
        for ee in range(ne):
            meta_ref[0, ee] = blocks[0, ee].astype(jnp.int32)
            meta_ref[1, ee] = off[0, ee].astype(jnp.int32)
        meta_ref[2, 0] = nchunks

    lane_b = lax.broadcasted_iota(jnp.int32, (MOE_BLK, LANES), 1)
    first = meta_ref[1, e]

    def expert_block(jb, carry):
        rs = pl.ds(pl.multiple_of((first + jb) * MOE_BLK, MOE_BLK), MOE_BLK)
        xg = sl_ref[rs, :]
        a = _dot(xg, w1_ref[0])
        b = _dot(xg, w3_ref[0])
        y = _dot((jax.nn.silu(a) * b).astype(BF16), w2_ref[0])
        gate = jnp.sum(jnp.where(lane_b == e, gg_ref[rs, :], 0.0), axis=-1, keepdims=True)
        sl_ref[rs, :] = (gate * y).astype(BF16)
        return carry

    lax.fori_loop(0, meta_ref[0, e], expert_block, 0)

    @pl.when(e == ne - 1)
    def _():
        def combine(c, carry):
            rs = pl.ds(pl.multiple_of(c * MOE_CHUNK, MOE_CHUNK), MOE_CHUNK)
            o_ref[0] += _dot(pt_ref[c], sl_ref[rs, :])
            return carry

        lax.fori_loop(0, meta_ref[2, 0], combine, 0)
        o_ref[0] = _rms(o_ref[0], gf_ref[...])


def _moe_final(h, g, w_router, w1, w3, w2, g_final, tm):
    b, l, d = h.shape
    ne, _, f = w1.shape
    slots = -(-(2 * tm + ne * MOE_BLK) // MOE_CHUNK) * MOE_CHUNK
    r = jnp.arange(tm)
    tri = (r[:, None] >= r[None, :]).astype(BF16)
    rows = pl.BlockSpec((1, tm, d), lambda bi, i, e: (bi, i, 0))
    return pl.pallas_call(
        functools.partial(_moe_kernel, ne=ne),
        grid=(b, l // tm, ne),
        in_specs=[rows, _full((1, d)), _full(w_router.shape), _resident((tm, tm)),
                  pl.BlockSpec((1, d, f), lambda bi, i, e: (e, 0, 0)),
                  pl.BlockSpec((1, d, f), lambda bi, i, e: (e, 0, 0)),
                  pl.BlockSpec((1, f, d), lambda bi, i, e: (e, 0, 0)), _full((1, d))],
        out_specs=rows,
        out_shape=jax.ShapeDtypeStruct((b, l, d), F32),
        scratch_shapes=[pltpu.VMEM((slots // MOE_CHUNK, tm, MOE_CHUNK), BF16), pltpu.VMEM((slots, d), BF16),
                        pltpu.VMEM((slots, LANES), F32),
                        pltpu.SMEM((3, ne), jnp.int32)],
        compiler_params=_cp("parallel", "parallel", "arbitrary"),
        name="moe_final",
    )(h, g, w_router, tri, w1, w3, w2, g_final)


def kernel(x_prompt, x_sample, state_pool, state_s5_re, state_s5_im, state_gla, meta_tokens, norm_mix_e, w_in_e, w_pool, pool_scale, s5_a_re, s5_a_im, s5_log_dt, s5_b_re, s5_b_im, s5_c_re, s5_c_im, s5_d, w_glu, b_glu, w_out_e, norm_ffn_e, ffn_w1, ffn_w3, ffn_w2, norm_mix_o, w_in_o, w_gk1, w_gk2, b_gk, gla_norm, w_out_o, norm_ffn_o, w_router, moe_w1, moe_w3, moe_w2, norm_final):
    bp, seq, d = x_prompt.shape
    bs = x_sample.shape[0]
    lp = seq + CHUNK
    dp = w_pool.shape[1] * w_pool.shape[2]
    g_ssm, n_ssm, p_ssm = s5_b_re.shape[1:]
    gn = g_ssm * n_ssm
    kd = w_gk2.shape[2]
    vd = w_out_o.shape[1]
    row = lambda t: t.reshape(1, -1)
    bf = lambda t: t.astype(BF16)

    j = 0
    w_in_e_b, w_out_e_b = bf(w_in_e[j]), bf(w_out_e[j])
    w_pool_b, w_glu_b = bf(w_pool[j]), bf(w_glu[j])
    w1_b, w3_b, w2_b = bf(ffn_w1[j]), bf(ffn_w3[j]), bf(ffn_w2[j])
    lbr, lbi, bbr, bbi = _s5_prep(s5_a_re[j], s5_a_im[j], s5_log_dt[j], s5_b_re[j], s5_b_im[j])
    s5_args = (lbr, lbi, bf(_block_diag_in(bbr, g_ssm, n_ssm, p_ssm)), bf(_block_diag_in(bbi, g_ssm, n_ssm, p_ssm)),
               bf(_block_diag_out(s5_c_re[j])), bf(_block_diag_out(s5_c_im[j])),
               row(s5_d[j]), w_glu_b, row(b_glu[j]))
    w_in_o_b, w_out_o_b = bf(w_in_o[j]), bf(w_out_o[j])
    rank = w_gk1.shape[2]
    wg1 = bf(jnp.pad(w_gk1[j], ((0, 0), (0, LANES - rank))))
    wg2 = bf(jnp.pad(w_gk2[j], ((0, LANES - rank), (0, 0))))
    w_router_b = bf(jnp.pad(w_router[j], ((0, 0), (0, LANES - w_router.shape[2]))))
    mw1, mw3, mw2 = bf(moe_w1[j]), bf(moe_w3[j]), bf(moe_w2[j])
    mix_w = (row(norm_mix_e[j]), w_in_e_b)
    pool_w = (w_pool_b, row(pool_scale[j]))

    tail = jnp.concatenate([jnp.zeros((CHUNK - N_META, d), F32), meta_tokens.astype(F32)], axis=0)
    zero_state = jnp.zeros((bp, gn), F32)
    h1, pool_p, re_p, im_p = _mixer0(
        x_prompt, tail, *mix_w, jnp.zeros(((POOL_BUF + 1) * bp, dp), F32), *pool_w,
        zero_state, zero_state, s5_args, w_out_e_b, tt=CHUNK, nb=bp, pos0=-(CHUNK - N_META))
    tm = _first_divisor(bp * lp, (768, 512, 256, 128, 8))
    h2 = _ffn(h1.reshape(bp * lp, d), row(norm_ffn_e[j]), w1_b, w3_b, w2_b, tm)
    q, k, gk, v, gt = _gla_proj(h2, row(norm_mix_o[j]), w_in_o_b, wg1, wg2, row(b_gk[j]), kd, vd, tm)
    seq3 = lambda t: t.reshape(bp, lp, t.shape[1])
    o, gla_p = _gla_chunked(seq3(q), seq3(k), seq3(gk), seq3(v), rot=lp // CHUNK - 1)
    ts = _first_divisor(seq, (1024, 512, 256, 128, 8))
    h3 = _gla_out(o, seq3(gt), seq3(h2), row(gla_norm[j]), w_out_o_b, seq, ts)
    y_prompt = _moe_final(h3, row(norm_ffn_o[j]), w_router_b, mw1, mw3, mw2, row(norm_final), ts)

    pool_init = jnp.pad(state_pool[j].transpose(1, 0, 2), ((1, 0), (0, 0), (0, 0))).reshape((POOL_BUF + 1) * bs, dp)
    h1s, pool_s, re_s, im_s = _mixer0(
        x_sample.reshape(1, bs, d), None, *mix_w, pool_init, *pool_w,
        state_s5_re[j].reshape(bs, gn), state_s5_im[j].reshape(bs, gn), s5_args, w_out_e_b,
        tt=1, nb=bs, pos0=PAST_LEN)
    h2s = _ffn(h1s.reshape(bs, d), row(norm_ffn_e[j]), w1_b, w3_b, w2_b, bs)
    qs, ks, gks, vs, gts = _gla_proj(h2s, row(norm_mix_o[j]), w_in_o_b, wg1, wg2, row(b_gk[j]), kd, vd, bs)
    os_, gla_s = _gla_step(qs, ks, gks, vs, state_gla[j])
    h3s = _gla_out(os_[None], gts[None], h2s[None], row(gla_norm[j]), w_out_o_b, bs, bs)
    y_sample = _moe_final(h3s, row(norm_ffn_o[j]), w_router_b, mw1, mw3, mw2, row(norm_final), bs)

    tb = lambda t, nb: t.reshape(POOL_BUF, nb, dp).transpose(1, 0, 2)[None]
    ssm = lambda t, nb: t.reshape(1, nb, g_ssm, n_ssm)
    return (y_prompt, y_sample.reshape(bs, 1, d), tb(pool_p, bp), tb(pool_s, bs),
            ssm(re_p, bp), ssm(re_s, bs), ssm(im_p, bp), ssm(im_s, bs), gla_p[None], gla_s[None])
```

```python
import functools

import jax
import jax.numpy as jnp
from jax import lax
from jax.experimental import pallas as pl
from jax.experimental.pallas import tpu as pltpu

F32 = jnp.float32
BF16 = jnp.bfloat16

EPS = 1e-6
LOG2_E = 1.4426950408889634
N_META = 16
PAST_LEN = 16384
POOL_WINDOWS = (2, 4, 8, 16)
POOL_BUF = max(POOL_WINDOWS) - 1
GLA_HEADS = 4
GLA_GATE_NORM = 16.0
CHUNK = 64
SUB = 16
MXU_DIM = 256
MOE_BLK = 128
MOE_CHUNK = MXU_DIM
MOE_ROUTE_ROWS = 16
LANES = 128
VMEM_LIMIT = 56 * 1024 * 1024


def _cp(*sem, vmem=VMEM_LIMIT):
    return pltpu.CompilerParams(dimension_semantics=sem, vmem_limit_bytes=vmem)


def _rms(x, g):
    return x * lax.rsqrt(jnp.mean(x * x, axis=-1, keepdims=True) + EPS) * g


def _dot(a, b):
    return jnp.dot(a, b, preferred_element_type=F32)


def _dot_nt(a, b):
    return lax.dot_general(a, b, (((1,), (1,)), ((), ())), preferred_element_type=F32)


def _dot_tn(a, b):
    return lax.dot_general(a, b, (((0,), (0,)), ((), ())), preferred_element_type=F32)


def _full(shape):
    return pl.BlockSpec(shape, lambda *_: (0,) * len(shape))


def _resident(shape):
    return pl.BlockSpec(shape, lambda *_: (0,) * len(shape), pipeline_mode=pl.Buffered(1))


def _first_divisor(n, candidates):
    return next(c for c in candidates if n % c == 0)


def _s5_prep_kernel(ar_ref, ai_ref, ldt_ref, br_ref, bi_ref, lbr_ref, lbi_ref, bbr_ref, bbi_ref):
    dt = jnp.exp(ldt_ref[...])
    ar = ar_ref[...]
    ai = ai_ref[...]
    mag = jnp.exp(ar * dt)
    lb_re = mag * jnp.cos(ai * dt)
    lb_im = mag * jnp.sin(ai * dt)
    den = ar * ar + ai * ai
    nr = lb_re - 1.0
    f_re = (nr * ar + lb_im * ai) / den
    f_im = (lb_im * ar - nr * ai) / den
    lbr_ref[...] = lb_re
    lbi_ref[...] = lb_im
    br = br_ref[...]
    bi = bi_ref[...]
    bbr_ref[...] = f_re * br - f_im * bi
    bbi_ref[...] = f_re * bi + f_im * br


def _s5_prep(a_re, a_im, log_dt, b_re, b_im):
    g, n, p = b_re.shape
    gn = g * n
    row = lambda t: t.reshape(1, gn)
    to_pgn = lambda t: t.transpose(2, 0, 1).reshape(p, gn)
    ldt = jnp.broadcast_to(log_dt[:, None], (g, n))
    return pl.pallas_call(
        _s5_prep_kernel,
        out_shape=[jax.ShapeDtypeStruct((1, gn), F32)] * 2 + [jax.ShapeDtypeStruct((p, gn), F32)] * 2,
        name="s5_prep",
    )(row(a_re), row(a_im), row(ldt), to_pgn(b_re), to_pgn(b_im))


def _block_diag_in(bb_pgn, g, n, p):
    bb = bb_pgn.reshape(p, g, n).transpose(1, 0, 2)
    eye = jnp.eye(g, dtype=bb.dtype)
    full = (bb[:, :, None, :] * eye[:, None, :, None]).reshape(g * p, g * n)
    nblk = g * p // MXU_DIM
    cols = g * n // nblk
    return jnp.stack([full[k * MXU_DIM:(k + 1) * MXU_DIM, k * cols:(k + 1) * cols] for k in range(nblk)])


def _block_diag_out(c_gpn):
    g, p, n = c_gpn.shape
    eye = jnp.eye(g, dtype=c_gpn.dtype)
    full = (c_gpn.transpose(0, 2, 1)[:, :, None, :] * eye[:, None, :, None]).reshape(g * n, g * p)
    nblk = g * p // MXU_DIM
    rows = g * n // nblk
    return jnp.stack([full[k * rows:(k + 1) * rows, k * MXU_DIM:(k + 1) * MXU_DIM] for k in range(nblk)])


def _pool_block(u_a, i, ext_ref, wp_ref, scale_ref, *, tt, nb, pos0):
    rows = tt * nb
    halo = (POOL_BUF + 1) * nb
    shift = nb.bit_length() - 1
    ext_ref[halo:halo + rows, :] = u_a
    t_in = lax.shift_right_logical(lax.broadcasted_iota(jnp.int32, (rows, LANES), 0), shift)
    pos1 = t_in + (pos0 + 1 + i * tt)
    group = u_a.shape[1] // len(POOL_WINDOWS)
    ys = []
    for gi, w in enumerate(POOL_WINDOWS):
        lo = gi * group
        s = ext_ref[(POOL_BUF + 2 - w) * nb:halo + rows, lo:lo + group]
        k = 1
        while k < w:
            n = s.shape[0]
            s = s[k * nb:, :] + s[:n - k * nb, :]
            k *= 2
        cnt = jnp.clip(pos1, 1, w).astype(F32)
        d = s / cnt - u_a[:, lo:lo + group]
        ys.append(_dot(d.astype(BF16), wp_ref[gi]) * scale_ref[:, lo:lo + group])
    tail = ext_ref[rows:rows + halo, :]
    ext_ref[0:halo, :] = tail
    return jnp.concatenate(ys, axis=1), tail[nb:, :]


def _s5_block(u_b, lbr_ref, lbi_ref, wbr_ref, wbi_ref, wcr_ref, wci_ref, dsk_ref, wg_ref, bg_ref,
              bur_ref, bui_ref, hr_ref, hi_ref, *, tt, nb):
    gn = bur_ref.shape[1]
    nkb = wbr_ref.shape[0]
    sb = gn // nkb
    ub = u_b.astype(BF16)
    for kb in range(nkb):
        uk = ub[:, kb * MXU_DIM:(kb + 1) * MXU_DIM]
        bur_ref[:, kb * sb:(kb + 1) * sb] = _dot(uk, wbr_ref[kb])
        bui_ref[:, kb * sb:(kb + 1) * sb] = _dot(uk, wbi_ref[kb])

    cw = 4 * LANES
    for c in range(gn // cw):
        cols = slice(c * cw, (c + 1) * cw)
        lr = jnp.broadcast_to(lbr_ref[:, cols], (nb, cw))
        li = jnp.broadcast_to(lbi_ref[:, cols], (nb, cw))

        def step(t, carry, cols=cols, lr=lr, li=li):
            h_re, h_im = carry
            r = pl.multiple_of(t * nb, nb)
            n_re = lr * h_re - li * h_im + bur_ref[pl.ds(r, nb), cols]
            n_im = lr * h_im + li * h_re + bui_ref[pl.ds(r, nb), cols]
            bur_ref[pl.ds(r, nb), cols] = n_re
            bui_ref[pl.ds(r, nb), cols] = n_im
            return n_re, n_im

        h_re, h_im = lax.fori_loop(0, tt, step, (hr_ref[:, cols], hi_ref[:, cols]),
                                   unroll=min(tt, 8))
        hr_ref[:, cols] = h_re
        hi_ref[:, cols] = h_im

    zs = []
    for kb in range(nkb):
        hrb = bur_ref[:, kb * sb:(kb + 1) * sb].astype(BF16)
        hib = bui_ref[:, kb * sb:(kb + 1) * sb].astype(BF16)
        ch = slice(kb * MXU_DIM, (kb + 1) * MXU_DIM)
        y = _dot(hrb, wcr_ref[kb]) - _dot(hib, wci_ref[kb]) + dsk_ref[:, ch] * u_b[:, ch]
        zs.append(jax.nn.gelu(y))
    z = jnp.concatenate(zs, axis=1)
    return z * jax.nn.sigmoid(_dot(z.astype(BF16), wg_ref[...]) + bg_ref[...])


def _mixer0_kernel(*refs, tt, nb, pos0, prompt):
    if prompt:
        x_ref, tail_ref, perm_ref, permt_ref = refs[:4]
        refs = refs[4:]
    else:
        x_ref = refs[0]
        refs = refs[1:]
    (g_ref, win_ref, pinit_ref, wp_ref, scale_ref, h0r_ref, h0i_ref, lbr_ref, lbi_ref, wbr_ref, wbi_ref,
     wcr_ref, wci_ref, dsk_ref, wg_ref, bg_ref, wout_ref,
     h_ref, pst_ref, str_ref, sti_ref, ext_ref, bur_ref, bui_ref, hr_ref, hi_ref) = refs
    i = pl.program_id(0)
    rows = tt * nb
    d = x_ref.shape[-1]
    dp = ext_ref.shape[1]

    @pl.when(i == 0)
    def _():
        ext_ref[0:(POOL_BUF + 1) * nb, :] = pinit_ref[...]
        hr_ref[...] = h0r_ref[...]
        hi_ref[...] = h0i_ref[...]

    x = x_ref[...]
    if prompt:
        x = jnp.where(i == 0, jnp.broadcast_to(tail_ref[...][None], x.shape), x)
    x2 = x.reshape(rows, d)
    xn = _rms(x2, g_ref[...]).astype(BF16)
    if prompt:
        xn = _dot(perm_ref[...], xn).astype(BF16)
    u = _dot(xn, win_ref[...])
    ya, pool_tail = _pool_block(u[:, :dp], i, ext_ref, wp_ref, scale_ref, tt=tt, nb=nb, pos0=pos0)
    yb = _s5_block(u[:, dp:], lbr_ref, lbi_ref, wbr_ref, wbi_ref, wcr_ref, wci_ref, dsk_ref, wg_ref, bg_ref,
                   bur_ref, bui_ref, hr_ref, hi_ref, tt=tt, nb=nb)
    y = jnp.concatenate([ya, yb], axis=1).astype(BF16)
    if prompt:
        y = _dot(permt_ref[...], y).astype(BF16)
    h_ref[...] = x + _dot(y, wout_ref[...]).reshape(x.shape)

    @pl.when(i == pl.num_programs(0) - 1)
    def _():
        pst_ref[...] = pool_tail
        str_ref[...] = hr_ref[...]
        sti_ref[...] = hi_ref[...]


def _mixer0(x, tail, g, w_in, pool_init, wp, scale, h0r, h0i, s5, w_out, *, tt, nb, pos0):
    prompt = tail is not None
    d = x.shape[-1]
    dp = wp.shape[0] * wp.shape[1]
    gn = h0r.shape[1]
    rows = tt * nb
    halo = (POOL_BUF + 1) * nb
    lbr, lbi, wbr, wbi, wcr, wci, dsk, wg, bg = s5
    if prompt:
        nblk = x.shape[1] // tt + 1
        out_rows = x.shape[1] + tt
        r = jnp.arange(rows)
        perm = (r[:, None] % nb * tt + r[:, None] // nb == r[None, :]).astype(BF16)
        lead = [x, tail, perm, perm.T]
        lead_specs = [pl.BlockSpec((nb, tt, d), lambda i: (0, jnp.maximum(i - 1, 0), 0)),
                      _full((tt, d)), _full((rows, rows)), _full((rows, rows))]
        h_spec = pl.BlockSpec((nb, tt, d), lambda i: (0, (i + nblk - 1) % nblk, 0))
        h_shape = (nb, out_rows, d)
    else:
        nblk = 1
        lead = [x]
        lead_specs = [_full(x.shape)]
        h_spec = _full(x.shape)
        h_shape = x.shape
    args = lead + [g, w_in, pool_init, wp, scale, h0r, h0i, lbr, lbi, wbr, wbi, wcr, wci, dsk, wg, bg, w_out]
    in_specs = lead_specs + [_full(a.shape) for a in args[len(lead):]]
    return pl.pallas_call(
        functools.partial(_mixer0_kernel, tt=tt, nb=nb, pos0=pos0, prompt=prompt),
        grid=(nblk,),
        in_specs=in_specs,
        out_specs=[h_spec, _full((POOL_BUF * nb, dp)), _full((nb, gn)), _full((nb, gn))],
        out_shape=[jax.ShapeDtypeStruct(h_shape, F32), jax.ShapeDtypeStruct((POOL_BUF * nb, dp), F32),
                   jax.ShapeDtypeStruct((nb, gn), F32), jax.ShapeDtypeStruct((nb, gn), F32)],
        scratch_shapes=[pltpu.VMEM((halo + rows, dp), F32), pltpu.VMEM((rows, gn), F32),
                        pltpu.VMEM((rows, gn), F32), pltpu.VMEM((nb, gn), F32), pltpu.VMEM((nb, gn), F32)],
        compiler_params=_cp("arbitrary"),
        name="mixer0",
    )(*args)


def _ffn_kernel(h_ref, g_ref, w1_ref, w3_ref, w2_ref, o_ref, *, fc):
    h = h_ref[...]
    xn = _rms(h, g_ref[...]).astype(BF16)
    acc = h
    for c in range(w1_ref.shape[1] // fc):
        cs = slice(c * fc, (c + 1) * fc)
        a = _dot(xn, w1_ref[:, cs])
        b = _dot(xn, w3_ref[:, cs])
        acc = acc + _dot((jax.nn.silu(a) * b).astype(BF16), w2_ref[cs, :])
    o_ref[...] = acc


def _ffn(h, g, w1, w3, w2, tm):
    r, d = h.shape
    f = w1.shape[1]
    fc = MXU_DIM if f % MXU_DIM == 0 else f
    return pl.pallas_call(
        functools.partial(_ffn_kernel, fc=fc),
        grid=(r // tm,),
        in_specs=[pl.BlockSpec((tm, d), lambda i: (i, 0)), _full((1, d)),
                  _resident((d, f)), _resident((d, f)), _resident((f, d))],
        out_specs=pl.BlockSpec((tm, d), lambda i: (i, 0)),
        out_shape=jax.ShapeDtypeStruct((r, d), F32),
        compiler_params=_cp("parallel"),
        name="ffn",
    )(h, g, w1, w3, w2)


def _gla_proj_kernel(h_ref, g_ref, win_ref, wg1_ref, wg2_ref, bgk_ref,
                     q_ref, k_ref, gk_ref, v_ref, gt_ref, *, q_scale):
    xn = _rms(h_ref[...], g_ref[...]).astype(BF16)
    kd = q_ref.shape[1]
    vd = v_ref.shape[1]
    q_ref[...] = _dot(xn, win_ref[:, 0:kd]) * q_scale
    k_ref[...] = _dot(xn, win_ref[:, kd:2 * kd])
    v_ref[...] = _dot(xn, win_ref[:, 2 * kd:2 * kd + vd]).astype(v_ref.dtype)
    gt_ref[...] = _dot(xn, win_ref[:, 2 * kd + vd:]).astype(gt_ref.dtype)
    low = _dot(xn, wg1_ref[...]).astype(BF16)
    z = _dot(low, wg2_ref[...]) + bgk_ref[...]
    log_sig = jnp.minimum(z, 0.0) - jnp.log1p(jnp.exp(-jnp.abs(z)))
    gk_ref[...] = log_sig / GLA_GATE_NORM


def _gla_proj(h, g, w_in, wg1, wg2, bgk, kd, vd, tm):
    r, d = h.shape
    rows = lambda n: pl.BlockSpec((tm, n), lambda i: (i, 0))
    return pl.pallas_call(
        functools.partial(_gla_proj_kernel, q_scale=float((kd // GLA_HEADS) ** -0.5)),
        grid=(r // tm,),
        in_specs=[rows(d), _full((1, d)), _resident(w_in.shape), _full(wg1.shape), _full(wg2.shape),
                  _full((1, kd))],
        out_specs=[rows(kd), rows(kd), rows(kd), rows(vd), rows(vd)],
        out_shape=[jax.ShapeDtypeStruct((r, kd), F32)] * 3 + [jax.ShapeDtypeStruct((r, vd), BF16)] * 2,
        compiler_params=_cp("parallel"),
        name="gla_proj",
    )(h, g, w_in, wg1, wg2, bgk)


def _split3(x):
    a = x.astype(BF16)
    r = x - a.astype(F32)
    b = r.astype(BF16)
    c = (r - b.astype(F32)).astype(BF16)
    return a, b, c


def _gla_chunk_kernel(q_ref, k_ref, gk_ref, v_ref, o_ref, s_ref, *, nchunk, rot, unroll):
    c_rows = CHUNK
    dk = q_ref.shape[2]
    nsub = c_rows // SUB
    row_i = lax.broadcasted_iota(jnp.int32, (c_rows, c_rows), 0)
    col_i = lax.broadcasted_iota(jnp.int32, (c_rows, c_rows), 1)
    tri = (row_i >= col_i).astype(BF16)
    sub_row = lax.broadcasted_iota(jnp.int32, (SUB, c_rows), 0)
    sub_col = lax.broadcasted_iota(jnp.int32, (SUB, c_rows), 1)
    half_col = lax.broadcasted_iota(jnp.int32, (SUB // 2, c_rows), 1)

    def intra(rs):
        q = q_ref[0, rs, :]
        k = k_ref[0, rs, :]
        vb = v_ref[0, rs, :]
        g1, g2, g3 = _split3(gk_ref[0, rs, :])
        bc = (_dot(tri, g1) + _dot(tri, g2) + _dot(tri, g3)) * LOG2_E
        blast = bc[c_rows - 1:c_rows, :]
        q_sub, k_hat = [], []
        att_rows = []
        half = SUB // 2
        for i in range(nsub):
            lo = i * SUB
            bs = bc[lo - 1:lo, :] if i > 0 else jnp.zeros((1, dk), F32)
            bc_i = bc[lo:lo + SUB, :]
            q_i = q[lo:lo + SUB, :]
            k_i = k[lo:lo + SUB, :]
            q_sub.append(q_i * jnp.exp2(bc_i - bs))
            if i > 0:
                k_hat.append(jnp.concatenate(
                    [k[:lo, :] * jnp.exp2(bs - bc[:lo, :]), jnp.zeros((c_rows - lo, dk), F32)], axis=0))
            top = jnp.zeros((half, c_rows), F32)
            bot = jnp.zeros((half, c_rows), F32)
            for s in range(SUB):
                r0 = 0 if s < half else half
                e = jnp.exp2(jnp.minimum(bc_i[r0:, :] - bc_i[s:s + 1, :], 0.0))
                col = jnp.sum(q_i[r0:, :] * k_i[s:s + 1, :] * e, axis=-1, keepdims=True)
                if s < half:
                    top = jnp.where(half_col == lo + s, col[:half, :], top)
                bot = jnp.where(half_col == lo + s, col[half - r0:, :], bot)
            diag = jnp.concatenate([top, bot], axis=0)
            att_rows.append(jnp.where(sub_row + lo >= sub_col, diag, 0.0))
        att = jnp.concatenate(att_rows, axis=0)
        zero = jnp.zeros((SUB, dk), F32)
        lhs = jnp.concatenate(
            [jnp.concatenate([q_sub[i] if j == i else zero for j in range(1, nsub)], axis=1)
             for i in range(nsub)], axis=0)
        rhs = jnp.concatenate(k_hat, axis=1)
        att = att + _dot_nt(lhs.astype(BF16), rhs.astype(BF16))
        o_intra = _dot(att.astype(BF16), vb)
        q_in = (q * jnp.exp2(bc)).astype(BF16)
        k_dec = (k * jnp.exp2(blast - bc)).astype(BF16)
        return o_intra, q_in, k_dec, jnp.exp2(blast), vb

    def body(it, st):
        spans = []
        for j in range(unroll):
            mem = lax.rem(it * unroll + j + rot, nchunk)
            spans.append(pl.ds(pl.multiple_of(mem * c_rows, c_rows), c_rows))
        parts = [intra(rs) for rs in spans]
        for rs, (o_intra, q_in, k_dec, decay, vb) in zip(spans, parts):
            o_ref[0, rs, :] = (o_intra + _dot_nt(q_in, st.astype(BF16))).astype(o_ref.dtype)
            st = st * decay + _dot_tn(vb, k_dec)
        return st

    st = lax.fori_loop(0, nchunk // unroll, body, jnp.zeros((v_ref.shape[2], dk), F32))
    s_ref[0, 0] = st.T


def _gla_chunked(q, k, gk, v, *, rot):
    b, l, kd = q.shape
    vd = v.shape[2]
    dk, dv = kd // GLA_HEADS, vd // GLA_HEADS
    seq = lambda n: pl.BlockSpec((1, l, n), lambda bi, hi: (bi, 0, hi))
    nchunk = l // CHUNK
    unroll = _first_divisor(nchunk, (11, 3, 2, 1))
    return pl.pallas_call(
        functools.partial(_gla_chunk_kernel, nchunk=nchunk, rot=rot, unroll=unroll),
        grid=(b, GLA_HEADS),
        in_specs=[seq(dk), seq(dk), seq(dk), seq(dv)],
        out_specs=[seq(dv), pl.BlockSpec((1, 1, dk, dv), lambda bi, hi: (bi, hi, 0, 0))],
        out_shape=[jax.ShapeDtypeStruct((b, l, vd), BF16),
                   jax.ShapeDtypeStruct((b, GLA_HEADS, dk, dv), F32)],
        compiler_params=_cp("parallel", "parallel"),
        name="gla_chunked",
    )(q, k, gk, v)


def _gla_step_kernel(qt_ref, kt_ref, gt_ref, v_ref, s0_ref, o_ref, s_ref):
    bt = v_ref.shape[1]
    dec = jnp.exp(gt_ref[0])
    kt = kt_ref[0]
    qt = qt_ref[0]
    vf = v_ref[0].astype(F32)
    for j in range(bt):
        v_row = vf[j:j + 1, :]
        s_new = dec[:, j:j + 1] * s0_ref[j, 0] + kt[:, j:j + 1] * v_row
        s_ref[j, 0] = s_new
        o_ref[0, j:j + 1, :] = jnp.sum(qt[:, j:j + 1] * s_new, axis=0, keepdims=True)


def _gla_step(q, k, gk, v, s0, bt=32):
    b, kd = q.shape
    vd = v.shape[1]
    dk, dv = kd // GLA_HEADS, vd // GLA_HEADS
    nt = b // bt
    cols = lambda t: t.reshape(nt, bt, kd).transpose(0, 2, 1)
    col_spec = pl.BlockSpec((1, dk, bt), lambda ti, hi: (ti, hi, 0))
    v_spec = pl.BlockSpec((1, bt, dv), lambda ti, hi: (ti, 0, hi))
    s_spec = pl.BlockSpec((bt, 1, dk, dv), lambda ti, hi: (ti, hi, 0, 0))
    o, s = pl.pallas_call(
        _gla_step_kernel,
        grid=(nt, GLA_HEADS),
        in_specs=[col_spec, col_spec, col_spec, v_spec, s_spec],
        out_specs=[v_spec, s_spec],
        out_shape=[jax.ShapeDtypeStruct((nt, bt, vd), F32), jax.ShapeDtypeStruct(s0.shape, F32)],
        compiler_params=_cp("parallel", "parallel"),
        name="gla_step",
    )(cols(q), cols(k), cols(gk), v.reshape(nt, bt, vd), s0)
    return o.reshape(b, vd), s


def _gla_out_kernel(o_ref, gt_ref, h_ref, gn_ref, w_ref, out_ref):
    dv = gn_ref.shape[1]
    parts = []
    for hd in range(o_ref.shape[2] // dv):
        sl = slice(hd * dv, (hd + 1) * dv)
        o_h = o_ref[0, :, sl].astype(F32)
        g_h = gt_ref[0, :, sl].astype(F32)
        parts.append((_rms(o_h, gn_ref[...]) * jax.nn.silu(g_h)).astype(BF16))
    out_ref[0] = h_ref[0] + _dot(jnp.concatenate(parts, axis=1), w_ref[...])


def _gla_out(o, gt, h, gnorm, w, n_rows, tm):
    b, _, d = h.shape
    vd = o.shape[2]
    rows = lambda n: pl.BlockSpec((1, tm, n), lambda bi, i: (bi, i, 0))
    return pl.pallas_call(
        _gla_out_kernel,
        grid=(b, n_rows // tm),
        in_specs=[rows(vd), rows(vd), rows(d), _full(gnorm.shape), _resident(w.shape)],
        out_specs=rows(d),
        out_shape=jax.ShapeDtypeStruct((b, n_rows, d), F32),
        compiler_params=_cp("parallel", "parallel"),
        name="gla_out",
    )(o, gt, h, gnorm, w)


def _moe_kernel(h_ref, g_ref, wr_ref, tri_ref, w1_ref, w3_ref, w2_ref, gf_ref, o_ref,
                pt_ref, sl_ref, gg_ref, meta_ref, *, ne):
    e = pl.program_id(2)
    t = h_ref.shape[1]
    nchunk_max = pt_ref.shape[0]

    @pl.when(e == 0)
    def _():
        h = h_ref[0]
        xn = _rms(h, g_ref[...])
        xb = xn.astype(BF16)
        nsub = wr_ref.shape[0]
        logits = _dot_nt(wr_ref[...], xb)
        sub = lax.broadcasted_iota(jnp.int32, logits.shape, 0)
        valid = sub < ne
        logits = jnp.where(valid, logits, -jnp.inf)
        p = jnp.exp(logits - jnp.max(logits, axis=0, keepdims=True))
        p = p / jnp.sum(p, axis=0, keepdims=True)
        p = jnp.where(valid, p, -1.0)
        m1 = jnp.max(p, axis=0, keepdims=True)
        i1 = jnp.min(jnp.where(p == m1, sub, nsub), axis=0, keepdims=True)
        rest = jnp.where(sub == i1, -1.0, p)
        m2 = jnp.max(rest, axis=0, keepdims=True)
        i2 = jnp.min(jnp.where(rest == m2, sub, nsub), axis=0, keepdims=True)
        tot = m1 + m2
        sel = (sub == i1) | (sub == i2)
        gates = jnp.where(sub == i1, m1 / tot, jnp.where(sub == i2, m2 / tot, 0.0))

        incl = _dot(jnp.where(sel, 1.0, 0.0).astype(BF16), tri_ref[...])
        blocks = jnp.floor((incl[:, t - 1:t] + (MOE_BLK - 1)) * (1.0 / MOE_BLK))
        before = (lax.broadcasted_iota(jnp.int32, (nsub, nsub), 1)
                  < lax.broadcasted_iota(jnp.int32, (nsub, nsub), 0)).astype(BF16)
        off = _dot(before, jnp.broadcast_to(blocks, (nsub, LANES)).astype(BF16))[:, 0:1]
        slot = jnp.where(sel, off * MOE_BLK + incl - 1.0, -1.0)
        s1 = jnp.sum(jnp.where(sub == i1, slot, 0.0), axis=0, keepdims=True)
        s2 = jnp.sum(jnp.where(sub == i2, slot, 0.0), axis=0, keepdims=True)
        g_src = jnp.concatenate([*_split3(gates), jnp.zeros((LANES - 3 * nsub, t), BF16)], axis=0)
        nchunks = jnp.floor((jnp.sum(blocks) + 1.0) * 0.5).astype(jnp.int32)
        for c in range(nchunk_max):
            @pl.when(c < nchunks)
            def _(c=c):
                ids = (lax.broadcasted_iota(jnp.int32, (MOE_CHUNK, t), 0) + c * MOE_CHUNK).astype(F32)
                pc = (jnp.where(s1 == ids, 1.0, 0.0) + jnp.where(s2 == ids, 1.0, 0.0)).astype(BF16)
                rs = slice(c * MOE_CHUNK, (c + 1) * MOE_CHUNK)
                pt_ref[c] = pc
                sl_ref[rs, :] = _dot(pc, xb).astype(BF16)
                gg_ref[rs, :] = _dot_nt(pc, g_src)
        o_ref[0] = h
        for ee in range(ne):
            meta_ref[0, ee] = blocks[ee, 0].astype(jnp.int32)
            meta_ref[1, ee] = off[ee, 0].astype(jnp.int32)
        meta_ref[2, 0] = nchunks

    def expert_rows(first_blk, nblk):
        rows = nblk * MOE_BLK
        rs = pl.ds(pl.multiple_of(first_blk * MOE_BLK, MOE_BLK), rows)
        xg = sl_ref[rs, :]
        a = _dot(xg, w1_ref[0])
        b = _dot(xg, w3_ref[0])
        y = _dot((jax.nn.silu(a) * b).astype(BF16), w2_ref[0])
        lane_b = lax.broadcasted_iota(jnp.int32, (rows, LANES), 1)
        mine = jnp.bitwise_and(lane_b, wr_ref.shape[0] - 1) == e
        gate = jnp.sum(jnp.where(mine, gg_ref[rs, :], 0.0), axis=-1, keepdims=True)
        sl_ref[rs, :] = (gate * y).astype(BF16)

    nblk = meta_ref[0, e]
    first = meta_ref[1, e]
    odd = lax.rem(nblk, 2)
    lead = jnp.where(nblk >= 3, 3 * odd, odd)

    @pl.when(lead == 1)
    def _():
        expert_rows(first, 1)

    @pl.when(lead == 3)
    def _():
        expert_rows(first, 3)

    def pair(jp, carry):
        expert_rows(first + lead + 2 * jp, 2)
        return carry

    lax.fori_loop(0, lax.div(nblk - lead, 2), pair, 0)

    @pl.when(e == ne - 1)
    def _():
        def combine(c, carry):
            rs = pl.ds(pl.multiple_of(c * MOE_CHUNK, MOE_CHUNK), MOE_CHUNK)
            o_ref[0] += _dot_tn(pt_ref[c], sl_ref[rs, :])
            return carry

        lax.fori_loop(0, meta_ref[2, 0], combine, 0)
        o_ref[0] = _rms(o_ref[0], gf_ref[...])


def _moe_final(h, g, w_router, w1, w3, w2, g_final, tm):
    b, l, d = h.shape
    ne, _, f = w1.shape
    slots = -(-(2 * tm + ne * MOE_BLK) // MOE_CHUNK) * MOE_CHUNK
    r = jnp.arange(tm)
    tri = (r[:, None] <= r[None, :]).astype(BF16)
    rows = pl.BlockSpec((1, tm, d), lambda bi, i, e: (bi, i, 0))
    return pl.pallas_call(
        functools.partial(_moe_kernel, ne=ne),
        grid=(b, l // tm, ne),
        in_specs=[rows, _full((1, d)), _full(w_router.shape), _resident((tm, tm)),
                  pl.BlockSpec((1, d, f), lambda bi, i, e: (e, 0, 0)),
                  pl.BlockSpec((1, d, f), lambda bi, i, e: (e, 0, 0)),
                  pl.BlockSpec((1, f, d), lambda bi, i, e: (e, 0, 0)), _full((1, d))],
        out_specs=rows,
        out_shape=jax.ShapeDtypeStruct((b, l, d), F32),
        scratch_shapes=[pltpu.VMEM((slots // MOE_CHUNK, MOE_CHUNK, tm), BF16), pltpu.VMEM((slots, d), BF16),
                        pltpu.VMEM((slots, LANES), F32),
                        pltpu.SMEM((3, ne), jnp.int32)],
        compiler_params=_cp("parallel", "parallel", "arbitrary"),
        name="moe_final",
    )(h, g, w_router, tri, w1, w3, w2, g_final)


def kernel(x_prompt, x_sample, state_pool, state_s5_re, state_s5_im, state_gla, meta_tokens, norm_mix_e, w_in_e, w_pool, pool_scale, s5_a_re, s5_a_im, s5_log_dt, s5_b_re, s5_b_im, s5_c_re, s5_c_im, s5_d, w_glu, b_glu, w_out_e, norm_ffn_e, ffn_w1, ffn_w3, ffn_w2, norm_mix_o, w_in_o, w_gk1, w_gk2, b_gk, gla_norm, w_out_o, norm_ffn_o, w_router, moe_w1, moe_w3, moe_w2, norm_final):
    bp, seq, d = x_prompt.shape
    bs = x_sample.shape[0]
    lp = seq + CHUNK
    dp = w_pool.shape[1] * w_pool.shape[2]
    g_ssm, n_ssm, p_ssm = s5_b_re.shape[1:]
    gn = g_ssm * n_ssm
    kd = w_gk2.shape[2]
    vd = w_out_o.shape[1]
    row = lambda t: t.reshape(1, -1)
    bf = lambda t: t.astype(BF16)

    j = 0
    w_in_e_b, w_out_e_b = bf(w_in_e[j]), bf(w_out_e[j])
    w_pool_b, w_glu_b = bf(w_pool[j]), bf(w_glu[j])
    w1_b, w3_b, w2_b = bf(ffn_w1[j]), bf(ffn_w3[j]), bf(ffn_w2[j])
    lbr, lbi, bbr, bbi = _s5_prep(s5_a_re[j], s5_a_im[j], s5_log_dt[j], s5_b_re[j], s5_b_im[j])
    s5_args = (lbr, lbi, bf(_block_diag_in(bbr, g_ssm, n_ssm, p_ssm)), bf(_block_diag_in(bbi, g_ssm, n_ssm, p_ssm)),
               bf(_block_diag_out(s5_c_re[j])), bf(_block_diag_out(s5_c_im[j])),
               row(s5_d[j]), w_glu_b, row(b_glu[j]))
    w_in_o_b, w_out_o_b = bf(w_in_o[j]), bf(w_out_o[j])
    rank = w_gk1.shape[2]
    wg1 = bf(jnp.pad(w_gk1[j], ((0, 0), (0, LANES - rank))))
    wg2 = bf(jnp.pad(w_gk2[j], ((0, LANES - rank), (0, 0))))
    w_router_b = bf(jnp.pad(w_router[j].T, ((0, MOE_ROUTE_ROWS - w_router.shape[2]), (0, 0))))
    mw1, mw3, mw2 = bf(moe_w1[j]), bf(moe_w3[j]), bf(moe_w2[j])
    mix_w = (row(norm_mix_e[j]), w_in_e_b)
    pool_w = (w_pool_b, row(pool_scale[j]))

    tail = jnp.concatenate([jnp.zeros((CHUNK - N_META, d), F32), meta_tokens.astype(F32)], axis=0)
    zero_state = jnp.zeros((bp, gn), F32)
    h1, pool_p, re_p, im_p = _mixer0(
        x_prompt, tail, *mix_w, jnp.zeros(((POOL_BUF + 1) * bp, dp), F32), *pool_w,
        zero_state, zero_state, s5_args, w_out_e_b, tt=CHUNK, nb=bp, pos0=-(CHUNK - N_META))
    tm = _first_divisor(bp * lp, (768, 512, 256, 128, 8))
    h2 = _ffn(h1.reshape(bp * lp, d), row(norm_ffn_e[j]), w1_b, w3_b, w2_b, tm)
    q, k, gk, v, gt = _gla_proj(h2, row(norm_mix_o[j]), w_in_o_b, wg1, wg2, row(b_gk[j]), kd, vd, tm)
    seq3 = lambda t: t.reshape(bp, lp, t.shape[1])
    o, gla_p = _gla_chunked(seq3(q), seq3(k), seq3(gk), seq3(v), rot=lp // CHUNK - 1)
    ts = _first_divisor(seq, (1024, 512, 256, 128, 8))
    h3 = _gla_out(o, seq3(gt), seq3(h2), row(gla_norm[j]), w_out_o_b, seq, ts)
    y_prompt = _moe_final(h3, row(norm_ffn_o[j]), w_router_b, mw1, mw3, mw2, row(norm_final), ts)

    pool_init = jnp.pad(state_pool[j].transpose(1, 0, 2), ((1, 0), (0, 0), (0, 0))).reshape((POOL_BUF + 1) * bs, dp)
    h1s, pool_s, re_s, im_s = _mixer0(
        x_sample.reshape(1, bs, d), None, *mix_w, pool_init, *pool_w,
        state_s5_re[j].reshape(bs, gn), state_s5_im[j].reshape(bs, gn), s5_args, w_out_e_b,
        tt=1, nb=bs, pos0=PAST_LEN)
    h2s = _ffn(h1s.reshape(bs, d), row(norm_ffn_e[j]), w1_b, w3_b, w2_b, bs)
    qs, ks, gks, vs, gts = _gla_proj(h2s, row(norm_mix_o[j]), w_in_o_b, wg1, wg2, row(b_gk[j]), kd, vd, bs)
    os_, gla_s = _gla_step(qs, ks, gks, vs, state_gla[j])
    h3s = _gla_out(os_[None], gts[None], h2s[None], row(gla_norm[j]), w_out_o_b, bs, bs)
    y_sample = _moe_final(h3s, row(norm_ffn_o[j]), w_router_b, mw1, mw3, mw2, row(norm_final), bs)

    tb = lambda t, nb: t.reshape(POOL_BUF, nb, dp).transpose(1, 0, 2)[None]
    ssm = lambda t, nb: t.reshape(1, nb, g_ssm, n_ssm)
    return (y_prompt, y_sample.reshape(bs, 1, d), tb(pool_p, bp), tb(pool_s, bs),
            ssm(re_p, bp), ssm(re_s, bs), ssm(im_p, bp), ssm(im_s, bs), gla_p[None], gla_s[None])
```

```python
import functools

import jax
import jax.numpy as jnp
from jax import lax
from jax.experimental import pallas as pl
from jax.experimental.pallas import tpu as pltpu

F32 = jnp.float32
BF16 = jnp.bfloat16

EPS = 1e-6
LOG2_E = 1.4426950408889634
N_META = 16
PAST_LEN = 16384
POOL_WINDOWS = (2, 4, 8, 16)
POOL_BUF = max(POOL_WINDOWS) - 1
GLA_HEADS = 4
GLA_GATE_NORM = 16.0
CHUNK = 64
SUB = 16
GLA_FAST_RANGE = 64.0
MXU_DIM = 256
MOE_BLK = 128
MOE_CHUNK = MXU_DIM
MOE_ROUTE_ROWS = 16
LANES = 128
VMEM_LIMIT = 56 * 1024 * 1024


def _cp(*sem, vmem=VMEM_LIMIT):
    return pltpu.CompilerParams(dimension_semantics=sem, vmem_limit_bytes=vmem)


def _rms(x, g):
    return x * lax.rsqrt(jnp.mean(x * x, axis=-1, keepdims=True) + EPS) * g


def _dot(a, b):
    return jnp.dot(a, b, preferred_element_type=F32)


def _dot_nt(a, b):
    return lax.dot_general(a, b, (((1,), (1,)), ((), ())), preferred_element_type=F32)


def _dot_tn(a, b):
    return lax.dot_general(a, b, (((0,), (0,)), ((), ())), preferred_element_type=F32)


def _full(shape):
    return pl.BlockSpec(shape, lambda *_: (0,) * len(shape))


def _resident(shape):
    return pl.BlockSpec(shape, lambda *_: (0,) * len(shape), pipeline_mode=pl.Buffered(1))


def _first_divisor(n, candidates):
    return next(c for c in candidates if n % c == 0)


def _s5_prep_kernel(ar_ref, ai_ref, ldt_ref, br_ref, bi_ref, lbr_ref, lbi_ref, bbr_ref, bbi_ref):
    dt = jnp.exp(ldt_ref[...])
    ar = ar_ref[...]
    ai = ai_ref[...]
    mag = jnp.exp(ar * dt)
    lb_re = mag * jnp.cos(ai * dt)
    lb_im = mag * jnp.sin(ai * dt)
    den = ar * ar + ai * ai
    nr = lb_re - 1.0
    f_re = (nr * ar + lb_im * ai) / den
    f_im = (lb_im * ar - nr * ai) / den
    lbr_ref[...] = lb_re
    lbi_ref[...] = lb_im
    br = br_ref[...]
    bi = bi_ref[...]
    bbr_ref[...] = f_re * br - f_im * bi
    bbi_ref[...] = f_re * bi + f_im * br


def _s5_prep(a_re, a_im, log_dt, b_re, b_im):
    g, n, p = b_re.shape
    gn = g * n
    row = lambda t: t.reshape(1, gn)
    to_pgn = lambda t: t.transpose(2, 0, 1).reshape(p, gn)
    ldt = jnp.broadcast_to(log_dt[:, None], (g, n))
    return pl.pallas_call(
        _s5_prep_kernel,
        out_shape=[jax.ShapeDtypeStruct((1, gn), F32)] * 2 + [jax.ShapeDtypeStruct((p, gn), F32)] * 2,
        name="s5_prep",
    )(row(a_re), row(a_im), row(ldt), to_pgn(b_re), to_pgn(b_im))


def _block_diag_in(bb_pgn, g, n, p):
    bb = bb_pgn.reshape(p, g, n).transpose(1, 0, 2)
    eye = jnp.eye(g, dtype=bb.dtype)
    full = (bb[:, :, None, :] * eye[:, None, :, None]).reshape(g * p, g * n)
    nblk = g * p // MXU_DIM
    cols = g * n // nblk
    return jnp.stack([full[k * MXU_DIM:(k + 1) * MXU_DIM, k * cols:(k + 1) * cols] for k in range(nblk)])


def _block_diag_out(c_gpn):
    g, p, n = c_gpn.shape
    eye = jnp.eye(g, dtype=c_gpn.dtype)
    full = (c_gpn.transpose(0, 2, 1)[:, :, None, :] * eye[:, None, :, None]).reshape(g * n, g * p)
    nblk = g * p // MXU_DIM
    rows = g * n // nblk
    return jnp.stack([full[k * rows:(k + 1) * rows, k * MXU_DIM:(k + 1) * MXU_DIM] for k in range(nblk)])


def _pool_block(u_a, i, ext_ref, wp_ref, scale_ref, *, tt, nb, pos0):
    rows = tt * nb
    halo = (POOL_BUF + 1) * nb
    shift = nb.bit_length() - 1
    ext_ref[halo:halo + rows, :] = u_a
    t_in = lax.shift_right_logical(lax.broadcasted_iota(jnp.int32, (rows, LANES), 0), shift)
    pos1 = t_in + (pos0 + 1 + i * tt)
    group = u_a.shape[1] // len(POOL_WINDOWS)
    ys = []
    for gi, w in enumerate(POOL_WINDOWS):
        lo = gi * group
        s = ext_ref[(POOL_BUF + 2 - w) * nb:halo + rows, lo:lo + group]
        k = 1
        while k < w:
            n = s.shape[0]
            s = s[k * nb:, :] + s[:n - k * nb, :]
            k *= 2
        cnt = jnp.clip(pos1, 1, w).astype(F32)
        d = s / cnt - u_a[:, lo:lo + group]
        ys.append(_dot(d.astype(BF16), wp_ref[gi]) * scale_ref[:, lo:lo + group])
    tail = ext_ref[rows:rows + halo, :]
    ext_ref[0:halo, :] = tail
    return jnp.concatenate(ys, axis=1), tail[nb:, :]


def _s5_block(u_b, lbr_ref, lbi_ref, wbr_ref, wbi_ref, wcr_ref, wci_ref, dsk_ref, wg_ref, bg_ref,
              bur_ref, bui_ref, hr_ref, hi_ref, *, tt, nb):
    gn = bur_ref.shape[1]
    nkb = wbr_ref.shape[0]
    sb = gn // nkb
    ub = u_b.astype(BF16)
    for kb in range(nkb):
        uk = ub[:, kb * MXU_DIM:(kb + 1) * MXU_DIM]
        bur_ref[:, kb * sb:(kb + 1) * sb] = _dot(uk, wbr_ref[kb])
        bui_ref[:, kb * sb:(kb + 1) * sb] = _dot(uk, wbi_ref[kb])

    cw = 4 * LANES
    for c in range(gn // cw):
        cols = slice(c * cw, (c + 1) * cw)
        lr = jnp.broadcast_to(lbr_ref[:, cols], (nb, cw))
        li = jnp.broadcast_to(lbi_ref[:, cols], (nb, cw))

        def step(t, carry, cols=cols, lr=lr, li=li):
            h_re, h_im = carry
            r = pl.multiple_of(t * nb, nb)
            n_re = lr * h_re - li * h_im + bur_ref[pl.ds(r, nb), cols]
            n_im = lr * h_im + li * h_re + bui_ref[pl.ds(r, nb), cols]
            bur_ref[pl.ds(r, nb), cols] = n_re
            bui_ref[pl.ds(r, nb), cols] = n_im
            return n_re, n_im

        h_re, h_im = lax.fori_loop(0, tt, step, (hr_ref[:, cols], hi_ref[:, cols]),
                                   unroll=min(tt, 8))
        hr_ref[:, cols] = h_re
        hi_ref[:, cols] = h_im

    zs = []
    for kb in range(nkb):
        hrb = bur_ref[:, kb * sb:(kb + 1) * sb].astype(BF16)
        hib = bui_ref[:, kb * sb:(kb + 1) * sb].astype(BF16)
        ch = slice(kb * MXU_DIM, (kb + 1) * MXU_DIM)
        y = _dot(hrb, wcr_ref[kb]) - _dot(hib, wci_ref[kb]) + dsk_ref[:, ch] * u_b[:, ch]
        zs.append(jax.nn.gelu(y))
    z = jnp.concatenate(zs, axis=1)
    return z * jax.nn.sigmoid(_dot(z.astype(BF16), wg_ref[...]) + bg_ref[...])


def _mixer0_kernel(*refs, tt, nb, pos0, prompt):
    if prompt:
        x_ref, tail_ref, perm_ref, permt_ref = refs[:4]
        refs = refs[4:]
    else:
        x_ref = refs[0]
        refs = refs[1:]
    (g_ref, win_ref, pinit_ref, wp_ref, scale_ref, h0r_ref, h0i_ref, lbr_ref, lbi_ref, wbr_ref, wbi_ref,
     wcr_ref, wci_ref, dsk_ref, wg_ref, bg_ref, wout_ref,
     h_ref, pst_ref, str_ref, sti_ref, ext_ref, bur_ref, bui_ref, hr_ref, hi_ref) = refs
    i = pl.program_id(0)
    rows = tt * nb
    d = x_ref.shape[-1]
    dp = ext_ref.shape[1]

    @pl.when(i == 0)
    def _():
        ext_ref[0:(POOL_BUF + 1) * nb, :] = pinit_ref[...]
        hr_ref[...] = h0r_ref[...]
        hi_ref[...] = h0i_ref[...]

    x = x_ref[...]
    if prompt:
        x = jnp.where(i == 0, jnp.broadcast_to(tail_ref[...][None], x.shape), x)
    x2 = x.reshape(rows, d)
    xn = _rms(x2, g_ref[...]).astype(BF16)
    if prompt:
        xn = _dot(perm_ref[...], xn).astype(BF16)
    u = _dot(xn, win_ref[...])
    ya, pool_tail = _pool_block(u[:, :dp], i, ext_ref, wp_ref, scale_ref, tt=tt, nb=nb, pos0=pos0)
    yb = _s5_block(u[:, dp:], lbr_ref, lbi_ref, wbr_ref, wbi_ref, wcr_ref, wci_ref, dsk_ref, wg_ref, bg_ref,
                   bur_ref, bui_ref, hr_ref, hi_ref, tt=tt, nb=nb)
    y = jnp.concatenate([ya, yb], axis=1).astype(BF16)
    if prompt:
        y = _dot(permt_ref[...], y).astype(BF16)
    h_ref[...] = x + _dot(y, wout_ref[...]).reshape(x.shape)

    @pl.when(i == pl.num_programs(0) - 1)
    def _():
        pst_ref[...] = pool_tail
        str_ref[...] = hr_ref[...]
        sti_ref[...] = hi_ref[...]


def _mixer0(x, tail, g, w_in, pool_init, wp, scale, h0r, h0i, s5, w_out, *, tt, nb, pos0):
    prompt = tail is not None
    d = x.shape[-1]
    dp = wp.shape[0] * wp.shape[1]
    gn = h0r.shape[1]
    rows = tt * nb
    halo = (POOL_BUF + 1) * nb
    lbr, lbi, wbr, wbi, wcr, wci, dsk, wg, bg = s5
    if prompt:
        nblk = x.shape[1] // tt + 1
        out_rows = x.shape[1] + tt
        r = jnp.arange(rows)
        perm = (r[:, None] % nb * tt + r[:, None] // nb == r[None, :]).astype(BF16)
        lead = [x, tail, perm, perm.T]
        lead_specs = [pl.BlockSpec((nb, tt, d), lambda i: (0, jnp.maximum(i - 1, 0), 0)),
                      _full((tt, d)), _full((rows, rows)), _full((rows, rows))]
        h_spec = pl.BlockSpec((nb, tt, d), lambda i: (0, (i + nblk - 1) % nblk, 0))
        h_shape = (nb, out_rows, d)
    else:
        nblk = 1
        lead = [x]
        lead_specs = [_full(x.shape)]
        h_spec = _full(x.shape)
        h_shape = x.shape
    args = lead + [g, w_in, pool_init, wp, scale, h0r, h0i, lbr, lbi, wbr, wbi, wcr, wci, dsk, wg, bg, w_out]
    in_specs = lead_specs + [_full(a.shape) for a in args[len(lead):]]
    return pl.pallas_call(
        functools.partial(_mixer0_kernel, tt=tt, nb=nb, pos0=pos0, prompt=prompt),
        grid=(nblk,),
        in_specs=in_specs,
        out_specs=[h_spec, _full((POOL_BUF * nb, dp)), _full((nb, gn)), _full((nb, gn))],
        out_shape=[jax.ShapeDtypeStruct(h_shape, F32), jax.ShapeDtypeStruct((POOL_BUF * nb, dp), F32),
                   jax.ShapeDtypeStruct((nb, gn), F32), jax.ShapeDtypeStruct((nb, gn), F32)],
        scratch_shapes=[pltpu.VMEM((halo + rows, dp), F32), pltpu.VMEM((rows, gn), F32),
                        pltpu.VMEM((rows, gn), F32), pltpu.VMEM((nb, gn), F32), pltpu.VMEM((nb, gn), F32)],
        compiler_params=_cp("arbitrary"),
        name="mixer0",
    )(*args)


def _ffn_kernel(h_ref, g_ref, w1_ref, w3_ref, w2_ref, o_ref, *, fc):
    h = h_ref[...]
    xn = _rms(h, g_ref[...]).astype(BF16)
    acc = h
    for c in range(w1_ref.shape[1] // fc):
        cs = slice(c * fc, (c + 1) * fc)
        a = _dot(xn, w1_ref[:, cs])
        b = _dot(xn, w3_ref[:, cs])
        acc = acc + _dot((jax.nn.silu(a) * b).astype(BF16), w2_ref[cs, :])
    o_ref[...] = acc


def _ffn(h, g, w1, w3, w2, tm):
    r, d = h.shape
    f = w1.shape[1]
    fc = MXU_DIM if f % MXU_DIM == 0 else f
    return pl.pallas_call(
        functools.partial(_ffn_kernel, fc=fc),
        grid=(r // tm,),
        in_specs=[pl.BlockSpec((tm, d), lambda i: (i, 0)), _full((1, d)),
                  _resident((d, f)), _resident((d, f)), _resident((f, d))],
        out_specs=pl.BlockSpec((tm, d), lambda i: (i, 0)),
        out_shape=jax.ShapeDtypeStruct((r, d), F32),
        compiler_params=_cp("parallel"),
        name="ffn",
    )(h, g, w1, w3, w2)


def _gla_proj_kernel(h_ref, g_ref, win_ref, wg1_ref, wg2_ref, bgk_ref,
                     q_ref, k_ref, gk_ref, v_ref, gt_ref, *, q_scale):
    xn = _rms(h_ref[...], g_ref[...]).astype(BF16)
    kd = q_ref.shape[1]
    vd = v_ref.shape[1]
    q_ref[...] = _dot(xn, win_ref[:, 0:kd]) * q_scale
    k_ref[...] = _dot(xn, win_ref[:, kd:2 * kd])
    v_ref[...] = _dot(xn, win_ref[:, 2 * kd:2 * kd + vd]).astype(v_ref.dtype)
    gt_ref[...] = _dot(xn, win_ref[:, 2 * kd + vd:]).astype(gt_ref.dtype)
    low = _dot(xn, wg1_ref[...]).astype(BF16)
    z = _dot(low, wg2_ref[...]) + bgk_ref[...]
    log_sig = jnp.minimum(z, 0.0) - jnp.log1p(jnp.exp(-jnp.abs(z)))
    gk_ref[...] = log_sig / GLA_GATE_NORM


def _gla_proj(h, g, w_in, wg1, wg2, bgk, kd, vd, tm):
    r, d = h.shape
    rows = lambda n: pl.BlockSpec((tm, n), lambda i: (i, 0))
    return pl.pallas_call(
        functools.partial(_gla_proj_kernel, q_scale=float((kd // GLA_HEADS) ** -0.5)),
        grid=(r // tm,),
        in_specs=[rows(d), _full((1, d)), _resident(w_in.shape), _full(wg1.shape), _full(wg2.shape),
                  _full((1, kd))],
        out_specs=[rows(kd), rows(kd), rows(kd), rows(vd), rows(vd)],
        out_shape=[jax.ShapeDtypeStruct((r, kd), F32)] * 3 + [jax.ShapeDtypeStruct((r, vd), BF16)] * 2,
        compiler_params=_cp("parallel"),
        name="gla_proj",
    )(h, g, w_in, wg1, wg2, bgk)


def _split3(x):
    a = x.astype(BF16)
    r = x - a.astype(F32)
    b = r.astype(BF16)
    c = (r - b.astype(F32)).astype(BF16)
    return a, b, c


def _gla_chunk_kernel(q_ref, k_ref, gk_ref, v_ref, o_ref, s_ref, st_ref, *, nchunk, rot, unroll, group):
    c_rows = CHUNK
    dk = q_ref.shape[2]
    nsub = c_rows // SUB
    row_i = lax.broadcasted_iota(jnp.int32, (c_rows, c_rows), 0)
    col_i = lax.broadcasted_iota(jnp.int32, (c_rows, c_rows), 1)
    tri = (row_i >= col_i).astype(BF16)
    sub_row = lax.broadcasted_iota(jnp.int32, (SUB, c_rows), 0)
    sub_col = lax.broadcasted_iota(jnp.int32, (SUB, c_rows), 1)
    half_col = lax.broadcasted_iota(jnp.int32, (SUB // 2, c_rows), 1)

    def cum_decay(rs):
        g1, g2, g3 = _split3(gk_ref[0, rs, :])
        return (_dot(tri, g1) + _dot(tri, g2) + _dot(tri, g3)) * LOG2_E

    def intra_fast(spans, bcs):
        rows = group * c_rows
        load = lambda ref: jnp.concatenate([ref[0, rs, :] for rs in spans], axis=0)
        q, k, vb = load(q_ref), load(k_ref), load(v_ref)
        run, pieces = jnp.zeros((1, dk), F32), []
        for bc in bcs:
            pieces.append(bc + run)
            run = run + bc[c_rows - 1:c_rows, :]
        bc = jnp.concatenate(pieces, axis=0)
        q_in = (q * jnp.exp2(bc)).astype(BF16)
        k_up = (k * jnp.exp2(-bc)).astype(BF16)
        causal = (lax.broadcasted_iota(jnp.int32, (rows, rows), 0)
                  >= lax.broadcasted_iota(jnp.int32, (rows, rows), 1))
        att = jnp.where(causal, _dot_nt(q_in, k_up), 0.0)
        o_intra = _dot(att.astype(BF16), vb)
        k_dec = (k * jnp.exp2(run - bc)).astype(BF16)
        return o_intra, q_in, k_dec, jnp.exp2(run), vb

    def intra_safe(rs, bc):
        q = q_ref[0, rs, :]
        k = k_ref[0, rs, :]
        vb = v_ref[0, rs, :]
        blast = bc[c_rows - 1:c_rows, :]
        q_sub, k_hat = [], []
        att_rows = []
        half = SUB // 2
        for i in range(nsub):
            lo = i * SUB
            bs = bc[lo - 1:lo, :] if i > 0 else jnp.zeros((1, dk), F32)
            bc_i = bc[lo:lo + SUB, :]
            q_i = q[lo:lo + SUB, :]
            k_i = k[lo:lo + SUB, :]
            q_sub.append(q_i * jnp.exp2(bc_i - bs))
            if i > 0:
                k_hat.append(jnp.concatenate(
                    [k[:lo, :] * jnp.exp2(bs - bc[:lo, :]), jnp.zeros((c_rows - lo, dk), F32)], axis=0))
            top = jnp.zeros((half, c_rows), F32)
            bot = jnp.zeros((half, c_rows), F32)
            for s in range(SUB):
                r0 = 0 if s < half else half
                e = jnp.exp2(jnp.minimum(bc_i[r0:, :] - bc_i[s:s + 1, :], 0.0))
                col = jnp.sum(q_i[r0:, :] * k_i[s:s + 1, :] * e, axis=-1, keepdims=True)
                if s < half:
                    top = jnp.where(half_col == lo + s, col[:half, :], top)
                bot = jnp.where(half_col == lo + s, col[half - r0:, :], bot)
            diag = jnp.concatenate([top, bot], axis=0)
            att_rows.append(jnp.where(sub_row + lo >= sub_col, diag, 0.0))
        att = jnp.concatenate(att_rows, axis=0)
        zero = jnp.zeros((SUB, dk), F32)
        lhs = jnp.concatenate(
            [jnp.concatenate([q_sub[i] if j == i else zero for j in range(1, nsub)], axis=1)
             for i in range(nsub)], axis=0)
        rhs = jnp.concatenate(k_hat, axis=1)
        att = att + _dot_nt(lhs.astype(BF16), rhs.astype(BF16))
        o_intra = _dot(att.astype(BF16), vb)
        q_in = (q * jnp.exp2(bc)).astype(BF16)
        k_dec = (k * jnp.exp2(blast - bc)).astype(BF16)
        return o_intra, q_in, k_dec, jnp.exp2(blast), vb

    st_ref[...] = jnp.zeros_like(st_ref)
    chunk_rows = lambda c: pl.ds(((c + rot) % nchunk) * c_rows, c_rows)
    blocks = [[chunk_rows(b * group + j) for j in range(group)] for b in range(nchunk // group)]
    all_bcs = [[cum_decay(rs) for rs in spans] for spans in blocks]
    totals = [-functools.reduce(jnp.add, [bc[c_rows - 1:c_rows, :] for bc in bcs]) for bcs in all_bcs]
    mild = jnp.max(functools.reduce(jnp.maximum, totals)) <= GLA_FAST_RANGE

    @pl.when(mild)
    def _():
        parts = [intra_fast(spans, bcs) for spans, bcs in zip(blocks, all_bcs)]
        st = st_ref[...]
        for spans, (o_intra, q_in, k_dec, decay, vb) in zip(blocks, parts):
            o = (o_intra + _dot_nt(q_in, st.astype(BF16))).astype(o_ref.dtype)
            for j, rs in enumerate(spans):
                o_ref[0, rs, :] = o[j * c_rows:(j + 1) * c_rows, :]
            st = st * decay + _dot_tn(vb, k_dec)
        st_ref[...] = st

    @pl.when(jnp.logical_not(mild))
    def _():
        def body(it, carry):
            spans = []
            for j in range(unroll):
                mem = lax.rem(it * unroll + j + rot, nchunk)
                spans.append(pl.ds(pl.multiple_of(mem * c_rows, c_rows), c_rows))
            parts = [intra_safe(rs, cum_decay(rs)) for rs in spans]
            st = st_ref[...]
            for rs, (o_intra, q_in, k_dec, decay, vb) in zip(spans, parts):
                o_ref[0, rs, :] = (o_intra + _dot_nt(q_in, st.astype(BF16))).astype(o_ref.dtype)
                st = st * decay + _dot_tn(vb, k_dec)
            st_ref[...] = st
            return carry

        lax.fori_loop(0, nchunk // unroll, body, 0)

    s_ref[0, 0] = st_ref[...].T


def _gla_chunked(q, k, gk, v, *, rot):
    b, l, kd = q.shape
    vd = v.shape[2]
    dk, dv = kd // GLA_HEADS, vd // GLA_HEADS
    seq = lambda n: pl.BlockSpec((1, l, n), lambda bi, hi: (bi, 0, hi))
    nchunk = l // CHUNK
    unroll = _first_divisor(nchunk, (11, 3, 2, 1))
    return pl.pallas_call(
        functools.partial(_gla_chunk_kernel, nchunk=nchunk, rot=rot, unroll=unroll,
                          group=_first_divisor(nchunk, (3, 2, 1))),
        grid=(b, GLA_HEADS),
        in_specs=[seq(dk), seq(dk), seq(dk), seq(dv)],
        out_specs=[seq(dv), pl.BlockSpec((1, 1, dk, dv), lambda bi, hi: (bi, hi, 0, 0))],
        out_shape=[jax.ShapeDtypeStruct((b, l, vd), BF16),
                   jax.ShapeDtypeStruct((b, GLA_HEADS, dk, dv), F32)],
        scratch_shapes=[pltpu.VMEM((dv, dk), F32)],
        compiler_params=_cp("parallel", "parallel"),
        name="gla_chunked",
    )(q, k, gk, v)


def _gla_step_kernel(qt_ref, kt_ref, gt_ref, v_ref, s0_ref, o_ref, s_ref):
    bt = v_ref.shape[1]
    dec = jnp.exp(gt_ref[0])
    kt = kt_ref[0]
    qt = qt_ref[0]
    vf = v_ref[0].astype(F32)
    for j in range(bt):
        v_row = vf[j:j + 1, :]
        s_new = dec[:, j:j + 1] * s0_ref[j, 0] + kt[:, j:j + 1] * v_row
        s_ref[j, 0] = s_new
        o_ref[0, j:j + 1, :] = jnp.sum(qt[:, j:j + 1] * s_new, axis=0, keepdims=True)


def _gla_step(q, k, gk, v, s0, bt=32):
    b, kd = q.shape
    vd = v.shape[1]
    dk, dv = kd // GLA_HEADS, vd // GLA_HEADS
    nt = b // bt
    cols = lambda t: t.reshape(nt, bt, kd).transpose(0, 2, 1)
    col_spec = pl.BlockSpec((1, dk, bt), lambda ti, hi: (ti, hi, 0))
    v_spec = pl.BlockSpec((1, bt, dv), lambda ti, hi: (ti, 0, hi))
    s_spec = pl.BlockSpec((bt, 1, dk, dv), lambda ti, hi: (ti, hi, 0, 0))
    o, s = pl.pallas_call(
        _gla_step_kernel,
        grid=(nt, GLA_HEADS),
        in_specs=[col_spec, col_spec, col_spec, v_spec, s_spec],
        out_specs=[v_spec, s_spec],
        out_shape=[jax.ShapeDtypeStruct((nt, bt, vd), F32), jax.ShapeDtypeStruct(s0.shape, F32)],
        compiler_params=_cp("parallel", "parallel"),
        name="gla_step",
    )(cols(q), cols(k), cols(gk), v.reshape(nt, bt, vd), s0)
    return o.reshape(b, vd), s


def _gla_out_kernel(o_ref, gt_ref, h_ref, gn_ref, w_ref, out_ref):
    dv = gn_ref.shape[1]
    parts = []
    for hd in range(o_ref.shape[2] // dv):
        sl = slice(hd * dv, (hd + 1) * dv)
        o_h = o_ref[0, :, sl].astype(F32)
        g_h = gt_ref[0, :, sl].astype(F32)
        parts.append((_rms(o_h, gn_ref[...]) * jax.nn.silu(g_h)).astype(BF16))
    out_ref[0] = h_ref[0] + _dot(jnp.concatenate(parts, axis=1), w_ref[...])


def _gla_out(o, gt, h, gnorm, w, n_rows, tm):
    b, _, d = h.shape
    vd = o.shape[2]
    rows = lambda n: pl.BlockSpec((1, tm, n), lambda bi, i: (bi, i, 0))
    return pl.pallas_call(
        _gla_out_kernel,
        grid=(b, n_rows // tm),
        in_specs=[rows(vd), rows(vd), rows(d), _full(gnorm.shape), _resident(w.shape)],
        out_specs=rows(d),
        out_shape=jax.ShapeDtypeStruct((b, n_rows, d), F32),
        compiler_params=_cp("parallel", "parallel"),
        name="gla_out",
    )(o, gt, h, gnorm, w)


def _moe_kernel(h_ref, g_ref, wr_ref, tri_ref, w1_ref, w3_ref, w2_ref, gf_ref, o_ref,
                pt_ref, sl_ref, gg_ref, meta_ref, *, ne):
    e = pl.program_id(2)
    t = h_ref.shape[1]
    nchunk_max = pt_ref.shape[0]

    @pl.when(e == 0)
    def _():
        h = h_ref[0]
        xn = _rms(h, g_ref[...])
        xb = xn.astype(BF16)
        nsub = wr_ref.shape[0]
        logits = _dot_nt(wr_ref[...], xb)
        sub = lax.broadcasted_iota(jnp.int32, logits.shape, 0)
        valid = sub < ne
        logits = jnp.where(valid, logits, -jnp.inf)
        p = jnp.exp(logits - jnp.max(logits, axis=0, keepdims=True))
        p = p / jnp.sum(p, axis=0, keepdims=True)
        p = jnp.where(valid, p, -1.0)
        m1 = jnp.max(p, axis=0, keepdims=True)
        i1 = jnp.min(jnp.where(p == m1, sub, nsub), axis=0, keepdims=True)
        rest = jnp.where(sub == i1, -1.0, p)
        m2 = jnp.max(rest, axis=0, keepdims=True)
        i2 = jnp.min(jnp.where(rest == m2, sub, nsub), axis=0, keepdims=True)
        tot = m1 + m2
        sel = (sub == i1) | (sub == i2)
        gates = jnp.where(sub == i1, m1 / tot, jnp.where(sub == i2, m2 / tot, 0.0))

        incl = _dot(jnp.where(sel, 1.0, 0.0).astype(BF16), tri_ref[...])
        blocks = jnp.floor((incl[:, t - 1:t] + (MOE_BLK - 1)) * (1.0 / MOE_BLK))
        before = (lax.broadcasted_iota(jnp.int32, (nsub, nsub), 1)
                  < lax.broadcasted_iota(jnp.int32, (nsub, nsub), 0)).astype(BF16)
        off = _dot(before, jnp.broadcast_to(blocks, (nsub, LANES)).astype(BF16))[:, 0:1]
        slot = jnp.where(sel, off * MOE_BLK + incl - 1.0, -1.0)
        s1 = jnp.sum(jnp.where(sub == i1, slot, 0.0), axis=0, keepdims=True)
        s2 = jnp.sum(jnp.where(sub == i2, slot, 0.0), axis=0, keepdims=True)
        g_src = jnp.concatenate([*_split3(gates), jnp.zeros((LANES - 3 * nsub, t), BF16)], axis=0)
        nchunks = jnp.floor((jnp.sum(blocks) + 1.0) * 0.5).astype(jnp.int32)
        for c in range(nchunk_max):
            @pl.when(c < nchunks)
            def _(c=c):
                ids = (lax.broadcasted_iota(jnp.int32, (MOE_CHUNK, t), 0) + c * MOE_CHUNK).astype(F32)
                pc = (jnp.where(s1 == ids, 1.0, 0.0) + jnp.where(s2 == ids, 1.0, 0.0)).astype(BF16)
                rs = slice(c * MOE_CHUNK, (c + 1) * MOE_CHUNK)
                pt_ref[c] = pc
                sl_ref[rs, :] = _dot(pc, xb).astype(BF16)
                gg_ref[rs, :] = _dot_nt(pc, g_src)
        o_ref[0] = h
        for ee in range(ne):
            meta_ref[0, ee] = blocks[ee, 0].astype(jnp.int32)
            meta_ref[1, ee] = off[ee, 0].astype(jnp.int32)
        meta_ref[2, 0] = nchunks

    def expert_rows(first_blk, nblk):
        rows = nblk * MOE_BLK
        rs = pl.ds(pl.multiple_of(first_blk * MOE_BLK, MOE_BLK), rows)
        xg = sl_ref[rs, :]
        a = _dot(xg, w1_ref[0])
        b = _dot(xg, w3_ref[0])
        y = _dot((jax.nn.silu(a) * b).astype(BF16), w2_ref[0])
        lane_b = lax.broadcasted_iota(jnp.int32, (rows, LANES), 1)
        mine = jnp.bitwise_and(lane_b, wr_ref.shape[0] - 1) == e
        gate = jnp.sum(jnp.where(mine, gg_ref[rs, :], 0.0), axis=-1, keepdims=True)
        sl_ref[rs, :] = (gate * y).astype(BF16)

    nblk = meta_ref[0, e]
    first = meta_ref[1, e]
    odd = lax.rem(nblk, 2)
    lead = jnp.where(nblk >= 3, 3 * odd, odd)

    @pl.when(lead == 1)
    def _():
        expert_rows(first, 1)

    @pl.when(lead == 3)
    def _():
        expert_rows(first, 3)

    def pair(jp, carry):
        expert_rows(first + lead + 2 * jp, 2)
        return carry

    lax.fori_loop(0, lax.div(nblk - lead, 2), pair, 0)

    @pl.when(e == ne - 1)
    def _():
        def combine(c, carry):
            rs = pl.ds(pl.multiple_of(c * MOE_CHUNK, MOE_CHUNK), MOE_CHUNK)
            o_ref[0] += _dot_tn(pt_ref[c], sl_ref[rs, :])
            return carry

        lax.fori_loop(0, meta_ref[2, 0], combine, 0)
        o_ref[0] = _rms(o_ref[0], gf_ref[...])


def _moe_final(h, g, w_router, w1, w3, w2, g_final, tm):
    b, l, d = h.shape
    ne, _, f = w1.shape
    slots = -(-(2 * tm + ne * MOE_BLK) // MOE_CHUNK) * MOE_CHUNK
    r = jnp.arange(tm)
    tri = (r[:, None] <= r[None, :]).astype(BF16)
    rows = pl.BlockSpec((1, tm, d), lambda bi, i, e: (bi, i, 0))
    return pl.pallas_call(
        functools.partial(_moe_kernel, ne=ne),
        grid=(b, l // tm, ne),
        in_specs=[rows, _full((1, d)), _full(w_router.shape), _resident((tm, tm)),
                  pl.BlockSpec((1, d, f), lambda bi, i, e: (e, 0, 0)),
                  pl.BlockSpec((1, d, f), lambda bi, i, e: (e, 0, 0)),
                  pl.BlockSpec((1, f, d), lambda bi, i, e: (e, 0, 0)), _full((1, d))],
        out_specs=rows,
        out_shape=jax.ShapeDtypeStruct((b, l, d), F32),
        scratch_shapes=[pltpu.VMEM((slots // MOE_CHUNK, MOE_CHUNK, tm), BF16), pltpu.VMEM((slots, d), BF16),
                        pltpu.VMEM((slots, LANES), F32),
                        pltpu.SMEM((3, ne), jnp.int32)],
        compiler_params=_cp("parallel", "parallel", "arbitrary"),
        name="moe_final",
    )(h, g, w_router, tri, w1, w3, w2, g_final)


def kernel(x_prompt, x_sample, state_pool, state_s5_re, state_s5_im, state_gla, meta_tokens, norm_mix_e, w_in_e, w_pool, pool_scale, s5_a_re, s5_a_im, s5_log_dt, s5_b_re, s5_b_im, s5_c_re, s5_c_im, s5_d, w_glu, b_glu, w_out_e, norm_ffn_e, ffn_w1, ffn_w3, ffn_w2, norm_mix_o, w_in_o, w_gk1, w_gk2, b_gk, gla_norm, w_out_o, norm_ffn_o, w_router, moe_w1, moe_w3, moe_w2, norm_final):
    bp, seq, d = x_prompt.shape
    bs = x_sample.shape[0]
    lp = seq + CHUNK
    dp = w_pool.shape[1] * w_pool.shape[2]
    g_ssm, n_ssm, p_ssm = s5_b_re.shape[1:]
    gn = g_ssm * n_ssm
    kd = w_gk2.shape[2]
    vd = w_out_o.shape[1]
    row = lambda t: t.reshape(1, -1)
    bf = lambda t: t.astype(BF16)

    j = 0
    w_in_e_b, w_out_e_b = bf(w_in_e[j]), bf(w_out_e[j])
    w_pool_b, w_glu_b = bf(w_pool[j]), bf(w_glu[j])
    w1_b, w3_b, w2_b = bf(ffn_w1[j]), bf(ffn_w3[j]), bf(ffn_w2[j])
    lbr, lbi, bbr, bbi = _s5_prep(s5_a_re[j], s5_a_im[j], s5_log_dt[j], s5_b_re[j], s5_b_im[j])
    s5_args = (lbr, lbi, bf(_block_diag_in(bbr, g_ssm, n_ssm, p_ssm)), bf(_block_diag_in(bbi, g_ssm, n_ssm, p_ssm)),
               bf(_block_diag_out(s5_c_re[j])), bf(_block_diag_out(s5_c_im[j])),
               row(s5_d[j]), w_glu_b, row(b_glu[j]))
    w_in_o_b, w_out_o_b = bf(w_in_o[j]), bf(w_out_o[j])
    rank = w_gk1.shape[2]
    wg1 = bf(jnp.pad(w_gk1[j], ((0, 0), (0, LANES - rank))))
    wg2 = bf(jnp.pad(w_gk2[j], ((0, LANES - rank), (0, 0))))
    w_router_b = bf(jnp.pad(w_router[j].T, ((0, MOE_ROUTE_ROWS - w_router.shape[2]), (0, 0))))
    mw1, mw3, mw2 = bf(moe_w1[j]), bf(moe_w3[j]), bf(moe_w2[j])
    mix_w = (row(norm_mix_e[j]), w_in_e_b)
    pool_w = (w_pool_b, row(pool_scale[j]))

    tail = jnp.concatenate([jnp.zeros((CHUNK - N_META, d), F32), meta_tokens.astype(F32)], axis=0)
    zero_state = jnp.zeros((bp, gn), F32)
    h1, pool_p, re_p, im_p = _mixer0(
        x_prompt, tail, *mix_w, jnp.zeros(((POOL_BUF + 1) * bp, dp), F32), *pool_w,
        zero_state, zero_state, s5_args, w_out_e_b, tt=CHUNK, nb=bp, pos0=-(CHUNK - N_META))
    tm = _first_divisor(bp * lp, (768, 512, 256, 128, 8))
    h2 = _ffn(h1.reshape(bp * lp, d), row(norm_ffn_e[j]), w1_b, w3_b, w2_b, tm)
    q, k, gk, v, gt = _gla_proj(h2, row(norm_mix_o[j]), w_in_o_b, wg1, wg2, row(b_gk[j]), kd, vd, tm)
    seq3 = lambda t: t.reshape(bp, lp, t.shape[1])
    o, gla_p = _gla_chunked(seq3(q), seq3(k), seq3(gk), seq3(v), rot=lp // CHUNK - 1)
    ts = _first_divisor(seq, (1024, 512, 256, 128, 8))
    h3 = _gla_out(o, seq3(gt), seq3(h2), row(gla_norm[j]), w_out_o_b, seq, ts)
    y_prompt = _moe_final(h3, row(norm_ffn_o[j]), w_router_b, mw1, mw3, mw2, row(norm_final), ts)

    pool_init = jnp.pad(state_pool[j].transpose(1, 0, 2), ((1, 0), (0, 0), (0, 0))).reshape((POOL_BUF + 1) * bs, dp)
    h1s, pool_s, re_s, im_s = _mixer0(
        x_sample.reshape(1, bs, d), None, *mix_w, pool_init, *pool_w,
        state_s5_re[j].reshape(bs, gn), state_s5_im[j].reshape(bs, gn), s5_args, w_out_e_b,
        tt=1, nb=bs, pos0=PAST_LEN)
    h2s = _ffn(h1s.reshape(bs, d), row(norm_ffn_e[j]), w1_b, w3_b, w2_b, bs)
    qs, ks, gks, vs, gts = _gla_proj(h2s, row(norm_mix_o[j]), w_in_o_b, wg1, wg2, row(b_gk[j]), kd, vd, bs)
    os_, gla_s = _gla_step(qs, ks, gks, vs, state_gla[j])
    h3s = _gla_out(os_[None], gts[None], h2s[None], row(gla_norm[j]), w_out_o_b, bs, bs)
    y_sample = _moe_final(h3s, row(norm_ffn_o[j]), w_router_b, mw1, mw3, mw2, row(norm_final), bs)

    tb = lambda t, nb: t.reshape(POOL_BUF, nb, dp).transpose(1, 0, 2)[None]
    ssm = lambda t, nb: t.reshape(1, nb, g_ssm, n_ssm)
    return (y_prompt, y_sample.reshape(bs, 1, d), tb(pool_p, bp), tb(pool_s, bs),
            ssm(re_p, bp), ssm(re_s, bs), ssm(im_p, bp), ssm(im_s, bs), gla_p[None], gla_s[None])
```

```python
import functools

import jax
import jax.numpy as jnp
from jax import lax
from jax.experimental import pallas as pl
from jax.experimental.pallas import tpu as pltpu

F32 = jnp.float32
BF16 = jnp.bfloat16

EPS = 1e-6
LOG2_E = 1.4426950408889634
N_META = 16
PAST_LEN = 16384
POOL_WINDOWS = (2, 4, 8, 16)
POOL_BUF = max(POOL_WINDOWS) - 1
GLA_HEADS = 4
GLA_GATE_NORM = 16.0
CHUNK = 64
SUB = 16
GLA_FAST_RANGE = 64.0
MXU_DIM = 256
MOE_BLK = 64
MOE_STRIDE = 4
MOE_CHUNK = MXU_DIM
MOE_ROUTE_ROWS = 16
LANES = 128
VMEM_LIMIT = 56 * 1024 * 1024


def _cp(*sem, vmem=VMEM_LIMIT):
    return pltpu.CompilerParams(dimension_semantics=sem, vmem_limit_bytes=vmem)


def _rms(x, g):
    return x * lax.rsqrt(jnp.mean(x * x, axis=-1, keepdims=True) + EPS) * g


def _dot(a, b):
    return jnp.dot(a, b, preferred_element_type=F32)


def _dot_nt(a, b):
    return lax.dot_general(a, b, (((1,), (1,)), ((), ())), preferred_element_type=F32)


def _dot_tn(a, b):
    return lax.dot_general(a, b, (((0,), (0,)), ((), ())), preferred_element_type=F32)


def _full(shape):
    return pl.BlockSpec(shape, lambda *_: (0,) * len(shape))


def _resident(shape):
    return pl.BlockSpec(shape, lambda *_: (0,) * len(shape), pipeline_mode=pl.Buffered(1))


def _first_divisor(n, candidates):
    return next(c for c in candidates if n % c == 0)


def _s5_prep_kernel(ar_ref, ai_ref, ldt_ref, br_ref, bi_ref, lbr_ref, lbi_ref, bbr_ref, bbi_ref):
    dt = jnp.exp(ldt_ref[...])
    ar = ar_ref[...]
    ai = ai_ref[...]
    mag = jnp.exp(ar * dt)
    lb_re = mag * jnp.cos(ai * dt)
    lb_im = mag * jnp.sin(ai * dt)
    den = ar * ar + ai * ai
    nr = lb_re - 1.0
    f_re = (nr * ar + lb_im * ai) / den
    f_im = (lb_im * ar - nr * ai) / den
    lbr_ref[...] = lb_re
    lbi_ref[...] = lb_im
    br = br_ref[...]
    bi = bi_ref[...]
    bbr_ref[...] = f_re * br - f_im * bi
    bbi_ref[...] = f_re * bi + f_im * br


def _s5_prep(a_re, a_im, log_dt, b_re, b_im):
    g, n, p = b_re.shape
    gn = g * n
    row = lambda t: t.reshape(1, gn)
    to_pgn = lambda t: t.transpose(2, 0, 1).reshape(p, gn)
    ldt = jnp.broadcast_to(log_dt[:, None], (g, n))
    return pl.pallas_call(
        _s5_prep_kernel,
        out_shape=[jax.ShapeDtypeStruct((1, gn), F32)] * 2 + [jax.ShapeDtypeStruct((p, gn), F32)] * 2,
        name="s5_prep",
    )(row(a_re), row(a_im), row(ldt), to_pgn(b_re), to_pgn(b_im))


def _block_diag_in(bb_pgn, g, n, p):
    bb = bb_pgn.reshape(p, g, n).transpose(1, 0, 2)
    eye = jnp.eye(g, dtype=bb.dtype)
    full = (bb[:, :, None, :] * eye[:, None, :, None]).reshape(g * p, g * n)
    nblk = g * p // MXU_DIM
    cols = g * n // nblk
    return jnp.stack([full[k * MXU_DIM:(k + 1) * MXU_DIM, k * cols:(k + 1) * cols] for k in range(nblk)])


def _block_diag_out(c_gpn):
    g, p, n = c_gpn.shape
    eye = jnp.eye(g, dtype=c_gpn.dtype)
    full = (c_gpn.transpose(0, 2, 1)[:, :, None, :] * eye[:, None, :, None]).reshape(g * n, g * p)
    nblk = g * p // MXU_DIM
    rows = g * n // nblk
    return jnp.stack([full[k * rows:(k + 1) * rows, k * MXU_DIM:(k + 1) * MXU_DIM] for k in range(nblk)])


def _pool_block(u_a, i, ext_ref, wp_ref, scale_ref, *, tt, nb, pos0):
    rows = tt * nb
    halo = (POOL_BUF + 1) * nb
    shift = nb.bit_length() - 1
    ext_ref[halo:halo + rows, :] = u_a
    t_in = lax.shift_right_logical(lax.broadcasted_iota(jnp.int32, (rows, LANES), 0), shift)
    pos1 = t_in + (pos0 + 1 + i * tt)
    group = u_a.shape[1] // len(POOL_WINDOWS)
    ys = []
    for gi, w in enumerate(POOL_WINDOWS):
        lo = gi * group
        s = ext_ref[(POOL_BUF + 2 - w) * nb:halo + rows, lo:lo + group]
        k = 1
        while k < w:
            n = s.shape[0]
            s = s[k * nb:, :] + s[:n - k * nb, :]
            k *= 2
        cnt = jnp.clip(pos1, 1, w).astype(F32)
        d = s / cnt - u_a[:, lo:lo + group]
        ys.append(_dot(d.astype(BF16), wp_ref[gi]) * scale_ref[:, lo:lo + group])
    tail = ext_ref[rows:rows + halo, :]
    ext_ref[0:halo, :] = tail
    return jnp.concatenate(ys, axis=1), tail[nb:, :]


def _s5_block(u_b, lbr_ref, lbi_ref, wbr_ref, wbi_ref, wcr_ref, wci_ref, dsk_ref, wg_ref, bg_ref,
              bur_ref, bui_ref, hr_ref, hi_ref, *, tt, nb):
    gn = bur_ref.shape[1]
    nkb = wbr_ref.shape[0]
    sb = gn // nkb
    ub = u_b.astype(BF16)
    for kb in range(nkb):
        uk = ub[:, kb * MXU_DIM:(kb + 1) * MXU_DIM]
        bur_ref[:, kb * sb:(kb + 1) * sb] = _dot(uk, wbr_ref[kb])
        bui_ref[:, kb * sb:(kb + 1) * sb] = _dot(uk, wbi_ref[kb])

    cw = 4 * LANES
    for c in range(gn // cw):
        cols = slice(c * cw, (c + 1) * cw)
        lr = jnp.broadcast_to(lbr_ref[:, cols], (nb, cw))
        li = jnp.broadcast_to(lbi_ref[:, cols], (nb, cw))

        def step(t, carry, cols=cols, lr=lr, li=li):
            h_re, h_im = carry
            r = pl.multiple_of(t * nb, nb)
            n_re = lr * h_re - li * h_im + bur_ref[pl.ds(r, nb), cols]
            n_im = lr * h_im + li * h_re + bui_ref[pl.ds(r, nb), cols]
            bur_ref[pl.ds(r, nb), cols] = n_re
            bui_ref[pl.ds(r, nb), cols] = n_im
            return n_re, n_im

        h_re, h_im = lax.fori_loop(0, tt, step, (hr_ref[:, cols], hi_ref[:, cols]),
                                   unroll=min(tt, 8))
        hr_ref[:, cols] = h_re
        hi_ref[:, cols] = h_im

    zs = []
    for kb in range(nkb):
        hrb = bur_ref[:, kb * sb:(kb + 1) * sb].astype(BF16)
        hib = bui_ref[:, kb * sb:(kb + 1) * sb].astype(BF16)
        ch = slice(kb * MXU_DIM, (kb + 1) * MXU_DIM)
        y = _dot(hrb, wcr_ref[kb]) - _dot(hib, wci_ref[kb]) + dsk_ref[:, ch] * u_b[:, ch]
        zs.append(jax.nn.gelu(y))
    z = jnp.concatenate(zs, axis=1)
    return z * jax.nn.sigmoid(_dot(z.astype(BF16), wg_ref[...]) + bg_ref[...])


def _mixer0_kernel(*refs, tt, nb, pos0, prompt):
    if prompt:
        x_ref, tail_ref, perm_ref, permt_ref = refs[:4]
        refs = refs[4:]
    else:
        x_ref = refs[0]
        refs = refs[1:]
    (g_ref, win_ref, pinit_ref, wp_ref, scale_ref, h0r_ref, h0i_ref, lbr_ref, lbi_ref, wbr_ref, wbi_ref,
     wcr_ref, wci_ref, dsk_ref, wg_ref, bg_ref, wout_ref,
     h_ref, pst_ref, str_ref, sti_ref, ext_ref, bur_ref, bui_ref, hr_ref, hi_ref) = refs
    i = pl.program_id(0)
    rows = tt * nb
    d = x_ref.shape[-1]
    dp = ext_ref.shape[1]

    @pl.when(i == 0)
    def _():
        ext_ref[0:(POOL_BUF + 1) * nb, :] = pinit_ref[...]
        hr_ref[...] = h0r_ref[...]
        hi_ref[...] = h0i_ref[...]

    x = x_ref[...]
    if prompt:
        x = jnp.where(i == 0, jnp.broadcast_to(tail_ref[...][None], x.shape), x)
    x2 = x.reshape(rows, d)
    xn = _rms(x2, g_ref[...]).astype(BF16)
    if prompt:
        xn = _dot(perm_ref[...], xn).astype(BF16)
    u = _dot(xn, win_ref[...])
    ya, pool_tail = _pool_block(u[:, :dp], i, ext_ref, wp_ref, scale_ref, tt=tt, nb=nb, pos0=pos0)
    yb = _s5_block(u[:, dp:], lbr_ref, lbi_ref, wbr_ref, wbi_ref, wcr_ref, wci_ref, dsk_ref, wg_ref, bg_ref,
                   bur_ref, bui_ref, hr_ref, hi_ref, tt=tt, nb=nb)
    y = jnp.concatenate([ya, yb], axis=1).astype(BF16)
    if prompt:
        y = _dot(permt_ref[...], y).astype(BF16)
    h_ref[...] = x + _dot(y, wout_ref[...]).reshape(x.shape)

    @pl.when(i == pl.num_programs(0) - 1)
    def _():
        pst_ref[...] = pool_tail
        str_ref[...] = hr_ref[...]
        sti_ref[...] = hi_ref[...]


def _mixer0(x, tail, g, w_in, pool_init, wp, scale, h0r, h0i, s5, w_out, *, tt, nb, pos0):
    prompt = tail is not None
    d = x.shape[-1]
    dp = wp.shape[0] * wp.shape[1]
    gn = h0r.shape[1]
    rows = tt * nb
    halo = (POOL_BUF + 1) * nb
    lbr, lbi, wbr, wbi, wcr, wci, dsk, wg, bg = s5
    if prompt:
        nblk = x.shape[1] // tt + 1
        out_rows = x.shape[1] + tt
        r = jnp.arange(rows)
        perm = (r[:, None] % nb * tt + r[:, None] // nb == r[None, :]).astype(BF16)
        lead = [x, tail, perm, perm.T]
        lead_specs = [pl.BlockSpec((nb, tt, d), lambda i: (0, jnp.maximum(i - 1, 0), 0)),
                      _full((tt, d)), _full((rows, rows)), _full((rows, rows))]
        h_spec = pl.BlockSpec((nb, tt, d), lambda i: (0, (i + nblk - 1) % nblk, 0))
        h_shape = (nb, out_rows, d)
    else:
        nblk = 1
        lead = [x]
        lead_specs = [_full(x.shape)]
        h_spec = _full(x.shape)
        h_shape = x.shape
    args = lead + [g, w_in, pool_init, wp, scale, h0r, h0i, lbr, lbi, wbr, wbi, wcr, wci, dsk, wg, bg, w_out]
    in_specs = lead_specs + [_full(a.shape) for a in args[len(lead):]]
    return pl.pallas_call(
        functools.partial(_mixer0_kernel, tt=tt, nb=nb, pos0=pos0, prompt=prompt),
        grid=(nblk,),
        in_specs=in_specs,
        out_specs=[h_spec, _full((POOL_BUF * nb, dp)), _full((nb, gn)), _full((nb, gn))],
        out_shape=[jax.ShapeDtypeStruct(h_shape, F32), jax.ShapeDtypeStruct((POOL_BUF * nb, dp), F32),
                   jax.ShapeDtypeStruct((nb, gn), F32), jax.ShapeDtypeStruct((nb, gn), F32)],
        scratch_shapes=[pltpu.VMEM((halo + rows, dp), F32), pltpu.VMEM((rows, gn), F32),
                        pltpu.VMEM((rows, gn), F32), pltpu.VMEM((nb, gn), F32), pltpu.VMEM((nb, gn), F32)],
        compiler_params=_cp("arbitrary"),
        name="mixer0",
    )(*args)


def _ffn_kernel(h_ref, g_ref, w1_ref, w3_ref, w2_ref, o_ref, *, fc):
    h = h_ref[...]
    xn = _rms(h, g_ref[...]).astype(BF16)
    acc = h
    for c in range(w1_ref.shape[1] // fc):
        cs = slice(c * fc, (c + 1) * fc)
        a = _dot(xn, w1_ref[:, cs])
        b = _dot(xn, w3_ref[:, cs])
        acc = acc + _dot((jax.nn.silu(a) * b).astype(BF16), w2_ref[cs, :])
    o_ref[...] = acc


def _ffn(h, g, w1, w3, w2, tm):
    r, d = h.shape
    f = w1.shape[1]
    fc = MXU_DIM if f % MXU_DIM == 0 else f
    return pl.pallas_call(
        functools.partial(_ffn_kernel, fc=fc),
        grid=(r // tm,),
        in_specs=[pl.BlockSpec((tm, d), lambda i: (i, 0)), _full((1, d)),
                  _resident((d, f)), _resident((d, f)), _resident((f, d))],
        out_specs=pl.BlockSpec((tm, d), lambda i: (i, 0)),
        out_shape=jax.ShapeDtypeStruct((r, d), F32),
        compiler_params=_cp("parallel"),
        name="ffn",
    )(h, g, w1, w3, w2)


def _gla_proj_kernel(h_ref, g_ref, win_ref, wg1_ref, wg2_ref, bgk_ref,
                     q_ref, k_ref, gk_ref, v_ref, gt_ref, *, q_scale):
    xn = _rms(h_ref[...], g_ref[...]).astype(BF16)
    kd = q_ref.shape[1]
    vd = v_ref.shape[1]
    q_ref[...] = _dot(xn, win_ref[:, 0:kd]) * q_scale
    k_ref[...] = _dot(xn, win_ref[:, kd:2 * kd])
    v_ref[...] = _dot(xn, win_ref[:, 2 * kd:2 * kd + vd]).astype(v_ref.dtype)
    gt_ref[...] = _dot(xn, win_ref[:, 2 * kd + vd:]).astype(gt_ref.dtype)
    low = _dot(xn, wg1_ref[...]).astype(BF16)
    z = _dot(low, wg2_ref[...]) + bgk_ref[...]
    log_sig = jnp.minimum(z, 0.0) - jnp.log1p(jnp.exp(-jnp.abs(z)))
    gk_ref[...] = log_sig / GLA_GATE_NORM


def _gla_proj(h, g, w_in, wg1, wg2, bgk, kd, vd, tm):
    r, d = h.shape
    rows = lambda n: pl.BlockSpec((tm, n), lambda i: (i, 0))
    return pl.pallas_call(
        functools.partial(_gla_proj_kernel, q_scale=float((kd // GLA_HEADS) ** -0.5)),
        grid=(r // tm,),
        in_specs=[rows(d), _full((1, d)), _resident(w_in.shape), _full(wg1.shape), _full(wg2.shape),
                  _full((1, kd))],
        out_specs=[rows(kd), rows(kd), rows(kd), rows(vd), rows(vd)],
        out_shape=[jax.ShapeDtypeStruct((r, kd), F32)] * 3 + [jax.ShapeDtypeStruct((r, vd), BF16)] * 2,
        compiler_params=_cp("parallel"),
        name="gla_proj",
    )(h, g, w_in, wg1, wg2, bgk)


def _split3(x):
    a = x.astype(BF16)
    r = x - a.astype(F32)
    b = r.astype(BF16)
    c = (r - b.astype(F32)).astype(BF16)
    return a, b, c


def _gla_chunk_kernel(q_ref, k_ref, gk_ref, v_ref, o_ref, s_ref, st_ref, *, nchunk, rot, unroll, group):
    c_rows = CHUNK
    dk = q_ref.shape[2]
    nsub = c_rows // SUB
    row_i = lax.broadcasted_iota(jnp.int32, (c_rows, c_rows), 0)
    col_i = lax.broadcasted_iota(jnp.int32, (c_rows, c_rows), 1)
    tri = (row_i >= col_i).astype(BF16)
    sub_row = lax.broadcasted_iota(jnp.int32, (SUB, c_rows), 0)
    sub_col = lax.broadcasted_iota(jnp.int32, (SUB, c_rows), 1)
    half_col = lax.broadcasted_iota(jnp.int32, (SUB // 2, c_rows), 1)

    def cum_decay(rs):
        g1, g2, g3 = _split3(gk_ref[0, rs, :])
        return (_dot(tri, g1) + _dot(tri, g2) + _dot(tri, g3)) * LOG2_E

    def intra_fast(spans, bcs):
        rows = group * c_rows
        load = lambda ref: jnp.concatenate([ref[0, rs, :] for rs in spans], axis=0)
        q, k, vb = load(q_ref), load(k_ref), load(v_ref)
        run, pieces = jnp.zeros((1, dk), F32), []
        for bc in bcs:
            pieces.append(bc + run)
            run = run + bc[c_rows - 1:c_rows, :]
        bc = jnp.concatenate(pieces, axis=0)
        q_in = (q * jnp.exp2(bc)).astype(BF16)
        k_up = (k * jnp.exp2(-bc)).astype(BF16)
        causal = (lax.broadcasted_iota(jnp.int32, (rows, rows), 0)
                  >= lax.broadcasted_iota(jnp.int32, (rows, rows), 1))
        att = jnp.where(causal, _dot_nt(q_in, k_up), 0.0)
        o_intra = _dot(att.astype(BF16), vb)
        k_dec = (k * jnp.exp2(run - bc)).astype(BF16)
        return o_intra, q_in, k_dec, jnp.exp2(run), vb

    def intra_safe(rs, bc):
        q = q_ref[0, rs, :]
        k = k_ref[0, rs, :]
        vb = v_ref[0, rs, :]
        blast = bc[c_rows - 1:c_rows, :]
        q_sub, k_hat = [], []
        att_rows = []
        half = SUB // 2
        for i in range(nsub):
            lo = i * SUB
            bs = bc[lo - 1:lo, :] if i > 0 else jnp.zeros((1, dk), F32)
            bc_i = bc[lo:lo + SUB, :]
            q_i = q[lo:lo + SUB, :]
            k_i = k[lo:lo + SUB, :]
            q_sub.append(q_i * jnp.exp2(bc_i - bs))
            if i > 0:
                k_hat.append(jnp.concatenate(
                    [k[:lo, :] * jnp.exp2(bs - bc[:lo, :]), jnp.zeros((c_rows - lo, dk), F32)], axis=0))
            top = jnp.zeros((half, c_rows), F32)
            bot = jnp.zeros((half, c_rows), F32)
            for s in range(SUB):
                r0 = 0 if s < half else half
                e = jnp.exp2(jnp.minimum(bc_i[r0:, :] - bc_i[s:s + 1, :], 0.0))
                col = jnp.sum(q_i[r0:, :] * k_i[s:s + 1, :] * e, axis=-1, keepdims=True)
                if s < half:
                    top = jnp.where(half_col == lo + s, col[:half, :], top)
                bot = jnp.where(half_col == lo + s, col[half - r0:, :], bot)
            diag = jnp.concatenate([top, bot], axis=0)
            att_rows.append(jnp.where(sub_row + lo >= sub_col, diag, 0.0))
        att = jnp.concatenate(att_rows, axis=0)
        zero = jnp.zeros((SUB, dk), F32)
        lhs = jnp.concatenate(
            [jnp.concatenate([q_sub[i] if j == i else zero for j in range(1, nsub)], axis=1)
             for i in range(nsub)], axis=0)
        rhs = jnp.concatenate(k_hat, axis=1)
        att = att + _dot_nt(lhs.astype(BF16), rhs.astype(BF16))
        o_intra = _dot(att.astype(BF16), vb)
        q_in = (q * jnp.exp2(bc)).astype(BF16)
        k_dec = (k * jnp.exp2(blast - bc)).astype(BF16)
        return o_intra, q_in, k_dec, jnp.exp2(blast), vb

    st_ref[...] = jnp.zeros_like(st_ref)
    chunk_rows = lambda c: pl.ds(((c + rot) % nchunk) * c_rows, c_rows)
    blocks = [[chunk_rows(b * group + j) for j in range(group)] for b in range(nchunk // group)]
    all_bcs = [[cum_decay(rs) for rs in spans] for spans in blocks]
    totals = [-functools.reduce(jnp.add, [bc[c_rows - 1:c_rows, :] for bc in bcs]) for bcs in all_bcs]
    mild = jnp.max(functools.reduce(jnp.maximum, totals)) <= GLA_FAST_RANGE

    @pl.when(mild)
    def _():
        parts = [intra_fast(spans, bcs) for spans, bcs in zip(blocks, all_bcs)]
        st = st_ref[...]
        for spans, (o_intra, q_in, k_dec, decay, vb) in zip(blocks, parts):
            o = (o_intra + _dot_nt(q_in, st.astype(BF16))).astype(o_ref.dtype)
            for j, rs in enumerate(spans):
                o_ref[0, rs, :] = o[j * c_rows:(j + 1) * c_rows, :]
            st = st * decay + _dot_tn(vb, k_dec)
        st_ref[...] = st

    @pl.when(jnp.logical_not(mild))
    def _():
        def body(it, carry):
            spans = []
            for j in range(unroll):
                mem = lax.rem(it * unroll + j + rot, nchunk)
                spans.append(pl.ds(pl.multiple_of(mem * c_rows, c_rows), c_rows))
            parts = [intra_safe(rs, cum_decay(rs)) for rs in spans]
            st = st_ref[...]
            for rs, (o_intra, q_in, k_dec, decay, vb) in zip(spans, parts):
                o_ref[0, rs, :] = (o_intra + _dot_nt(q_in, st.astype(BF16))).astype(o_ref.dtype)
                st = st * decay + _dot_tn(vb, k_dec)
            st_ref[...] = st
            return carry

        lax.fori_loop(0, nchunk // unroll, body, 0)

    s_ref[0, 0] = st_ref[...].T


def _gla_chunked(q, k, gk, v, *, rot):
    b, l, kd = q.shape
    vd = v.shape[2]
    dk, dv = kd // GLA_HEADS, vd // GLA_HEADS
    seq = lambda n: pl.BlockSpec((1, l, n), lambda bi, hi: (bi, 0, hi))
    nchunk = l // CHUNK
    unroll = _first_divisor(nchunk, (11, 3, 2, 1))
    return pl.pallas_call(
        functools.partial(_gla_chunk_kernel, nchunk=nchunk, rot=rot, unroll=unroll,
                          group=_first_divisor(nchunk, (3, 2, 1))),
        grid=(b, GLA_HEADS),
        in_specs=[seq(dk), seq(dk), seq(dk), seq(dv)],
        out_specs=[seq(dv), pl.BlockSpec((1, 1, dk, dv), lambda bi, hi: (bi, hi, 0, 0))],
        out_shape=[jax.ShapeDtypeStruct((b, l, vd), BF16),
                   jax.ShapeDtypeStruct((b, GLA_HEADS, dk, dv), F32)],
        scratch_shapes=[pltpu.VMEM((dv, dk), F32)],
        compiler_params=_cp("parallel", "parallel"),
        name="gla_chunked",
    )(q, k, gk, v)


def _gla_step_kernel(qt_ref, kt_ref, gt_ref, v_ref, s0_ref, o_ref, s_ref):
    bt = v_ref.shape[1]
    dec = jnp.exp(gt_ref[0])
    kt = kt_ref[0]
    qt = qt_ref[0]
    vf = v_ref[0].astype(F32)
    for j in range(bt):
        v_row = vf[j:j + 1, :]
        s_new = dec[:, j:j + 1] * s0_ref[j, 0] + kt[:, j:j + 1] * v_row
        s_ref[j, 0] = s_new
        o_ref[0, j:j + 1, :] = jnp.sum(qt[:, j:j + 1] * s_new, axis=0, keepdims=True)


def _gla_step(q, k, gk, v, s0, bt=32):
    b, kd = q.shape
    vd = v.shape[1]
    dk, dv = kd // GLA_HEADS, vd // GLA_HEADS
    nt = b // bt
    cols = lambda t: t.reshape(nt, bt, kd).transpose(0, 2, 1)
    col_spec = pl.BlockSpec((1, dk, bt), lambda ti, hi: (ti, hi, 0))
    v_spec = pl.BlockSpec((1, bt, dv), lambda ti, hi: (ti, 0, hi))
    s_spec = pl.BlockSpec((bt, 1, dk, dv), lambda ti, hi: (ti, hi, 0, 0))
    o, s = pl.pallas_call(
        _gla_step_kernel,
        grid=(nt, GLA_HEADS),
        in_specs=[col_spec, col_spec, col_spec, v_spec, s_spec],
        out_specs=[v_spec, s_spec],
        out_shape=[jax.ShapeDtypeStruct((nt, bt, vd), F32), jax.ShapeDtypeStruct(s0.shape, F32)],
        compiler_params=_cp("parallel", "parallel"),
        name="gla_step",
    )(cols(q), cols(k), cols(gk), v.reshape(nt, bt, vd), s0)
    return o.reshape(b, vd), s


def _gla_out_kernel(o_ref, gt_ref, h_ref, gn_ref, w_ref, out_ref):
    dv = gn_ref.shape[1]
    parts = []
    for hd in range(o_ref.shape[2] // dv):
        sl = slice(hd * dv, (hd + 1) * dv)
        o_h = o_ref[0, :, sl].astype(F32)
        g_h = gt_ref[0, :, sl].astype(F32)
        parts.append((_rms(o_h, gn_ref[...]) * jax.nn.silu(g_h)).astype(BF16))
    out_ref[0] = h_ref[0] + _dot(jnp.concatenate(parts, axis=1), w_ref[...])


def _gla_out(o, gt, h, gnorm, w, n_rows, tm):
    b, _, d = h.shape
    vd = o.shape[2]
    rows = lambda n: pl.BlockSpec((1, tm, n), lambda bi, i: (bi, i, 0))
    return pl.pallas_call(
        _gla_out_kernel,
        grid=(b, n_rows // tm),
        in_specs=[rows(vd), rows(vd), rows(d), _full(gnorm.shape), _resident(w.shape)],
        out_specs=rows(d),
        out_shape=jax.ShapeDtypeStruct((b, n_rows, d), F32),
        compiler_params=_cp("parallel", "parallel"),
        name="gla_out",
    )(o, gt, h, gnorm, w)


def _moe_kernel(h_ref, g_ref, wr_ref, tri_ref, w1_ref, w3_ref, w2_ref, gf_ref, o_ref,
                pt_ref, sl_ref, gg_ref, meta_ref, *, ne):
    e = pl.program_id(2)
    t = h_ref.shape[1]
    nchunk_max = pt_ref.shape[0]

    @pl.when(e == 0)
    def _():
        h = h_ref[0]
        xn = _rms(h, g_ref[...])
        xb = xn.astype(BF16)
        nsub = wr_ref.shape[0]
        logits = _dot_nt(wr_ref[...], xb)
        sub = lax.broadcasted_iota(jnp.int32, logits.shape, 0)
        valid = sub < ne
        logits = jnp.where(valid, logits, -jnp.inf)
        p = jnp.exp(logits - jnp.max(logits, axis=0, keepdims=True))
        p = p / jnp.sum(p, axis=0, keepdims=True)
        p = jnp.where(valid, p, -1.0)
        m1 = jnp.max(p, axis=0, keepdims=True)
        i1 = jnp.min(jnp.where(p == m1, sub, nsub), axis=0, keepdims=True)
        rest = jnp.where(sub == i1, -1.0, p)
        m2 = jnp.max(rest, axis=0, keepdims=True)
        i2 = jnp.min(jnp.where(rest == m2, sub, nsub), axis=0, keepdims=True)
        tot = m1 + m2
        sel = (sub == i1) | (sub == i2)
        gates = jnp.where(sub == i1, m1 / tot, jnp.where(sub == i2, m2 / tot, 0.0))

        incl = _dot(jnp.where(sel, 1.0, 0.0).astype(BF16), tri_ref[...])
        blocks = jnp.floor((incl[:, t - 1:t] + (MOE_BLK - 1)) * (1.0 / MOE_BLK))
        before = (lax.broadcasted_iota(jnp.int32, (nsub, nsub), 1)
                  < lax.broadcasted_iota(jnp.int32, (nsub, nsub), 0)).astype(BF16)
        off = _dot(before, jnp.broadcast_to(blocks, (nsub, LANES)).astype(BF16))[:, 0:1]
        slot = jnp.where(sel, off * MOE_BLK + incl - 1.0, -1.0)
        s1 = jnp.sum(jnp.where(sub == i1, slot, 0.0), axis=0, keepdims=True)
        s2 = jnp.sum(jnp.where(sub == i2, slot, 0.0), axis=0, keepdims=True)
        g_src = jnp.concatenate([*_split3(gates), jnp.zeros((LANES - 3 * nsub, t), BF16)], axis=0)
        per_chunk = MOE_CHUNK // MOE_BLK
        nchunks = jnp.floor((jnp.sum(blocks) + (per_chunk - 1.0)) * (1.0 / per_chunk)).astype(jnp.int32)
        for c in range(nchunk_max):
            @pl.when(c < nchunks)
            def _(c=c):
                ids = (lax.broadcasted_iota(jnp.int32, (MOE_CHUNK, t), 0) + c * MOE_CHUNK).astype(F32)
                pc = (jnp.where(s1 == ids, 1.0, 0.0) + jnp.where(s2 == ids, 1.0, 0.0)).astype(BF16)
                rs = slice(c * MOE_CHUNK, (c + 1) * MOE_CHUNK)
                pt_ref[c] = pc
                sl_ref[rs, :] = _dot(pc, xb).astype(BF16)
                gg_ref[rs, :] = _dot_nt(pc, g_src)
        o_ref[0] = h
        for ee in range(ne):
            meta_ref[0, ee] = blocks[ee, 0].astype(jnp.int32)
            meta_ref[1, ee] = off[ee, 0].astype(jnp.int32)
        meta_ref[2, 0] = nchunks

    def expert_rows(first_blk, nblk):
        rows = nblk * MOE_BLK
        rs = pl.ds(pl.multiple_of(first_blk * MOE_BLK, MOE_BLK), rows)
        xg = sl_ref[rs, :]
        a = _dot(xg, w1_ref[0])
        b = _dot(xg, w3_ref[0])
        y = _dot((jax.nn.silu(a) * b).astype(BF16), w2_ref[0])
        lane_b = lax.broadcasted_iota(jnp.int32, (rows, LANES), 1)
        mine = jnp.bitwise_and(lane_b, wr_ref.shape[0] - 1) == e
        gate = jnp.sum(jnp.where(mine, gg_ref[rs, :], 0.0), axis=-1, keepdims=True)
        sl_ref[rs, :] = (gate * y).astype(BF16)

    nblk = meta_ref[0, e]
    first = meta_ref[1, e]
    rem = lax.rem(nblk, MOE_STRIDE)
    lead = jnp.where(jnp.logical_and(rem > 0, nblk > MOE_STRIDE), rem + MOE_STRIDE, rem)
    for n in list(range(1, MOE_STRIDE)) + list(range(MOE_STRIDE + 1, 2 * MOE_STRIDE)):
        @pl.when(lead == n)
        def _(n=n):
            expert_rows(first, n)

    def stride(js, carry):
        expert_rows(first + lead + MOE_STRIDE * js, MOE_STRIDE)
        return carry

    lax.fori_loop(0, lax.div(nblk - lead, MOE_STRIDE), stride, 0)

    @pl.when(e == ne - 1)
    def _():
        def combine(c, carry):
            rs = pl.ds(pl.multiple_of(c * MOE_CHUNK, MOE_CHUNK), MOE_CHUNK)
            o_ref[0] += _dot_tn(pt_ref[c], sl_ref[rs, :])
            return carry

        lax.fori_loop(0, meta_ref[2, 0], combine, 0)
        o_ref[0] = _rms(o_ref[0], gf_ref[...])


def _moe_final(h, g, w_router, w1, w3, w2, g_final, tm):
    b, l, d = h.shape
    ne, _, f = w1.shape
    slots = -(-(2 * tm + ne * MOE_BLK) // MOE_CHUNK) * MOE_CHUNK
    r = jnp.arange(tm)
    tri = (r[:, None] <= r[None, :]).astype(BF16)
    rows = pl.BlockSpec((1, tm, d), lambda bi, i, e: (bi, i, 0))
    return pl.pallas_call(
        functools.partial(_moe_kernel, ne=ne),
        grid=(b, l // tm, ne),
        in_specs=[rows, _full((1, d)), _full(w_router.shape), _resident((tm, tm)),
                  pl.BlockSpec((1, d, f), lambda bi, i, e: (e, 0, 0)),
                  pl.BlockSpec((1, d, f), lambda bi, i, e: (e, 0, 0)),
                  pl.BlockSpec((1, f, d), lambda bi, i, e: (e, 0, 0)), _full((1, d))],
        out_specs=rows,
        out_shape=jax.ShapeDtypeStruct((b, l, d), F32),
        scratch_shapes=[pltpu.VMEM((slots // MOE_CHUNK, MOE_CHUNK, tm), BF16), pltpu.VMEM((slots, d), BF16),
                        pltpu.VMEM((slots, LANES), F32),
                        pltpu.SMEM((3, ne), jnp.int32)],
        compiler_params=_cp("parallel", "parallel", "arbitrary"),
        name="moe_final",
    )(h, g, w_router, tri, w1, w3, w2, g_final)


def kernel(x_prompt, x_sample, state_pool, state_s5_re, state_s5_im, state_gla, meta_tokens, norm_mix_e, w_in_e, w_pool, pool_scale, s5_a_re, s5_a_im, s5_log_dt, s5_b_re, s5_b_im, s5_c_re, s5_c_im, s5_d, w_glu, b_glu, w_out_e, norm_ffn_e, ffn_w1, ffn_w3, ffn_w2, norm_mix_o, w_in_o, w_gk1, w_gk2, b_gk, gla_norm, w_out_o, norm_ffn_o, w_router, moe_w1, moe_w3, moe_w2, norm_final):
    bp, seq, d = x_prompt.shape
    bs = x_sample.shape[0]
    lp = seq + CHUNK
    dp = w_pool.shape[1] * w_pool.shape[2]
    g_ssm, n_ssm, p_ssm = s5_b_re.shape[1:]
    gn = g_ssm * n_ssm
    kd = w_gk2.shape[2]
    vd = w_out_o.shape[1]
    row = lambda t: t.reshape(1, -1)
    bf = lambda t: t.astype(BF16)

    j = 0
    w_in_e_b, w_out_e_b = bf(w_in_e[j]), bf(w_out_e[j])
    w_pool_b, w_glu_b = bf(w_pool[j]), bf(w_glu[j])
    w1_b, w3_b, w2_b = bf(ffn_w1[j]), bf(ffn_w3[j]), bf(ffn_w2[j])
    lbr, lbi, bbr, bbi = _s5_prep(s5_a_re[j], s5_a_im[j], s5_log_dt[j], s5_b_re[j], s5_b_im[j])
    s5_args = (lbr, lbi, bf(_block_diag_in(bbr, g_ssm, n_ssm, p_ssm)), bf(_block_diag_in(bbi, g_ssm, n_ssm, p_ssm)),
               bf(_block_diag_out(s5_c_re[j])), bf(_block_diag_out(s5_c_im[j])),
               row(s5_d[j]), w_glu_b, row(b_glu[j]))
    w_in_o_b, w_out_o_b = bf(w_in_o[j]), bf(w_out_o[j])
    rank = w_gk1.shape[2]
    wg1 = bf(jnp.pad(w_gk1[j], ((0, 0), (0, LANES - rank))))
    wg2 = bf(jnp.pad(w_gk2[j], ((0, LANES - rank), (0, 0))))
    w_router_b = bf(jnp.pad(w_router[j].T, ((0, MOE_ROUTE_ROWS - w_router.shape[2]), (0, 0))))
    mw1, mw3, mw2 = bf(moe_w1[j]), bf(moe_w3[j]), bf(moe_w2[j])
    mix_w = (row(norm_mix_e[j]), w_in_e_b)
    pool_w = (w_pool_b, row(pool_scale[j]))

    tail = jnp.concatenate([jnp.zeros((CHUNK - N_META, d), F32), meta_tokens.astype(F32)], axis=0)
    zero_state = jnp.zeros((bp, gn), F32)
    h1, pool_p, re_p, im_p = _mixer0(
        x_prompt, tail, *mix_w, jnp.zeros(((POOL_BUF + 1) * bp, dp), F32), *pool_w,
        zero_state, zero_state, s5_args, w_out_e_b, tt=CHUNK, nb=bp, pos0=-(CHUNK - N_META))
    tm = _first_divisor(bp * lp, (768, 512, 256, 128, 8))
    h2 = _ffn(h1.reshape(bp * lp, d), row(norm_ffn_e[j]), w1_b, w3_b, w2_b, tm)
    q, k, gk, v, gt = _gla_proj(h2, row(norm_mix_o[j]), w_in_o_b, wg1, wg2, row(b_gk[j]), kd, vd, tm)
    seq3 = lambda t: t.reshape(bp, lp, t.shape[1])
    o, gla_p = _gla_chunked(seq3(q), seq3(k), seq3(gk), seq3(v), rot=lp // CHUNK - 1)
    ts = _first_divisor(seq, (1024, 512, 256, 128, 8))
    h3 = _gla_out(o, seq3(gt), seq3(h2), row(gla_norm[j]), w_out_o_b, seq, ts)
    y_prompt = _moe_final(h3, row(norm_ffn_o[j]), w_router_b, mw1, mw3, mw2, row(norm_final), ts)

    pool_init = jnp.pad(state_pool[j].transpose(1, 0, 2), ((1, 0), (0, 0), (0, 0))).reshape((POOL_BUF + 1) * bs, dp)
    h1s, pool_s, re_s, im_s = _mixer0(
        x_sample.reshape(1, bs, d), None, *mix_w, pool_init, *pool_w,
        state_s5_re[j].reshape(bs, gn), state_s5_im[j].reshape(bs, gn), s5_args, w_out_e_b,
        tt=1, nb=bs, pos0=PAST_LEN)
    h2s = _ffn(h1s.reshape(bs, d), row(norm_ffn_e[j]), w1_b, w3_b, w2_b, bs)
    qs, ks, gks, vs, gts = _gla_proj(h2s, row(norm_mix_o[j]), w_in_o_b, wg1, wg2, row(b_gk[j]), kd, vd, bs)
    os_, gla_s = _gla_step(qs, ks, gks, vs, state_gla[j])
    h3s = _gla_out(os_[None], gts[None], h2s[None], row(gla_norm[j]), w_out_o_b, bs, bs)
    y_sample = _moe_final(h3s, row(norm_ffn_o[j]), w_router_b, mw1, mw3, mw2, row(norm_final), bs)

    tb = lambda t, nb: t.reshape(POOL_BUF, nb, dp).transpose(1, 0, 2)[None]
    ssm = lambda t, nb: t.reshape(1, nb, g_ssm, n_ssm)
    return (y_prompt, y_sample.reshape(bs, 1, d), tb(pool_p, bp), tb(pool_s, bs),
            ssm(re_p, bp), ssm(re_s, bs), ssm(im_p, bp), ssm(im_s, bs), gla_p[None], gla_s[None])
```

```python
import functools

import jax
import jax.numpy as jnp
from jax import lax
from jax.experimental import pallas as pl
from jax.experimental.pallas import tpu as pltpu

F32 = jnp.float32
BF16 = jnp.bfloat16

EPS = 1e-6
LOG2_E = 1.4426950408889634
N_META = 16
PAST_LEN = 16384
POOL_WINDOWS = (2, 4, 8, 16)
POOL_BUF = max(POOL_WINDOWS) - 1
GLA_HEADS = 4
GLA_GATE_NORM = 16.0
CHUNK = 64
SUB = 16
GLA_FAST_RANGE = 64.0
MXU_DIM = 256
MOE_BLK = 64
MOE_STRIDE = 4
MOE_CHUNK = MXU_DIM
MOE_ROUTE_ROWS = 16
LANES = 128
VMEM_LIMIT = 56 * 1024 * 1024


def _cp(*sem, vmem=VMEM_LIMIT):
    return pltpu.CompilerParams(dimension_semantics=sem, vmem_limit_bytes=vmem)


def _rms(x, g):
    return x * lax.rsqrt(jnp.mean(x * x, axis=-1, keepdims=True) + EPS) * g


def _dot(a, b):
    return jnp.dot(a, b, preferred_element_type=F32)


def _dot_nt(a, b):
    return lax.dot_general(a, b, (((1,), (1,)), ((), ())), preferred_element_type=F32)


def _dot_tn(a, b):
    return lax.dot_general(a, b, (((0,), (0,)), ((), ())), preferred_element_type=F32)


def _full(shape):
    return pl.BlockSpec(shape, lambda *_: (0,) * len(shape))


def _resident(shape):
    return pl.BlockSpec(shape, lambda *_: (0,) * len(shape), pipeline_mode=pl.Buffered(1))


def _first_divisor(n, candidates):
    return next(c for c in candidates if n % c == 0)


def _s5_prep_kernel(ar_ref, ai_ref, ldt_ref, br_ref, bi_ref, lbr_ref, lbi_ref, bbr_ref, bbi_ref):
    dt = jnp.exp(ldt_ref[...])
    ar = ar_ref[...]
    ai = ai_ref[...]
    mag = jnp.exp(ar * dt)
    lb_re = mag * jnp.cos(ai * dt)
    lb_im = mag * jnp.sin(ai * dt)
    den = ar * ar + ai * ai
    nr = lb_re - 1.0
    f_re = (nr * ar + lb_im * ai) / den
    f_im = (lb_im * ar - nr * ai) / den
    lbr_ref[...] = lb_re
    lbi_ref[...] = lb_im
    br = br_ref[...]
    bi = bi_ref[...]
    bbr_ref[...] = f_re * br - f_im * bi
    bbi_ref[...] = f_re * bi + f_im * br


def _s5_prep(a_re, a_im, log_dt, b_re, b_im):
    g, n, p = b_re.shape
    gn = g * n
    row = lambda t: t.reshape(1, gn)
    to_pgn = lambda t: t.transpose(2, 0, 1).reshape(p, gn)
    ldt = jnp.broadcast_to(log_dt[:, None], (g, n))
    return pl.pallas_call(
        _s5_prep_kernel,
        out_shape=[jax.ShapeDtypeStruct((1, gn), F32)] * 2 + [jax.ShapeDtypeStruct((p, gn), F32)] * 2,
        name="s5_prep",
    )(row(a_re), row(a_im), row(ldt), to_pgn(b_re), to_pgn(b_im))


def _block_diag_in(bb_pgn, g, n, p):
    bb = bb_pgn.reshape(p, g, n).transpose(1, 0, 2)
    eye = jnp.eye(g, dtype=bb.dtype)
    full = (bb[:, :, None, :] * eye[:, None, :, None]).reshape(g * p, g * n)
    nblk = g * p // MXU_DIM
    cols = g * n // nblk
    return jnp.stack([full[k * MXU_DIM:(k + 1) * MXU_DIM, k * cols:(k + 1) * cols] for k in range(nblk)])


def _block_diag_out(c_gpn):
    g, p, n = c_gpn.shape
    eye = jnp.eye(g, dtype=c_gpn.dtype)
    full = (c_gpn.transpose(0, 2, 1)[:, :, None, :] * eye[:, None, :, None]).reshape(g * n, g * p)
    nblk = g * p // MXU_DIM
    rows = g * n // nblk
    return jnp.stack([full[k * rows:(k + 1) * rows, k * MXU_DIM:(k + 1) * MXU_DIM] for k in range(nblk)])


def _pool_block(u_a, i, ext_ref, wp_ref, scale_ref, *, tt, nb, pos0):
    rows = tt * nb
    halo = (POOL_BUF + 1) * nb
    shift = nb.bit_length() - 1
    ext_ref[halo:halo + rows, :] = u_a
    t_in = lax.shift_right_logical(lax.broadcasted_iota(jnp.int32, (rows, LANES), 0), shift)
    pos1 = t_in + (pos0 + 1 + i * tt)
    group = u_a.shape[1] // len(POOL_WINDOWS)
    ys = []
    for gi, w in enumerate(POOL_WINDOWS):
        lo = gi * group
        s = ext_ref[(POOL_BUF + 2 - w) * nb:halo + rows, lo:lo + group]
        k = 1
        while k < w:
            n = s.shape[0]
            s = s[k * nb:, :] + s[:n - k * nb, :]
            k *= 2
        cnt = jnp.clip(pos1, 1, w).astype(F32)
        d = s / cnt - u_a[:, lo:lo + group]
        ys.append(_dot(d.astype(BF16), wp_ref[gi]) * scale_ref[:, lo:lo + group])
    tail = ext_ref[rows:rows + halo, :]
    ext_ref[0:halo, :] = tail
    return jnp.concatenate(ys, axis=1), tail[nb:, :]


def _s5_block(u_b, lbr_ref, lbi_ref, wbr_ref, wbi_ref, wcr_ref, wci_ref, dsk_ref, wg_ref, bg_ref,
              bur_ref, bui_ref, hr_ref, hi_ref, *, tt, nb):
    gn = bur_ref.shape[1]
    nkb = wbr_ref.shape[0]
    sb = gn // nkb
    ub = u_b.astype(BF16)
    for kb in range(nkb):
        uk = ub[:, kb * MXU_DIM:(kb + 1) * MXU_DIM]
        bur_ref[:, kb * sb:(kb + 1) * sb] = _dot(uk, wbr_ref[kb])
        bui_ref[:, kb * sb:(kb + 1) * sb] = _dot(uk, wbi_ref[kb])

    cw = 4 * LANES
    for c in range(gn // cw):
        cols = slice(c * cw, (c + 1) * cw)
        lr = jnp.broadcast_to(lbr_ref[:, cols], (nb, cw))
        li = jnp.broadcast_to(lbi_ref[:, cols], (nb, cw))

        h_re, h_im = hr_ref[:, cols], hi_ref[:, cols]
        for t in range(tt):
            rows = slice(t * nb, (t + 1) * nb)
            h_re, h_im = (lr * h_re - li * h_im + bur_ref[rows, cols],
                          lr * h_im + li * h_re + bui_ref[rows, cols])
            bur_ref[rows, cols] = h_re
            bui_ref[rows, cols] = h_im
        hr_ref[:, cols] = h_re
        hi_ref[:, cols] = h_im

    zs = []
    for kb in range(nkb):
        hrb = bur_ref[:, kb * sb:(kb + 1) * sb].astype(BF16)
        hib = bui_ref[:, kb * sb:(kb + 1) * sb].astype(BF16)
        ch = slice(kb * MXU_DIM, (kb + 1) * MXU_DIM)
        y = _dot(hrb, wcr_ref[kb]) - _dot(hib, wci_ref[kb]) + dsk_ref[:, ch] * u_b[:, ch]
        zs.append(jax.nn.gelu(y))
    z = jnp.concatenate(zs, axis=1)
    return z * jax.nn.sigmoid(_dot(z.astype(BF16), wg_ref[...]) + bg_ref[...])


def _mixer0_kernel(*refs, tt, nb, pos0, prompt):
    if prompt:
        x_ref, tail_ref, perm_ref, permt_ref = refs[:4]
        refs = refs[4:]
    else:
        x_ref = refs[0]
        refs = refs[1:]
    (g_ref, win_ref, pinit_ref, wp_ref, scale_ref, h0r_ref, h0i_ref, lbr_ref, lbi_ref, wbr_ref, wbi_ref,
     wcr_ref, wci_ref, dsk_ref, wg_ref, bg_ref, wout_ref,
     h_ref, pst_ref, str_ref, sti_ref, ext_ref, bur_ref, bui_ref, hr_ref, hi_ref) = refs
    i = pl.program_id(0)
    rows = tt * nb
    d = x_ref.shape[-1]
    dp = ext_ref.shape[1]

    @pl.when(i == 0)
    def _():
        ext_ref[0:(POOL_BUF + 1) * nb, :] = pinit_ref[...]
        hr_ref[...] = h0r_ref[...]
        hi_ref[...] = h0i_ref[...]

    x = x_ref[...]
    if prompt:
        x = jnp.where(i == 0, jnp.broadcast_to(tail_ref[...][None], x.shape), x)
    x2 = x.reshape(rows, d)
    xn = _rms(x2, g_ref[...]).astype(BF16)
    if prompt:
        xn = _dot(perm_ref[...], xn).astype(BF16)
    u = _dot(xn, win_ref[...])
    ya, pool_tail = _pool_block(u[:, :dp], i, ext_ref, wp_ref, scale_ref, tt=tt, nb=nb, pos0=pos0)
    yb = _s5_block(u[:, dp:], lbr_ref, lbi_ref, wbr_ref, wbi_ref, wcr_ref, wci_ref, dsk_ref, wg_ref, bg_ref,
                   bur_ref, bui_ref, hr_ref, hi_ref, tt=tt, nb=nb)
    y = jnp.concatenate([ya, yb], axis=1).astype(BF16)
    if prompt:
        y = _dot(permt_ref[...], y).astype(BF16)
    h_ref[...] = x + _dot(y, wout_ref[...]).reshape(x.shape)

    @pl.when(i == pl.num_programs(0) - 1)
    def _():
        pst_ref[...] = pool_tail
        str_ref[...] = hr_ref[...]
        sti_ref[...] = hi_ref[...]


def _mixer0(x, tail, g, w_in, pool_init, wp, scale, h0r, h0i, s5, w_out, *, tt, nb, pos0):
    prompt = tail is not None
    d = x.shape[-1]
    dp = wp.shape[0] * wp.shape[1]
    gn = h0r.shape[1]
    rows = tt * nb
    halo = (POOL_BUF + 1) * nb
    lbr, lbi, wbr, wbi, wcr, wci, dsk, wg, bg = s5
    if prompt:
        nblk = x.shape[1] // tt + 1
        out_rows = x.shape[1] + tt
        r = jnp.arange(rows)
        perm = (r[:, None] % nb * tt + r[:, None] // nb == r[None, :]).astype(BF16)
        lead = [x, tail, perm, perm.T]
        lead_specs = [pl.BlockSpec((nb, tt, d), lambda i: (0, jnp.maximum(i - 1, 0), 0)),
                      _full((tt, d)), _full((rows, rows)), _full((rows, rows))]
        h_spec = pl.BlockSpec((nb, tt, d), lambda i: (0, (i + nblk - 1) % nblk, 0))
        h_shape = (nb, out_rows, d)
    else:
        nblk = 1
        lead = [x]
        lead_specs = [_full(x.shape)]
        h_spec = _full(x.shape)
        h_shape = x.shape
    args = lead + [g, w_in, pool_init, wp, scale, h0r, h0i, lbr, lbi, wbr, wbi, wcr, wci, dsk, wg, bg, w_out]
    in_specs = lead_specs + [_full(a.shape) for a in args[len(lead):]]
    return pl.pallas_call(
        functools.partial(_mixer0_kernel, tt=tt, nb=nb, pos0=pos0, prompt=prompt),
        grid=(nblk,),
        in_specs=in_specs,
        out_specs=[h_spec, _full((POOL_BUF * nb, dp)), _full((nb, gn)), _full((nb, gn))],
        out_shape=[jax.ShapeDtypeStruct(h_shape, F32), jax.ShapeDtypeStruct((POOL_BUF * nb, dp), F32),
                   jax.ShapeDtypeStruct((nb, gn), F32), jax.ShapeDtypeStruct((nb, gn), F32)],
        scratch_shapes=[pltpu.VMEM((halo + rows, dp), F32), pltpu.VMEM((rows, gn), F32),
                        pltpu.VMEM((rows, gn), F32), pltpu.VMEM((nb, gn), F32), pltpu.VMEM((nb, gn), F32)],
        compiler_params=_cp("arbitrary"),
        name="mixer0",
    )(*args)


def _ffn_kernel(h_ref, g_ref, w1_ref, w3_ref, w2_ref, o_ref, *, fc):
    h = h_ref[...]
    xn = _rms(h, g_ref[...]).astype(BF16)
    acc = h
    for c in range(w1_ref.shape[1] // fc):
        cs = slice(c * fc, (c + 1) * fc)
        a = _dot(xn, w1_ref[:, cs])
        b = _dot(xn, w3_ref[:, cs])
        acc = acc + _dot((jax.nn.silu(a) * b).astype(BF16), w2_ref[cs, :])
    o_ref[...] = acc


def _ffn(h, g, w1, w3, w2, tm):
    r, d = h.shape
    f = w1.shape[1]
    fc = MXU_DIM if f % MXU_DIM == 0 else f
    return pl.pallas_call(
        functools.partial(_ffn_kernel, fc=fc),
        grid=(r // tm,),
        in_specs=[pl.BlockSpec((tm, d), lambda i: (i, 0)), _full((1, d)),
                  _resident((d, f)), _resident((d, f)), _resident((f, d))],
        out_specs=pl.BlockSpec((tm, d), lambda i: (i, 0)),
        out_shape=jax.ShapeDtypeStruct((r, d), F32),
        compiler_params=_cp("parallel"),
        name="ffn",
    )(h, g, w1, w3, w2)


def _gla_proj_kernel(h_ref, g_ref, win_ref, wg1_ref, wg2_ref, bgk_ref,
                     q_ref, k_ref, gk_ref, v_ref, gt_ref, *, q_scale):
    xn = _rms(h_ref[...], g_ref[...]).astype(BF16)
    kd = q_ref.shape[1]
    vd = v_ref.shape[1]
    q_ref[...] = _dot(xn, win_ref[:, 0:kd]) * q_scale
    k_ref[...] = _dot(xn, win_ref[:, kd:2 * kd])
    v_ref[...] = _dot(xn, win_ref[:, 2 * kd:2 * kd + vd]).astype(v_ref.dtype)
    gt_ref[...] = _dot(xn, win_ref[:, 2 * kd + vd:]).astype(gt_ref.dtype)
    low = _dot(xn, wg1_ref[...]).astype(BF16)
    z = _dot(low, wg2_ref[...]) + bgk_ref[...]
    log_sig = jnp.minimum(z, 0.0) - jnp.log1p(jnp.exp(-jnp.abs(z)))
    gk_ref[...] = log_sig / GLA_GATE_NORM


def _gla_proj(h, g, w_in, wg1, wg2, bgk, kd, vd, tm):
    r, d = h.shape
    rows = lambda n: pl.BlockSpec((tm, n), lambda i: (i, 0))
    return pl.pallas_call(
        functools.partial(_gla_proj_kernel, q_scale=float((kd // GLA_HEADS) ** -0.5)),
        grid=(r // tm,),
        in_specs=[rows(d), _full((1, d)), _resident(w_in.shape), _full(wg1.shape), _full(wg2.shape),
                  _full((1, kd))],
        out_specs=[rows(kd), rows(kd), rows(kd), rows(vd), rows(vd)],
        out_shape=[jax.ShapeDtypeStruct((r, kd), F32)] * 3 + [jax.ShapeDtypeStruct((r, vd), BF16)] * 2,
        compiler_params=_cp("parallel"),
        name="gla_proj",
    )(h, g, w_in, wg1, wg2, bgk)


def _split3(x):
    a = x.astype(BF16)
    r = x - a.astype(F32)
    b = r.astype(BF16)
    c = (r - b.astype(F32)).astype(BF16)
    return a, b, c


def _gla_chunk_kernel(q_ref, k_ref, gk_ref, v_ref, o_ref, s_ref, st_ref, *, nchunk, rot, unroll, group):
    c_rows = CHUNK
    dk = q_ref.shape[2]
    nsub = c_rows // SUB
    row_i = lax.broadcasted_iota(jnp.int32, (c_rows, c_rows), 0)
    col_i = lax.broadcasted_iota(jnp.int32, (c_rows, c_rows), 1)
    tri = (row_i >= col_i).astype(BF16)
    sub_row = lax.broadcasted_iota(jnp.int32, (SUB, c_rows), 0)
    sub_col = lax.broadcasted_iota(jnp.int32, (SUB, c_rows), 1)
    half_col = lax.broadcasted_iota(jnp.int32, (SUB // 2, c_rows), 1)

    def cum_decay(rs):
        g1, g2, g3 = _split3(gk_ref[0, rs, :])
        return (_dot(tri, g1) + _dot(tri, g2) + _dot(tri, g3)) * LOG2_E

    def intra_fast(spans, bcs):
        rows = group * c_rows
        load = lambda ref: jnp.concatenate([ref[0, rs, :] for rs in spans], axis=0)
        q, k, vb = load(q_ref), load(k_ref), load(v_ref)
        run, pieces = jnp.zeros((1, dk), F32), []
        for bc in bcs:
            pieces.append(bc + run)
            run = run + bc[c_rows - 1:c_rows, :]
        bc = jnp.concatenate(pieces, axis=0)
        q_in = (q * jnp.exp2(bc)).astype(BF16)
        k_up = (k * jnp.exp2(-bc)).astype(BF16)
        causal = (lax.broadcasted_iota(jnp.int32, (rows, rows), 0)
                  >= lax.broadcasted_iota(jnp.int32, (rows, rows), 1))
        att = jnp.where(causal, _dot_nt(q_in, k_up), 0.0)
        o_intra = _dot(att.astype(BF16), vb)
        k_dec = (k * jnp.exp2(run - bc)).astype(BF16)
        return o_intra, q_in, k_dec, jnp.exp2(run), vb

    def intra_safe(rs, bc):
        q = q_ref[0, rs, :]
        k = k_ref[0, rs, :]
        vb = v_ref[0, rs, :]
        blast = bc[c_rows - 1:c_rows, :]
        q_sub, k_hat = [], []
        att_rows = []
        half = SUB // 2
        for i in range(nsub):
            lo = i * SUB
            bs = bc[lo - 1:lo, :] if i > 0 else jnp.zeros((1, dk), F32)
            bc_i = bc[lo:lo + SUB, :]
            q_i = q[lo:lo + SUB, :]
            k_i = k[lo:lo + SUB, :]
            q_sub.append(q_i * jnp.exp2(bc_i - bs))
            if i > 0:
                k_hat.append(jnp.concatenate(
                    [k[:lo, :] * jnp.exp2(bs - bc[:lo, :]), jnp.zeros((c_rows - lo, dk), F32)], axis=0))
            top = jnp.zeros((half, c_rows), F32)
            bot = jnp.zeros((half, c_rows), F32)
            for s in range(SUB):
                r0 = 0 if s < half else half
                e = jnp.exp2(jnp.minimum(bc_i[r0:, :] - bc_i[s:s + 1, :], 0.0))
                col = jnp.sum(q_i[r0:, :] * k_i[s:s + 1, :] * e, axis=-1, keepdims=True)
                if s < half:
                    top = jnp.where(half_col == lo + s, col[:half, :], top)
                bot = jnp.where(half_col == lo + s, col[half - r0:, :], bot)
            diag = jnp.concatenate([top, bot], axis=0)
            att_rows.append(jnp.where(sub_row + lo >= sub_col, diag, 0.0))
        att = jnp.concatenate(att_rows, axis=0)
        zero = jnp.zeros((SUB, dk), F32)
        lhs = jnp.concatenate(
            [jnp.concatenate([q_sub[i] if j == i else zero for j in range(1, nsub)], axis=1)
             for i in range(nsub)], axis=0)
        rhs = jnp.concatenate(k_hat, axis=1)
        att = att + _dot_nt(lhs.astype(BF16), rhs.astype(BF16))
        o_intra = _dot(att.astype(BF16), vb)
        q_in = (q * jnp.exp2(bc)).astype(BF16)
        k_dec = (k * jnp.exp2(blast - bc)).astype(BF16)
        return o_intra, q_in, k_dec, jnp.exp2(blast), vb

    st_ref[...] = jnp.zeros_like(st_ref)
    chunk_rows = lambda c: pl.ds(((c + rot) % nchunk) * c_rows, c_rows)
    blocks = [[chunk_rows(b * group + j) for j in range(group)] for b in range(nchunk // group)]
    all_bcs = [[cum_decay(rs) for rs in spans] for spans in blocks]
    totals = [-functools.reduce(jnp.add, [bc[c_rows - 1:c_rows, :] for bc in bcs]) for bcs in all_bcs]
    mild = jnp.max(functools.reduce(jnp.maximum, totals)) <= GLA_FAST_RANGE

    @pl.when(mild)
    def _():
        parts = [intra_fast(spans, bcs) for spans, bcs in zip(blocks, all_bcs)]
        st = st_ref[...]
        for spans, (o_intra, q_in, k_dec, decay, vb) in zip(blocks, parts):
            o = (o_intra + _dot_nt(q_in, st.astype(BF16))).astype(o_ref.dtype)
            for j, rs in enumerate(spans):
                o_ref[0, rs, :] = o[j * c_rows:(j + 1) * c_rows, :]
            st = st * decay + _dot_tn(vb, k_dec)
        st_ref[...] = st

    @pl.when(jnp.logical_not(mild))
    def _():
        def body(it, carry):
            spans = []
            for j in range(unroll):
                mem = lax.rem(it * unroll + j + rot, nchunk)
                spans.append(pl.ds(pl.multiple_of(mem * c_rows, c_rows), c_rows))
            parts = [intra_safe(rs, cum_decay(rs)) for rs in spans]
            st = st_ref[...]
            for rs, (o_intra, q_in, k_dec, decay, vb) in zip(spans, parts):
                o_ref[0, rs, :] = (o_intra + _dot_nt(q_in, st.astype(BF16))).astype(o_ref.dtype)
                st = st * decay + _dot_tn(vb, k_dec)
            st_ref[...] = st
            return carry

        lax.fori_loop(0, nchunk // unroll, body, 0)

    s_ref[0, 0] = st_ref[...].T


def _gla_chunked(q, k, gk, v, *, rot):
    b, l, kd = q.shape
    vd = v.shape[2]
    dk, dv = kd // GLA_HEADS, vd // GLA_HEADS
    seq = lambda n: pl.BlockSpec((1, l, n), lambda bi, hi: (bi, 0, hi))
    nchunk = l // CHUNK
    unroll = _first_divisor(nchunk, (11, 3, 2, 1))
    return pl.pallas_call(
        functools.partial(_gla_chunk_kernel, nchunk=nchunk, rot=rot, unroll=unroll,
                          group=_first_divisor(nchunk, (3, 2, 1))),
        grid=(b, GLA_HEADS),
        in_specs=[seq(dk), seq(dk), seq(dk), seq(dv)],
        out_specs=[seq(dv), pl.BlockSpec((1, 1, dk, dv), lambda bi, hi: (bi, hi, 0, 0))],
        out_shape=[jax.ShapeDtypeStruct((b, l, vd), BF16),
                   jax.ShapeDtypeStruct((b, GLA_HEADS, dk, dv), F32)],
        scratch_shapes=[pltpu.VMEM((dv, dk), F32)],
        compiler_params=_cp("parallel", "parallel"),
        name="gla_chunked",
    )(q, k, gk, v)


def _gla_step_kernel(qt_ref, kt_ref, gt_ref, v_ref, s0_ref, o_ref, s_ref):
    bt = v_ref.shape[1]
    dec = jnp.exp(gt_ref[0])
    kt = kt_ref[0]
    qt = qt_ref[0]
    vf = v_ref[0].astype(F32)
    for j in range(bt):
        v_row = vf[j:j + 1, :]
        s_new = dec[:, j:j + 1] * s0_ref[j, 0] + kt[:, j:j + 1] * v_row
        s_ref[j, 0] = s_new
        o_ref[0, j:j + 1, :] = jnp.sum(qt[:, j:j + 1] * s_new, axis=0, keepdims=True)


def _gla_step(q, k, gk, v, s0, bt=32):
    b, kd = q.shape
    vd = v.shape[1]
    dk, dv = kd // GLA_HEADS, vd // GLA_HEADS
    nt = b // bt
    cols = lambda t: t.reshape(nt, bt, kd).transpose(0, 2, 1)
    col_spec = pl.BlockSpec((1, dk, bt), lambda ti, hi: (ti, hi, 0))
    v_spec = pl.BlockSpec((1, bt, dv), lambda ti, hi: (ti, 0, hi))
    s_spec = pl.BlockSpec((bt, 1, dk, dv), lambda ti, hi: (ti, hi, 0, 0))
    o, s = pl.pallas_call(
        _gla_step_kernel,
        grid=(nt, GLA_HEADS),
        in_specs=[col_spec, col_spec, col_spec, v_spec, s_spec],
        out_specs=[v_spec, s_spec],
        out_shape=[jax.ShapeDtypeStruct((nt, bt, vd), F32), jax.ShapeDtypeStruct(s0.shape, F32)],
        compiler_params=_cp("parallel", "parallel"),
        name="gla_step",
    )(cols(q), cols(k), cols(gk), v.reshape(nt, bt, vd), s0)
    return o.reshape(b, vd), s


def _gla_out_kernel(o_ref, gt_ref, h_ref, gn_ref, w_ref, out_ref):
    dv = gn_ref.shape[1]
    parts = []
    for hd in range(o_ref.shape[2] // dv):
        sl = slice(hd * dv, (hd + 1) * dv)
        o_h = o_ref[0, :, sl].astype(F32)
        g_h = gt_ref[0, :, sl].astype(F32)
        parts.append((_rms(o_h, gn_ref[...]) * jax.nn.silu(g_h)).astype(BF16))
    out_ref[0] = h_ref[0] + _dot(jnp.concatenate(parts, axis=1), w_ref[...])


def _gla_out(o, gt, h, gnorm, w, n_rows, tm):
    b, _, d = h.shape
    vd = o.shape[2]
    rows = lambda n: pl.BlockSpec((1, tm, n), lambda bi, i: (bi, i, 0))
    return pl.pallas_call(
        _gla_out_kernel,
        grid=(b, n_rows // tm),
        in_specs=[rows(vd), rows(vd), rows(d), _full(gnorm.shape), _resident(w.shape)],
        out_specs=rows(d),
        out_shape=jax.ShapeDtypeStruct((b, n_rows, d), F32),
        compiler_params=_cp("parallel", "parallel"),
        name="gla_out",
    )(o, gt, h, gnorm, w)


def _moe_kernel(h_ref, g_ref, wr_ref, tri_ref, w1_ref, w3_ref, w2_ref, gf_ref, o_ref,
                pt_ref, sl_ref, gg_ref, meta_ref, *, ne):
    e = pl.program_id(2)
    t = h_ref.shape[1]
    nchunk_max = pt_ref.shape[0]

    @pl.when(e == 0)
    def _():
        h = h_ref[0]
        xn = _rms(h, g_ref[...])
        xb = xn.astype(BF16)
        nsub = wr_ref.shape[0]
        logits = _dot_nt(wr_ref[...], xb)
        sub = lax.broadcasted_iota(jnp.int32, logits.shape, 0)
        valid = sub < ne
        logits = jnp.where(valid, logits, -jnp.inf)
        p = jnp.exp(logits - jnp.max(logits, axis=0, keepdims=True))
        p = p / jnp.sum(p, axis=0, keepdims=True)
        p = jnp.where(valid, p, -1.0)
        m1 = jnp.max(p, axis=0, keepdims=True)
        i1 = jnp.min(jnp.where(p == m1, sub, nsub), axis=0, keepdims=True)
        rest = jnp.where(sub == i1, -1.0, p)
        m2 = jnp.max(rest, axis=0, keepdims=True)
        i2 = jnp.min(jnp.where(rest == m2, sub, nsub), axis=0, keepdims=True)
        tot = m1 + m2
        sel = (sub == i1) | (sub == i2)
        gates = jnp.where(sub == i1, m1 / tot, jnp.where(sub == i2, m2 / tot, 0.0))

        incl = _dot(jnp.where(sel, 1.0, 0.0).astype(BF16), tri_ref[...])
        blocks = jnp.floor((incl[:, t - 1:t] + (MOE_BLK - 1)) * (1.0 / MOE_BLK))
        before = (lax.broadcasted_iota(jnp.int32, (nsub, nsub), 1)
                  < lax.broadcasted_iota(jnp.int32, (nsub, nsub), 0)).astype(BF16)
        off = _dot(before, jnp.broadcast_to(blocks, (nsub, LANES)).astype(BF16))[:, 0:1]
        slot = jnp.where(sel, off * MOE_BLK + incl - 1.0, -1.0)
        s1 = jnp.sum(jnp.where(sub == i1, slot, 0.0), axis=0, keepdims=True)
        s2 = jnp.sum(jnp.where(sub == i2, slot, 0.0), axis=0, keepdims=True)
        g_src = jnp.concatenate([*_split3(gates), jnp.zeros((LANES - 3 * nsub, t), BF16)], axis=0)
        per_chunk = MOE_CHUNK // MOE_BLK
        nchunks = jnp.floor((jnp.sum(blocks) + (per_chunk - 1.0)) * (1.0 / per_chunk)).astype(jnp.int32)
        for c in range(nchunk_max):
            @pl.when(c < nchunks)
            def _(c=c):
                ids = (lax.broadcasted_iota(jnp.int32, (MOE_CHUNK, t), 0) + c * MOE_CHUNK).astype(F32)
                pc = (jnp.where(s1 == ids, 1.0, 0.0) + jnp.where(s2 == ids, 1.0, 0.0)).astype(BF16)
                rs = slice(c * MOE_CHUNK, (c + 1) * MOE_CHUNK)
                pt_ref[c] = pc
                sl_ref[rs, :] = _dot(pc, xb).astype(BF16)
                gg_ref[rs, :] = _dot_nt(pc, g_src)
        o_ref[0] = h
        for ee in range(ne):
            meta_ref[0, ee] = blocks[ee, 0].astype(jnp.int32)
            meta_ref[1, ee] = off[ee, 0].astype(jnp.int32)
        meta_ref[2, 0] = nchunks

    def expert_rows(first_blk, nblk):
        rows = nblk * MOE_BLK
        rs = pl.ds(pl.multiple_of(first_blk * MOE_BLK, MOE_BLK), rows)
        xg = sl_ref[rs, :]
        a = _dot(xg, w1_ref[0])
        b = _dot(xg, w3_ref[0])
        y = _dot((jax.nn.silu(a) * b).astype(BF16), w2_ref[0])
        lane_b = lax.broadcasted_iota(jnp.int32, (rows, LANES), 1)
        mine = jnp.bitwise_and(lane_b, wr_ref.shape[0] - 1) == e
        gate = jnp.sum(jnp.where(mine, gg_ref[rs, :], 0.0), axis=-1, keepdims=True)
        sl_ref[rs, :] = (gate * y).astype(BF16)

    nblk = meta_ref[0, e]
    first = meta_ref[1, e]
    rem = lax.rem(nblk, MOE_STRIDE)
    lead = jnp.where(jnp.logical_and(rem > 0, nblk > MOE_STRIDE), rem + MOE_STRIDE, rem)
    for n in list(range(1, MOE_STRIDE)) + list(range(MOE_STRIDE + 1, 2 * MOE_STRIDE)):
        @pl.when(lead == n)
        def _(n=n):
            expert_rows(first, n)

    def stride(js, carry):
        expert_rows(first + lead + MOE_STRIDE * js, MOE_STRIDE)
        return carry

    lax.fori_loop(0, lax.div(nblk - lead, MOE_STRIDE), stride, 0)

    @pl.when(e == ne - 1)
    def _():
        def combine(c, carry):
            rs = pl.ds(pl.multiple_of(c * MOE_CHUNK, MOE_CHUNK), MOE_CHUNK)
            o_ref[0] += _dot_tn(pt_ref[c], sl_ref[rs, :])
            return carry

        lax.fori_loop(0, meta_ref[2, 0], combine, 0)
        o_ref[0] = _rms(o_ref[0], gf_ref[...])


def _moe_final(h, g, w_router, w1, w3, w2, g_final, tm):
    b, l, d = h.shape
    ne, _, f = w1.shape
    slots = -(-(2 * tm + ne * MOE_BLK) // MOE_CHUNK) * MOE_CHUNK
    r = jnp.arange(tm)
    tri = (r[:, None] <= r[None, :]).astype(BF16)
    rows = pl.BlockSpec((1, tm, d), lambda bi, i, e: (bi, i, 0))
    return pl.pallas_call(
        functools.partial(_moe_kernel, ne=ne),
        grid=(b, l // tm, ne),
        in_specs=[rows, _full((1, d)), _full(w_router.shape), _resident((tm, tm)),
                  pl.BlockSpec((1, d, f), lambda bi, i, e: (e, 0, 0)),
                  pl.BlockSpec((1, d, f), lambda bi, i, e: (e, 0, 0)),
                  pl.BlockSpec((1, f, d), lambda bi, i, e: (e, 0, 0)), _full((1, d))],
        out_specs=rows,
        out_shape=jax.ShapeDtypeStruct((b, l, d), F32),
        scratch_shapes=[pltpu.VMEM((slots // MOE_CHUNK, MOE_CHUNK, tm), BF16), pltpu.VMEM((slots, d), BF16),
                        pltpu.VMEM((slots, LANES), F32),
                        pltpu.SMEM((3, ne), jnp.int32)],
        compiler_params=_cp("parallel", "parallel", "arbitrary"),
        name="moe_final",
    )(h, g, w_router, tri, w1, w3, w2, g_final)


def kernel(x_prompt, x_sample, state_pool, state_s5_re, state_s5_im, state_gla, meta_tokens, norm_mix_e, w_in_e, w_pool, pool_scale, s5_a_re, s5_a_im, s5_log_dt, s5_b_re, s5_b_im, s5_c_re, s5_c_im, s5_d, w_glu, b_glu, w_out_e, norm_ffn_e, ffn_w1, ffn_w3, ffn_w2, norm_mix_o, w_in_o, w_gk1, w_gk2, b_gk, gla_norm, w_out_o, norm_ffn_o, w_router, moe_w1, moe_w3, moe_w2, norm_final):
    bp, seq, d = x_prompt.shape
    bs = x_sample.shape[0]
    lp = seq + CHUNK
    dp = w_pool.shape[1] * w_pool.shape[2]
    g_ssm, n_ssm, p_ssm = s5_b_re.shape[1:]
    gn = g_ssm * n_ssm
    kd = w_gk2.shape[2]
    vd = w_out_o.shape[1]
    row = lambda t: t.reshape(1, -1)
    bf = lambda t: t.astype(BF16)

    j = 0
    w_in_e_b, w_out_e_b = bf(w_in_e[j]), bf(w_out_e[j])
    w_pool_b, w_glu_b = bf(w_pool[j]), bf(w_glu[j])
    w1_b, w3_b, w2_b = bf(ffn_w1[j]), bf(ffn_w3[j]), bf(ffn_w2[j])
    lbr, lbi, bbr, bbi = _s5_prep(s5_a_re[j], s5_a_im[j], s5_log_dt[j], s5_b_re[j], s5_b_im[j])
    s5_args = (lbr, lbi, bf(_block_diag_in(bbr, g_ssm, n_ssm, p_ssm)), bf(_block_diag_in(bbi, g_ssm, n_ssm, p_ssm)),
               bf(_block_diag_out(s5_c_re[j])), bf(_block_diag_out(s5_c_im[j])),
               row(s5_d[j]), w_glu_b, row(b_glu[j]))
    w_in_o_b, w_out_o_b = bf(w_in_o[j]), bf(w_out_o[j])
    rank = w_gk1.shape[2]
    wg1 = bf(jnp.pad(w_gk1[j], ((0, 0), (0, LANES - rank))))
    wg2 = bf(jnp.pad(w_gk2[j], ((0, LANES - rank), (0, 0))))
    w_router_b = bf(jnp.pad(w_router[j].T, ((0, MOE_ROUTE_ROWS - w_router.shape[2]), (0, 0))))
    mw1, mw3, mw2 = bf(moe_w1[j]), bf(moe_w3[j]), bf(moe_w2[j])
    mix_w = (row(norm_mix_e[j]), w_in_e_b)
    pool_w = (w_pool_b, row(pool_scale[j]))

    tail = jnp.concatenate([jnp.zeros((CHUNK - N_META, d), F32), meta_tokens.astype(F32)], axis=0)
    zero_state = jnp.zeros((bp, gn), F32)
    h1, pool_p, re_p, im_p = _mixer0(
        x_prompt, tail, *mix_w, jnp.zeros(((POOL_BUF + 1) * bp, dp), F32), *pool_w,
        zero_state, zero_state, s5_args, w_out_e_b, tt=CHUNK, nb=bp, pos0=-(CHUNK - N_META))
    tm = _first_divisor(bp * lp, (768, 512, 256, 128, 8))
    h2 = _ffn(h1.reshape(bp * lp, d), row(norm_ffn_e[j]), w1_b, w3_b, w2_b, tm)
    q, k, gk, v, gt = _gla_proj(h2, row(norm_mix_o[j]), w_in_o_b, wg1, wg2, row(b_gk[j]), kd, vd, tm)
    seq3 = lambda t: t.reshape(bp, lp, t.shape[1])
    o, gla_p = _gla_chunked(seq3(q), seq3(k), seq3(gk), seq3(v), rot=lp // CHUNK - 1)
    ts = _first_divisor(seq, (1024, 512, 256, 128, 8))
    h3 = _gla_out(o, seq3(gt), seq3(h2), row(gla_norm[j]), w_out_o_b, seq, ts)
    y_prompt = _moe_final(h3, row(norm_ffn_o[j]), w_router_b, mw1, mw3, mw2, row(norm_final), ts)

    pool_init = jnp.pad(state_pool[j].transpose(1, 0, 2), ((1, 0), (0, 0), (0, 0))).reshape((POOL_BUF + 1) * bs, dp)
    h1s, pool_s, re_s, im_s = _mixer0(
        x_sample.reshape(1, bs, d), None, *mix_w, pool_init, *pool_w,
        state_s5_re[j].reshape(bs, gn), state_s5_im[j].reshape(bs, gn), s5_args, w_out_e_b,
        tt=1, nb=bs, pos0=PAST_LEN)
    h2s = _ffn(h1s.reshape(bs, d), row(norm_ffn_e[j]), w1_b, w3_b, w2_b, bs)
    qs, ks, gks, vs, gts = _gla_proj(h2s, row(norm_mix_o[j]), w_in_o_b, wg1, wg2, row(b_gk[j]), kd, vd, bs)
    os_, gla_s = _gla_step(qs, ks, gks, vs, state_gla[j])
    h3s = _gla_out(os_[None], gts[None], h2s[None], row(gla_norm[j]), w_out_o_b, bs, bs)
    y_sample = _moe_final(h3s, row(norm_ffn_o[j]), w_router_b, mw1, mw3, mw2, row(norm_final), bs)

    tb = lambda t, nb: t.reshape(POOL_BUF, nb, dp).transpose(1, 0, 2)[None]
    ssm = lambda t, nb: t.reshape(1, nb, g_ssm, n_ssm)
    return (y_prompt, y_sample.reshape(bs, 1, d), tb(pool_p, bp), tb(pool_s, bs),
            ssm(re_p, bp), ssm(re_s, bs), ssm(im_p, bp), ssm(im_s, bs), gla_p[None], gla_s[None])
```

```python
import functools

import jax
import jax.numpy as jnp
from jax import lax
from jax.experimental import pallas as pl
from jax.experimental.pallas import tpu as pltpu

F32 = jnp.float32
BF16 = jnp.bfloat16

EPS = 1e-6
LOG2_E = 1.4426950408889634
N_META = 16
PAST_LEN = 16384
POOL_WINDOWS = (2, 4, 8, 16)
POOL_BUF = max(POOL_WINDOWS) - 1
GLA_HEADS = 4
GLA_GATE_NORM = 16.0
CHUNK = 64
SUB = 16
GLA_FAST_RANGE = 64.0
GLA_GROUPS = (3, 2, 1)
GLA_UNROLLS = (11, 3, 2, 1)
MXU_DIM = 256
LANES = 128
MOE_BLK = 64
MOE_STRIDE = MXU_DIM // MOE_BLK
MOE_CHUNK = MXU_DIM
MOE_ROUTE_ROWS = 16
S5_COLS = 4 * LANES
ROW_TILES = (1024, 768, 512, 256, 128, 8)
V7X_VMEM_BYTES = 64 * 1024 * 1024
VMEM_LIMIT = V7X_VMEM_BYTES // 8 * 7


def _cp(*sem, vmem=VMEM_LIMIT):
    return pltpu.CompilerParams(dimension_semantics=sem, vmem_limit_bytes=vmem)


def _rms(x, g):
    return x * lax.rsqrt(jnp.mean(x * x, axis=-1, keepdims=True) + EPS) * g


def _dot(a, b):
    return jnp.dot(a, b, preferred_element_type=F32)


def _dot_nt(a, b):
    return lax.dot_general(a, b, (((1,), (1,)), ((), ())), preferred_element_type=F32)


def _dot_tn(a, b):
    return lax.dot_general(a, b, (((0,), (0,)), ((), ())), preferred_element_type=F32)


def _full(shape):
    return pl.BlockSpec(shape, lambda *_: (0,) * len(shape))


def _resident(shape):
    return pl.BlockSpec(shape, lambda *_: (0,) * len(shape), pipeline_mode=pl.Buffered(1))


def _first_divisor(n, candidates):
    return next(c for c in candidates if n % c == 0)


def _s5_prep_kernel(ar_ref, ai_ref, ldt_ref, br_ref, bi_ref, lbr_ref, lbi_ref, bbr_ref, bbi_ref):
    dt = jnp.exp(ldt_ref[...])
    ar = ar_ref[...]
    ai = ai_ref[...]
    mag = jnp.exp(ar * dt)
    lb_re = mag * jnp.cos(ai * dt)
    lb_im = mag * jnp.sin(ai * dt)
    den = ar * ar + ai * ai
    nr = lb_re - 1.0
    f_re = (nr * ar + lb_im * ai) / den
    f_im = (lb_im * ar - nr * ai) / den
    lbr_ref[...] = lb_re
    lbi_ref[...] = lb_im
    br = br_ref[...]
    bi = bi_ref[...]
    bbr_ref[...] = f_re * br - f_im * bi
    bbi_ref[...] = f_re * bi + f_im * br


def _s5_prep(a_re, a_im, log_dt, b_re, b_im):
    g, n, p = b_re.shape
    gn = g * n
    row = lambda t: t.reshape(1, gn)
    to_pgn = lambda t: t.transpose(2, 0, 1).reshape(p, gn)
    ldt = jnp.broadcast_to(log_dt[:, None], (g, n))
    return pl.pallas_call(
        _s5_prep_kernel,
        out_shape=[jax.ShapeDtypeStruct((1, gn), F32)] * 2 + [jax.ShapeDtypeStruct((p, gn), F32)] * 2,
        name="s5_prep",
    )(row(a_re), row(a_im), row(ldt), to_pgn(b_re), to_pgn(b_im))


def _block_diag_in(bb_pgn, g, n, p):
    bb = bb_pgn.reshape(p, g, n).transpose(1, 0, 2)
    eye = jnp.eye(g, dtype=bb.dtype)
    full = (bb[:, :, None, :] * eye[:, None, :, None]).reshape(g * p, g * n)
    nblk = g * p // MXU_DIM
    cols = g * n // nblk
    return jnp.stack([full[k * MXU_DIM:(k + 1) * MXU_DIM, k * cols:(k + 1) * cols] for k in range(nblk)])


def _block_diag_out(c_gpn):
    g, p, n = c_gpn.shape
    eye = jnp.eye(g, dtype=c_gpn.dtype)
    full = (c_gpn.transpose(0, 2, 1)[:, :, None, :] * eye[:, None, :, None]).reshape(g * n, g * p)
    nblk = g * p // MXU_DIM
    rows = g * n // nblk
    return jnp.stack([full[k * rows:(k + 1) * rows, k * MXU_DIM:(k + 1) * MXU_DIM] for k in range(nblk)])


def _pool_block(u_a, i, ext_ref, wp_ref, scale_ref, *, tt, nb, pos0):
    rows = tt * nb
    halo = (POOL_BUF + 1) * nb
    shift = nb.bit_length() - 1
    ext_ref[halo:halo + rows, :] = u_a
    t_in = lax.shift_right_logical(lax.broadcasted_iota(jnp.int32, (rows, LANES), 0), shift)
    pos1 = t_in + (pos0 + 1 + i * tt)
    group = u_a.shape[1] // len(POOL_WINDOWS)
    ys = []
    for gi, w in enumerate(POOL_WINDOWS):
        lo = gi * group
        s = ext_ref[(POOL_BUF + 2 - w) * nb:halo + rows, lo:lo + group]
        k = 1
        while k < w:
            n = s.shape[0]
            s = s[k * nb:, :] + s[:n - k * nb, :]
            k *= 2
        cnt = jnp.clip(pos1, 1, w).astype(F32)
        d = s / cnt - u_a[:, lo:lo + group]
        ys.append(_dot(d.astype(BF16), wp_ref[gi]) * scale_ref[:, lo:lo + group])
    tail = ext_ref[rows:rows + halo, :]
    ext_ref[0:halo, :] = tail
    return jnp.concatenate(ys, axis=1), tail[nb:, :]


def _s5_block(u_b, lbr_ref, lbi_ref, wbr_ref, wbi_ref, wcr_ref, wci_ref, dsk_ref, wg_ref, bg_ref,
              bur_ref, bui_ref, hr_ref, hi_ref, *, tt, nb):
    gn = bur_ref.shape[1]
    nkb = wbr_ref.shape[0]
    sb = gn // nkb
    ub = u_b.astype(BF16)
    for kb in range(nkb):
        uk = ub[:, kb * MXU_DIM:(kb + 1) * MXU_DIM]
        bur_ref[:, kb * sb:(kb + 1) * sb] = _dot(uk, wbr_ref[kb])
        bui_ref[:, kb * sb:(kb + 1) * sb] = _dot(uk, wbi_ref[kb])

    cw = S5_COLS
    for c in range(gn // cw):
        cols = slice(c * cw, (c + 1) * cw)
        lr = jnp.broadcast_to(lbr_ref[:, cols], (nb, cw))
        li = jnp.broadcast_to(lbi_ref[:, cols], (nb, cw))

        h_re, h_im = hr_ref[:, cols], hi_ref[:, cols]
        for t in range(tt):
            rows = slice(t * nb, (t + 1) * nb)
            h_re, h_im = (lr * h_re - li * h_im + bur_ref[rows, cols],
                          lr * h_im + li * h_re + bui_ref[rows, cols])
            bur_ref[rows, cols] = h_re
            bui_ref[rows, cols] = h_im
        hr_ref[:, cols] = h_re
        hi_ref[:, cols] = h_im

    zs = []
    for kb in range(nkb):
        hrb = bur_ref[:, kb * sb:(kb + 1) * sb].astype(BF16)
        hib = bui_ref[:, kb * sb:(kb + 1) * sb].astype(BF16)
        ch = slice(kb * MXU_DIM, (kb + 1) * MXU_DIM)
        y = _dot(hrb, wcr_ref[kb]) - _dot(hib, wci_ref[kb]) + dsk_ref[:, ch] * u_b[:, ch]
        zs.append(jax.nn.gelu(y))
    z = jnp.concatenate(zs, axis=1)
    return z * jax.nn.sigmoid(_dot(z.astype(BF16), wg_ref[...]) + bg_ref[...])


def _mixer0_kernel(*refs, tt, nb, pos0, prompt):
    if prompt:
        x_ref, tail_ref, perm_ref, permt_ref = refs[:4]
        refs = refs[4:]
    else:
        x_ref = refs[0]
        refs = refs[1:]
    (g_ref, win_ref, pinit_ref, wp_ref, scale_ref, h0r_ref, h0i_ref, lbr_ref, lbi_ref, wbr_ref, wbi_ref,
     wcr_ref, wci_ref, dsk_ref, wg_ref, bg_ref, wout_ref,
     h_ref, pst_ref, str_ref, sti_ref, ext_ref, bur_ref, bui_ref, hr_ref, hi_ref) = refs
    i = pl.program_id(0)
    rows = tt * nb
    d = x_ref.shape[-1]
    dp = ext_ref.shape[1]

    @pl.when(i == 0)
    def _():
        ext_ref[0:(POOL_BUF + 1) * nb, :] = pinit_ref[...]
        hr_ref[...] = h0r_ref[...]
        hi_ref[...] = h0i_ref[...]

    x = x_ref[...]
    if prompt:
        x = jnp.where(i == 0, jnp.broadcast_to(tail_ref[...][None], x.shape), x)
    x2 = x.reshape(rows, d)
    xn = _rms(x2, g_ref[...]).astype(BF16)
    if prompt:
        xn = _dot(perm_ref[...], xn).astype(BF16)
    u = _dot(xn, win_ref[...])
    ya, pool_tail = _pool_block(u[:, :dp], i, ext_ref, wp_ref, scale_ref, tt=tt, nb=nb, pos0=pos0)
    yb = _s5_block(u[:, dp:], lbr_ref, lbi_ref, wbr_ref, wbi_ref, wcr_ref, wci_ref, dsk_ref, wg_ref, bg_ref,
                   bur_ref, bui_ref, hr_ref, hi_ref, tt=tt, nb=nb)
    y = jnp.concatenate([ya, yb], axis=1).astype(BF16)
    if prompt:
        y = _dot(permt_ref[...], y).astype(BF16)
    h_ref[...] = x + _dot(y, wout_ref[...]).reshape(x.shape)

    @pl.when(i == pl.num_programs(0) - 1)
    def _():
        pst_ref[...] = pool_tail
        str_ref[...] = hr_ref[...]
        sti_ref[...] = hi_ref[...]


def _mixer0(x, tail, g, w_in, pool_init, wp, scale, h0r, h0i, s5, w_out, *, tt, nb, pos0):
    prompt = tail is not None
    d = x.shape[-1]
    dp = wp.shape[0] * wp.shape[1]
    gn = h0r.shape[1]
    rows = tt * nb
    halo = (POOL_BUF + 1) * nb
    lbr, lbi, wbr, wbi, wcr, wci, dsk, wg, bg = s5
    if prompt:
        nblk = x.shape[1] // tt + 1
        out_rows = x.shape[1] + tt
        r = jnp.arange(rows)
        perm = (r[:, None] % nb * tt + r[:, None] // nb == r[None, :]).astype(BF16)
        lead = [x, tail, perm, perm.T]
        lead_specs = [pl.BlockSpec((nb, tt, d), lambda i: (0, jnp.maximum(i - 1, 0), 0)),
                      _full((tt, d)), _full((rows, rows)), _full((rows, rows))]
        h_spec = pl.BlockSpec((nb, tt, d), lambda i: (0, (i + nblk - 1) % nblk, 0))
        h_shape = (nb, out_rows, d)
    else:
        nblk = 1
        lead = [x]
        lead_specs = [_full(x.shape)]
        h_spec = _full(x.shape)
        h_shape = x.shape
    args = lead + [g, w_in, pool_init, wp, scale, h0r, h0i, lbr, lbi, wbr, wbi, wcr, wci, dsk, wg, bg, w_out]
    in_specs = lead_specs + [_full(a.shape) for a in args[len(lead):]]
    return pl.pallas_call(
        functools.partial(_mixer0_kernel, tt=tt, nb=nb, pos0=pos0, prompt=prompt),
        grid=(nblk,),
        in_specs=in_specs,
        out_specs=[h_spec, _full((POOL_BUF * nb, dp)), _full((nb, gn)), _full((nb, gn))],
        out_shape=[jax.ShapeDtypeStruct(h_shape, F32), jax.ShapeDtypeStruct((POOL_BUF * nb, dp), F32),
                   jax.ShapeDtypeStruct((nb, gn), F32), jax.ShapeDtypeStruct((nb, gn), F32)],
        scratch_shapes=[pltpu.VMEM((halo + rows, dp), F32), pltpu.VMEM((rows, gn), F32),
                        pltpu.VMEM((rows, gn), F32), pltpu.VMEM((nb, gn), F32), pltpu.VMEM((nb, gn), F32)],
        compiler_params=_cp("arbitrary"),
        name="mixer0",
    )(*args)


def _ffn_kernel(h_ref, g_ref, w1_ref, w3_ref, w2_ref, o_ref, *, fc):
    h = h_ref[...]
    xn = _rms(h, g_ref[...]).astype(BF16)
    acc = h
    for c in range(w1_ref.shape[1] // fc):
        cs = slice(c * fc, (c + 1) * fc)
        a = _dot(xn, w1_ref[:, cs])
        b = _dot(xn, w3_ref[:, cs])
        acc = acc + _dot((jax.nn.silu(a) * b).astype(BF16), w2_ref[cs, :])
    o_ref[...] = acc


def _ffn(h, g, w1, w3, w2, tm):
    r, d = h.shape
    f = w1.shape[1]
    fc = MXU_DIM if f % MXU_DIM == 0 else f
    return pl.pallas_call(
        functools.partial(_ffn_kernel, fc=fc),
        grid=(r // tm,),
        in_specs=[pl.BlockSpec((tm, d), lambda i: (i, 0)), _full((1, d)),
                  _resident((d, f)), _resident((d, f)), _resident((f, d))],
        out_specs=pl.BlockSpec((tm, d), lambda i: (i, 0)),
        out_shape=jax.ShapeDtypeStruct((r, d), F32),
        compiler_params=_cp("parallel"),
        name="ffn",
    )(h, g, w1, w3, w2)


def _gla_proj_kernel(h_ref, g_ref, win_ref, wg1_ref, wg2_ref, bgk_ref,
                     q_ref, k_ref, gk_ref, v_ref, gt_ref, *, q_scale):
    xn = _rms(h_ref[...], g_ref[...]).astype(BF16)
    kd = q_ref.shape[1]
    vd = v_ref.shape[1]
    q_ref[...] = _dot(xn, win_ref[:, 0:kd]) * q_scale
    k_ref[...] = _dot(xn, win_ref[:, kd:2 * kd])
    v_ref[...] = _dot(xn, win_ref[:, 2 * kd:2 * kd + vd]).astype(v_ref.dtype)
    gt_ref[...] = _dot(xn, win_ref[:, 2 * kd + vd:]).astype(gt_ref.dtype)
    low = _dot(xn, wg1_ref[...]).astype(BF16)
    z = _dot(low, wg2_ref[...]) + bgk_ref[...]
    log_sig = jnp.minimum(z, 0.0) - jnp.log1p(jnp.exp(-jnp.abs(z)))
    gk_ref[...] = log_sig / GLA_GATE_NORM


def _gla_proj(h, g, w_in, wg1, wg2, bgk, kd, vd, tm):
    r, d = h.shape
    rows = lambda n: pl.BlockSpec((tm, n), lambda i: (i, 0))
    return pl.pallas_call(
        functools.partial(_gla_proj_kernel, q_scale=float((kd // GLA_HEADS) ** -0.5)),
        grid=(r // tm,),
        in_specs=[rows(d), _full((1, d)), _resident(w_in.shape), _full(wg1.shape), _full(wg2.shape),
                  _full((1, kd))],
        out_specs=[rows(kd), rows(kd), rows(kd), rows(vd), rows(vd)],
        out_shape=[jax.ShapeDtypeStruct((r, kd), F32)] * 3 + [jax.ShapeDtypeStruct((r, vd), BF16)] * 2,
        compiler_params=_cp("parallel"),
        name="gla_proj",
    )(h, g, w_in, wg1, wg2, bgk)


def _split3(x):
    a = x.astype(BF16)
    r = x - a.astype(F32)
    b = r.astype(BF16)
    c = (r - b.astype(F32)).astype(BF16)
    return a, b, c


def _gla_chunk_kernel(q_ref, k_ref, gk_ref, v_ref, o_ref, s_ref, st_ref, *, nchunk, rot, unroll, group):
    c_rows = CHUNK
    dk = q_ref.shape[2]
    nsub = c_rows // SUB
    row_i = lax.broadcasted_iota(jnp.int32, (c_rows, c_rows), 0)
    col_i = lax.broadcasted_iota(jnp.int32, (c_rows, c_rows), 1)
    tri = (row_i >= col_i).astype(BF16)
    sub_row = lax.broadcasted_iota(jnp.int32, (SUB, c_rows), 0)
    sub_col = lax.broadcasted_iota(jnp.int32, (SUB, c_rows), 1)
    half_col = lax.broadcasted_iota(jnp.int32, (SUB // 2, c_rows), 1)

    def cum_decay(rs):
        g1, g2, g3 = _split3(gk_ref[0, rs, :])
        return (_dot(tri, g1) + _dot(tri, g2) + _dot(tri, g3)) * LOG2_E

    def intra_fast(spans, bcs):
        rows = group * c_rows
        load = lambda ref: jnp.concatenate([ref[0, rs, :] for rs in spans], axis=0)
        q, k, vb = load(q_ref), load(k_ref), load(v_ref)
        run, pieces = jnp.zeros((1, dk), F32), []
        for bc in bcs:
            pieces.append(bc + run)
            run = run + bc[c_rows - 1:c_rows, :]
        bc = jnp.concatenate(pieces, axis=0)
        q_in = (q * jnp.exp2(bc)).astype(BF16)
        k_up = (k * jnp.exp2(-bc)).astype(BF16)
        causal = (lax.broadcasted_iota(jnp.int32, (rows, rows), 0)
                  >= lax.broadcasted_iota(jnp.int32, (rows, rows), 1))
        att = jnp.where(causal, _dot_nt(q_in, k_up), 0.0)
        o_intra = _dot(att.astype(BF16), vb)
        k_dec = (k * jnp.exp2(run - bc)).astype(BF16)
        return o_intra, q_in, k_dec, jnp.exp2(run), vb

    def intra_safe(rs, bc):
        q = q_ref[0, rs, :]
        k = k_ref[0, rs, :]
        vb = v_ref[0, rs, :]
        blast = bc[c_rows - 1:c_rows, :]
        q_sub, k_hat = [], []
        att_rows = []
        half = SUB // 2
        for i in range(nsub):
            lo = i * SUB
            bs = bc[lo - 1:lo, :] if i > 0 else jnp.zeros((1, dk), F32)
            bc_i = bc[lo:lo + SUB, :]
            q_i = q[lo:lo + SUB, :]
            k_i = k[lo:lo + SUB, :]
            q_sub.append(q_i * jnp.exp2(bc_i - bs))
            if i > 0:
                k_hat.append(jnp.concatenate(
                    [k[:lo, :] * jnp.exp2(bs - bc[:lo, :]), jnp.zeros((c_rows - lo, dk), F32)], axis=0))
            top = jnp.zeros((half, c_rows), F32)
            bot = jnp.zeros((half, c_rows), F32)
            for s in range(SUB):
                r0 = 0 if s < half else half
                e = jnp.exp2(jnp.minimum(bc_i[r0:, :] - bc_i[s:s + 1, :], 0.0))
                col = jnp.sum(q_i[r0:, :] * k_i[s:s + 1, :] * e, axis=-1, keepdims=True)
                if s < half:
                    top = jnp.where(half_col == lo + s, col[:half, :], top)
                bot = jnp.where(half_col == lo + s, col[half - r0:, :], bot)
            diag = jnp.concatenate([top, bot], axis=0)
            att_rows.append(jnp.where(sub_row + lo >= sub_col, diag, 0.0))
        att = jnp.concatenate(att_rows, axis=0)
        zero = jnp.zeros((SUB, dk), F32)
        lhs = jnp.concatenate(
            [jnp.concatenate([q_sub[i] if j == i else zero for j in range(1, nsub)], axis=1)
             for i in range(nsub)], axis=0)
        rhs = jnp.concatenate(k_hat, axis=1)
        att = att + _dot_nt(lhs.astype(BF16), rhs.astype(BF16))
        o_intra = _dot(att.astype(BF16), vb)
        q_in = (q * jnp.exp2(bc)).astype(BF16)
        k_dec = (k * jnp.exp2(blast - bc)).astype(BF16)
        return o_intra, q_in, k_dec, jnp.exp2(blast), vb

    st_ref[...] = jnp.zeros_like(st_ref)
    chunk_rows = lambda c: pl.ds(((c + rot) % nchunk) * c_rows, c_rows)
    blocks = [[chunk_rows(b * group + j) for j in range(group)] for b in range(nchunk // group)]
    all_bcs = [[cum_decay(rs) for rs in spans] for spans in blocks]
    totals = [-functools.reduce(jnp.add, [bc[c_rows - 1:c_rows, :] for bc in bcs]) for bcs in all_bcs]
    mild = jnp.max(functools.reduce(jnp.maximum, totals)) <= GLA_FAST_RANGE

    @pl.when(mild)
    def _():
        parts = [intra_fast(spans, bcs) for spans, bcs in zip(blocks, all_bcs)]
        st = st_ref[...]
        for spans, (o_intra, q_in, k_dec, decay, vb) in zip(blocks, parts):
            o = (o_intra + _dot_nt(q_in, st.astype(BF16))).astype(o_ref.dtype)
            for j, rs in enumerate(spans):
                o_ref[0, rs, :] = o[j * c_rows:(j + 1) * c_rows, :]
            st = st * decay + _dot_tn(vb, k_dec)
        st_ref[...] = st

    @pl.when(jnp.logical_not(mild))
    def _():
        def body(it, carry):
            spans = []
            for j in range(unroll):
                mem = lax.rem(it * unroll + j + rot, nchunk)
                spans.append(pl.ds(pl.multiple_of(mem * c_rows, c_rows), c_rows))
            parts = [intra_safe(rs, cum_decay(rs)) for rs in spans]
            st = st_ref[...]
            for rs, (o_intra, q_in, k_dec, decay, vb) in zip(spans, parts):
                o_ref[0, rs, :] = (o_intra + _dot_nt(q_in, st.astype(BF16))).astype(o_ref.dtype)
                st = st * decay + _dot_tn(vb, k_dec)
            st_ref[...] = st
            return carry

        lax.fori_loop(0, nchunk // unroll, body, 0)

    s_ref[0, 0] = st_ref[...].T


def _gla_chunked(q, k, gk, v, *, rot):
    b, l, kd = q.shape
    vd = v.shape[2]
    dk, dv = kd // GLA_HEADS, vd // GLA_HEADS
    seq = lambda n: pl.BlockSpec((1, l, n), lambda bi, hi: (bi, 0, hi))
    nchunk = l // CHUNK
    unroll = _first_divisor(nchunk, GLA_UNROLLS)
    return pl.pallas_call(
        functools.partial(_gla_chunk_kernel, nchunk=nchunk, rot=rot, unroll=unroll,
                          group=_first_divisor(nchunk, GLA_GROUPS)),
        grid=(b, GLA_HEADS),
        in_specs=[seq(dk), seq(dk), seq(dk), seq(dv)],
        out_specs=[seq(dv), pl.BlockSpec((1, 1, dk, dv), lambda bi, hi: (bi, hi, 0, 0))],
        out_shape=[jax.ShapeDtypeStruct((b, l, vd), BF16),
                   jax.ShapeDtypeStruct((b, GLA_HEADS, dk, dv), F32)],
        scratch_shapes=[pltpu.VMEM((dv, dk), F32)],
        compiler_params=_cp("parallel", "parallel"),
        name="gla_chunked",
    )(q, k, gk, v)


def _gla_step_kernel(qt_ref, kt_ref, gt_ref, v_ref, s0_ref, o_ref, s_ref):
    bt = v_ref.shape[1]
    dec = jnp.exp(gt_ref[0])
    kt = kt_ref[0]
    qt = qt_ref[0]
    vf = v_ref[0].astype(F32)
    for j in range(bt):
        v_row = vf[j:j + 1, :]
        s_new = dec[:, j:j + 1] * s0_ref[j, 0] + kt[:, j:j + 1] * v_row
        s_ref[j, 0] = s_new
        o_ref[0, j:j + 1, :] = jnp.sum(qt[:, j:j + 1] * s_new, axis=0, keepdims=True)


def _gla_step(q, k, gk, v, s0, bt=32):
    b, kd = q.shape
    vd = v.shape[1]
    dk, dv = kd // GLA_HEADS, vd // GLA_HEADS
    nt = b // bt
    cols = lambda t: t.reshape(nt, bt, kd).transpose(0, 2, 1)
    col_spec = pl.BlockSpec((1, dk, bt), lambda ti, hi: (ti, hi, 0))
    v_spec = pl.BlockSpec((1, bt, dv), lambda ti, hi: (ti, 0, hi))
    s_spec = pl.BlockSpec((bt, 1, dk, dv), lambda ti, hi: (ti, hi, 0, 0))
    o, s = pl.pallas_call(
        _gla_step_kernel,
        grid=(nt, GLA_HEADS),
        in_specs=[col_spec, col_spec, col_spec, v_spec, s_spec],
        out_specs=[v_spec, s_spec],
        out_shape=[jax.ShapeDtypeStruct((nt, bt, vd), F32), jax.ShapeDtypeStruct(s0.shape, F32)],
        compiler_params=_cp("parallel", "parallel"),
        name="gla_step",
    )(cols(q), cols(k), cols(gk), v.reshape(nt, bt, vd), s0)
    return o.reshape(b, vd), s


def _gla_out_kernel(o_ref, gt_ref, h_ref, gn_ref, w_ref, out_ref):
    dv = gn_ref.shape[1]
    parts = []
    for hd in range(o_ref.shape[2] // dv):
        sl = slice(hd * dv, (hd + 1) * dv)
        o_h = o_ref[0, :, sl].astype(F32)
        g_h = gt_ref[0, :, sl].astype(F32)
        parts.append((_rms(o_h, gn_ref[...]) * jax.nn.silu(g_h)).astype(BF16))
    out_ref[0] = h_ref[0] + _dot(jnp.concatenate(parts, axis=1), w_ref[...])


def _gla_out(o, gt, h, gnorm, w, n_rows, tm):
    b, _, d = h.shape
    vd = o.shape[2]
    rows = lambda n: pl.BlockSpec((1, tm, n), lambda bi, i: (bi, i, 0))
    return pl.pallas_call(
        _gla_out_kernel,
        grid=(b, n_rows // tm),
        in_specs=[rows(vd), rows(vd), rows(d), _full(gnorm.shape), _resident(w.shape)],
        out_specs=rows(d),
        out_shape=jax.ShapeDtypeStruct((b, n_rows, d), F32),
        compiler_params=_cp("parallel", "parallel"),
        name="gla_out",
    )(o, gt, h, gnorm, w)


def _moe_kernel(h_ref, g_ref, wr_ref, tri_ref, w1_ref, w3_ref, w2_ref, gf_ref, o_ref,
                pt_ref, sl_ref, gg_ref, meta_ref, *, ne):
    e = pl.program_id(2)
    t = h_ref.shape[1]
    nchunk_max = pt_ref.shape[0]

    @pl.when(e == 0)
    def _():
        h = h_ref[0]
        xn = _rms(h, g_ref[...])
        xb = xn.astype(BF16)
        nsub = wr_ref.shape[0]
        logits = _dot_nt(wr_ref[...], xb)
        sub = lax.broadcasted_iota(jnp.int32, logits.shape, 0)
        valid = sub < ne
        logits = jnp.where(valid, logits, -jnp.inf)
        p = jnp.exp(logits - jnp.max(logits, axis=0, keepdims=True))
        p = p / jnp.sum(p, axis=0, keepdims=True)
        p = jnp.where(valid, p, -1.0)
        m1 = jnp.max(p, axis=0, keepdims=True)
        i1 = jnp.min(jnp.where(p == m1, sub, nsub), axis=0, keepdims=True)
        rest = jnp.where(sub == i1, -1.0, p)
        m2 = jnp.max(rest, axis=0, keepdims=True)
        i2 = jnp.min(jnp.where(rest == m2, sub, nsub), axis=0, keepdims=True)
        tot = m1 + m2
        sel = (sub == i1) | (sub == i2)
        gates = jnp.where(sub == i1, m1 / tot, jnp.where(sub == i2, m2 / tot, 0.0))

        incl = _dot(jnp.where(sel, 1.0, 0.0).astype(BF16), tri_ref[...])
        blocks = jnp.floor((incl[:, t - 1:t] + (MOE_BLK - 1)) * (1.0 / MOE_BLK))
        before = (lax.broadcasted_iota(jnp.int32, (nsub, nsub), 1)
                  < lax.broadcasted_iota(jnp.int32, (nsub, nsub), 0)).astype(BF16)
        off = _dot(before, jnp.broadcast_to(blocks, (nsub, LANES)).astype(BF16))[:, 0:1]
        slot = jnp.where(sel, off * MOE_BLK + incl - 1.0, -1.0)
        s1 = jnp.sum(jnp.where(sub == i1, slot, 0.0), axis=0, keepdims=True)
        s2 = jnp.sum(jnp.where(sub == i2, slot, 0.0), axis=0, keepdims=True)
        g_src = jnp.concatenate([*_split3(gates), jnp.zeros((LANES - 3 * nsub, t), BF16)], axis=0)
        per_chunk = MOE_CHUNK // MOE_BLK
        nchunks = jnp.floor((jnp.sum(blocks) + (per_chunk - 1.0)) * (1.0 / per_chunk)).astype(jnp.int32)
        for c in range(nchunk_max):
            @pl.when(c < nchunks)
            def _(c=c):
                ids = (lax.broadcasted_iota(jnp.int32, (MOE_CHUNK, t), 0) + c * MOE_CHUNK).astype(F32)
                pc = (jnp.where(s1 == ids, 1.0, 0.0) + jnp.where(s2 == ids, 1.0, 0.0)).astype(BF16)
                rs = slice(c * MOE_CHUNK, (c + 1) * MOE_CHUNK)
                pt_ref[c] = pc
                sl_ref[rs, :] = _dot(pc, xb).astype(BF16)
                gg_ref[rs, :] = _dot_nt(pc, g_src)
        o_ref[0] = h
        for ee in range(ne):
            meta_ref[0, ee] = blocks[ee, 0].astype(jnp.int32)
            meta_ref[1, ee] = off[ee, 0].astype(jnp.int32)
        meta_ref[2, 0] = nchunks

    def expert_rows(first_blk, nblk):
        rows = nblk * MOE_BLK
        rs = pl.ds(pl.multiple_of(first_blk * MOE_BLK, MOE_BLK), rows)
        xg = sl_ref[rs, :]
        a = _dot(xg, w1_ref[0])
        b = _dot(xg, w3_ref[0])
        y = _dot((jax.nn.silu(a) * b).astype(BF16), w2_ref[0])
        lane_b = lax.broadcasted_iota(jnp.int32, (rows, LANES), 1)
        mine = jnp.bitwise_and(lane_b, wr_ref.shape[0] - 1) == e
        gate = jnp.sum(jnp.where(mine, gg_ref[rs, :], 0.0), axis=-1, keepdims=True)
        sl_ref[rs, :] = (gate * y).astype(BF16)

    nblk = meta_ref[0, e]
    first = meta_ref[1, e]
    rem = lax.rem(nblk, MOE_STRIDE)
    lead = jnp.where(jnp.logical_and(rem > 0, nblk > MOE_STRIDE), rem + MOE_STRIDE, rem)
    for n in list(range(1, MOE_STRIDE)) + list(range(MOE_STRIDE + 1, 2 * MOE_STRIDE)):
        @pl.when(lead == n)
        def _(n=n):
            expert_rows(first, n)

    def stride(js, carry):
        expert_rows(first + lead + MOE_STRIDE * js, MOE_STRIDE)
        return carry

    lax.fori_loop(0, lax.div(nblk - lead, MOE_STRIDE), stride, 0)

    @pl.when(e == ne - 1)
    def _():
        def combine(c, carry):
            rs = pl.ds(pl.multiple_of(c * MOE_CHUNK, MOE_CHUNK), MOE_CHUNK)
            o_ref[0] += _dot_tn(pt_ref[c], sl_ref[rs, :])
            return carry

        lax.fori_loop(0, meta_ref[2, 0], combine, 0)
        o_ref[0] = _rms(o_ref[0], gf_ref[...])


def _moe_final(h, g, w_router, w1, w3, w2, g_final, tm):
    b, l, d = h.shape
    ne, _, f = w1.shape
    slots = -(-(2 * tm + ne * MOE_BLK) // MOE_CHUNK) * MOE_CHUNK
    r = jnp.arange(tm)
    tri = (r[:, None] <= r[None, :]).astype(BF16)
    rows = pl.BlockSpec((1, tm, d), lambda bi, i, e: (bi, i, 0))
    return pl.pallas_call(
        functools.partial(_moe_kernel, ne=ne),
        grid=(b, l // tm, ne),
        in_specs=[rows, _full((1, d)), _full(w_router.shape), _resident((tm, tm)),
                  pl.BlockSpec((1, d, f), lambda bi, i, e: (e, 0, 0)),
                  pl.BlockSpec((1, d, f), lambda bi, i, e: (e, 0, 0)),
                  pl.BlockSpec((1, f, d), lambda bi, i, e: (e, 0, 0)), _full((1, d))],
        out_specs=rows,
        out_shape=jax.ShapeDtypeStruct((b, l, d), F32),
        scratch_shapes=[pltpu.VMEM((slots // MOE_CHUNK, MOE_CHUNK, tm), BF16), pltpu.VMEM((slots, d), BF16),
                        pltpu.VMEM((slots, LANES), F32),
                        pltpu.SMEM((3, ne), jnp.int32)],
        compiler_params=_cp("parallel", "parallel", "arbitrary"),
        name="moe_final",
    )(h, g, w_router, tri, w1, w3, w2, g_final)


def kernel(x_prompt, x_sample, state_pool, state_s5_re, state_s5_im, state_gla, meta_tokens, norm_mix_e, w_in_e, w_pool, pool_scale, s5_a_re, s5_a_im, s5_log_dt, s5_b_re, s5_b_im, s5_c_re, s5_c_im, s5_d, w_glu, b_glu, w_out_e, norm_ffn_e, ffn_w1, ffn_w3, ffn_w2, norm_mix_o, w_in_o, w_gk1, w_gk2, b_gk, gla_norm, w_out_o, norm_ffn_o, w_router, moe_w1, moe_w3, moe_w2, norm_final):
    bp, seq, d = x_prompt.shape
    bs = x_sample.shape[0]
    lp = seq + CHUNK
    dp = w_pool.shape[1] * w_pool.shape[2]
    g_ssm, n_ssm, p_ssm = s5_b_re.shape[1:]
    gn = g_ssm * n_ssm
    kd = w_gk2.shape[2]
    vd = w_out_o.shape[1]
    row = lambda t: t.reshape(1, -1)
    bf = lambda t: t.astype(BF16)

    j = 0
    w_in_e_b, w_out_e_b = bf(w_in_e[j]), bf(w_out_e[j])
    w_pool_b, w_glu_b = bf(w_pool[j]), bf(w_glu[j])
    w1_b, w3_b, w2_b = bf(ffn_w1[j]), bf(ffn_w3[j]), bf(ffn_w2[j])
    lbr, lbi, bbr, bbi = _s5_prep(s5_a_re[j], s5_a_im[j], s5_log_dt[j], s5_b_re[j], s5_b_im[j])
    s5_args = (lbr, lbi, bf(_block_diag_in(bbr, g_ssm, n_ssm, p_ssm)), bf(_block_diag_in(bbi, g_ssm, n_ssm, p_ssm)),
               bf(_block_diag_out(s5_c_re[j])), bf(_block_diag_out(s5_c_im[j])),
               row(s5_d[j]), w_glu_b, row(b_glu[j]))
    w_in_o_b, w_out_o_b = bf(w_in_o[j]), bf(w_out_o[j])
    rank = w_gk1.shape[2]
    wg1 = bf(jnp.pad(w_gk1[j], ((0, 0), (0, LANES - rank))))
    wg2 = bf(jnp.pad(w_gk2[j], ((0, LANES - rank), (0, 0))))
    w_router_b = bf(jnp.pad(w_router[j].T, ((0, MOE_ROUTE_ROWS - w_router.shape[2]), (0, 0))))
    mw1, mw3, mw2 = bf(moe_w1[j]), bf(moe_w3[j]), bf(moe_w2[j])
    mix_w = (row(norm_mix_e[j]), w_in_e_b)
    pool_w = (w_pool_b, row(pool_scale[j]))

    tail = jnp.concatenate([jnp.zeros((CHUNK - N_META, d), F32), meta_tokens.astype(F32)], axis=0)
    zero_state = jnp.zeros((bp, gn), F32)
    h1, pool_p, re_p, im_p = _mixer0(
        x_prompt, tail, *mix_w, jnp.zeros(((POOL_BUF + 1) * bp, dp), F32), *pool_w,
        zero_state, zero_state, s5_args, w_out_e_b, tt=CHUNK, nb=bp, pos0=-(CHUNK - N_META))
    tm = _first_divisor(bp * lp, ROW_TILES)
    h2 = _ffn(h1.reshape(bp * lp, d), row(norm_ffn_e[j]), w1_b, w3_b, w2_b, tm)
    q, k, gk, v, gt = _gla_proj(h2, row(norm_mix_o[j]), w_in_o_b, wg1, wg2, row(b_gk[j]), kd, vd, tm)
    seq3 = lambda t: t.reshape(bp, lp, t.shape[1])
    o, gla_p = _gla_chunked(seq3(q), seq3(k), seq3(gk), seq3(v), rot=lp // CHUNK - 1)
    ts = _first_divisor(seq, ROW_TILES)
    h3 = _gla_out(o, seq3(gt), seq3(h2), row(gla_norm[j]), w_out_o_b, seq, ts)
    y_prompt = _moe_final(h3, row(norm_ffn_o[j]), w_router_b, mw1, mw3, mw2, row(norm_final), ts)

    pool_init = jnp.pad(state_pool[j].transpose(1, 0, 2), ((1, 0), (0, 0), (0, 0))).reshape((POOL_BUF + 1) * bs, dp)
    h1s, pool_s, re_s, im_s = _mixer0(
        x_sample.reshape(1, bs, d), None, *mix_w, pool_init, *pool_w,
        state_s5_re[j].reshape(bs, gn), state_s5_im[j].reshape(bs, gn), s5_args, w_out_e_b,
        tt=1, nb=bs, pos0=PAST_LEN)
    h2s = _ffn(h1s.reshape(bs, d), row(norm_ffn_e[j]), w1_b, w3_b, w2_b, bs)
    qs, ks, gks, vs, gts = _gla_proj(h2s, row(norm_mix_o[j]), w_in_o_b, wg1, wg2, row(b_gk[j]), kd, vd, bs)
    os_, gla_s = _gla_step(qs, ks, gks, vs, state_gla[j])
    h3s = _gla_out(os_[None], gts[None], h2s[None], row(gla_norm[j]), w_out_o_b, bs, bs)
    y_sample = _moe_final(h3s, row(norm_ffn_o[j]), w_router_b, mw1, mw3, mw2, row(norm_final), bs)

    tb = lambda t, nb: t.reshape(POOL_BUF, nb, dp).transpose(1, 0, 2)[None]
    ssm = lambda t, nb: t.reshape(1, nb, g_ssm, n_ssm)
    return (y_prompt, y_sample.reshape(bs, 1, d), tb(pool_p, bp), tb(pool_s, bs),
            ssm(re_p, bp), ssm(re_s, bs), ssm(im_p, bp), ssm(im_s, bs), gla_p[None], gla_s[None])
```

```python
import functools

import jax
import jax.numpy as jnp
from jax import lax
from jax.experimental import pallas as pl
from jax.experimental.pallas import tpu as pltpu

F32 = jnp.float32
BF16 = jnp.bfloat16

EPS = 1e-6
LOG2_E = 1.4426950408889634
N_META = 16
PAST_LEN = 16384
POOL_WINDOWS = (2, 4, 8, 16)
POOL_BUF = max(POOL_WINDOWS) - 1
GLA_HEADS = 4
GLA_GATE_NORM = 16.0
CHUNK = 64
SUB = 16
GLA_FAST_RANGE = 64.0
GLA_GROUPS = (3, 2, 1)
GLA_UNROLLS = (11, 3, 2, 1)
MXU_DIM = 256
LANES = 128
MOE_BLK = 64
MOE_STRIDE = MXU_DIM // MOE_BLK
MOE_CHUNK = MXU_DIM
MOE_ROUTE_ROWS = 16
MOE_PART_ROWS = 512
S5_COLS = 4 * LANES
ROW_TILES = (1024, 768, 512, 256, 128, 8)
V7X_VMEM_BYTES = 64 * 1024 * 1024
VMEM_LIMIT = V7X_VMEM_BYTES // 8 * 7


def _cp(*sem, vmem=VMEM_LIMIT):
    return pltpu.CompilerParams(dimension_semantics=sem, vmem_limit_bytes=vmem)


def _rms(x, g):
    return x * lax.rsqrt(jnp.mean(x * x, axis=-1, keepdims=True) + EPS) * g


def _dot(a, b):
    return jnp.dot(a, b, preferred_element_type=F32)


def _dot_nt(a, b):
    return lax.dot_general(a, b, (((1,), (1,)), ((), ())), preferred_element_type=F32)


def _dot_tn(a, b):
    return lax.dot_general(a, b, (((0,), (0,)), ((), ())), preferred_element_type=F32)


def _full(shape):
    return pl.BlockSpec(shape, lambda *_: (0,) * len(shape))


def _resident(shape):
    return pl.BlockSpec(shape, lambda *_: (0,) * len(shape), pipeline_mode=pl.Buffered(1))


def _first_divisor(n, candidates):
    return next(c for c in candidates if n % c == 0)


def _s5_prep_kernel(ar_ref, ai_ref, ldt_ref, br_ref, bi_ref, lbr_ref, lbi_ref, bbr_ref, bbi_ref):
    dt = jnp.exp(ldt_ref[...])
    ar = ar_ref[...]
    ai = ai_ref[...]
    mag = jnp.exp(ar * dt)
    lb_re = mag * jnp.cos(ai * dt)
    lb_im = mag * jnp.sin(ai * dt)
    den = ar * ar + ai * ai
    nr = lb_re - 1.0
    f_re = (nr * ar + lb_im * ai) / den
    f_im = (lb_im * ar - nr * ai) / den
    lbr_ref[...] = lb_re
    lbi_ref[...] = lb_im
    br = br_ref[...]
    bi = bi_ref[...]
    bbr_ref[...] = f_re * br - f_im * bi
    bbi_ref[...] = f_re * bi + f_im * br


def _s5_prep(a_re, a_im, log_dt, b_re, b_im):
    g, n, p = b_re.shape
    gn = g * n
    row = lambda t: t.reshape(1, gn)
    to_pgn = lambda t: t.transpose(2, 0, 1).reshape(p, gn)
    ldt = jnp.broadcast_to(log_dt[:, None], (g, n))
    return pl.pallas_call(
        _s5_prep_kernel,
        out_shape=[jax.ShapeDtypeStruct((1, gn), F32)] * 2 + [jax.ShapeDtypeStruct((p, gn), F32)] * 2,
        name="s5_prep",
    )(row(a_re), row(a_im), row(ldt), to_pgn(b_re), to_pgn(b_im))


def _block_diag_in(bb_pgn, g, n, p):
    bb = bb_pgn.reshape(p, g, n).transpose(1, 0, 2)
    eye = jnp.eye(g, dtype=bb.dtype)
    full = (bb[:, :, None, :] * eye[:, None, :, None]).reshape(g * p, g * n)
    nblk = g * p // MXU_DIM
    cols = g * n // nblk
    return jnp.stack([full[k * MXU_DIM:(k + 1) * MXU_DIM, k * cols:(k + 1) * cols] for k in range(nblk)])


def _block_diag_out(c_gpn):
    g, p, n = c_gpn.shape
    eye = jnp.eye(g, dtype=c_gpn.dtype)
    full = (c_gpn.transpose(0, 2, 1)[:, :, None, :] * eye[:, None, :, None]).reshape(g * n, g * p)
    nblk = g * p // MXU_DIM
    rows = g * n // nblk
    return jnp.stack([full[k * rows:(k + 1) * rows, k * MXU_DIM:(k + 1) * MXU_DIM] for k in range(nblk)])


def _pool_block(u_a, i, ext_ref, wp_ref, scale_ref, *, tt, nb, pos0):
    rows = tt * nb
    halo = (POOL_BUF + 1) * nb
    shift = nb.bit_length() - 1
    ext_ref[halo:halo + rows, :] = u_a
    t_in = lax.shift_right_logical(lax.broadcasted_iota(jnp.int32, (rows, LANES), 0), shift)
    pos1 = t_in + (pos0 + 1 + i * tt)
    group = u_a.shape[1] // len(POOL_WINDOWS)
    ys = []
    for gi, w in enumerate(POOL_WINDOWS):
        lo = gi * group
        s = ext_ref[(POOL_BUF + 2 - w) * nb:halo + rows, lo:lo + group]
        k = 1
        while k < w:
            n = s.shape[0]
            s = s[k * nb:, :] + s[:n - k * nb, :]
            k *= 2
        cnt = jnp.clip(pos1, 1, w).astype(F32)
        d = s / cnt - u_a[:, lo:lo + group]
        ys.append(_dot(d.astype(BF16), wp_ref[gi]) * scale_ref[:, lo:lo + group])
    tail = ext_ref[rows:rows + halo, :]
    ext_ref[0:halo, :] = tail
    return jnp.concatenate(ys, axis=1), tail[nb:, :]


def _s5_block(u_b, lbr_ref, lbi_ref, wbr_ref, wbi_ref, wcr_ref, wci_ref, dsk_ref, wg_ref, bg_ref,
              bur_ref, bui_ref, hr_ref, hi_ref, *, tt, nb):
    gn = bur_ref.shape[1]
    nkb = wbr_ref.shape[0]
    sb = gn // nkb
    ub = u_b.astype(BF16)
    for kb in range(nkb):
        uk = ub[:, kb * MXU_DIM:(kb + 1) * MXU_DIM]
        bur_ref[:, kb * sb:(kb + 1) * sb] = _dot(uk, wbr_ref[kb])
        bui_ref[:, kb * sb:(kb + 1) * sb] = _dot(uk, wbi_ref[kb])

    cw = S5_COLS
    for c in range(gn // cw):
        cols = slice(c * cw, (c + 1) * cw)
        lr = jnp.broadcast_to(lbr_ref[:, cols], (nb, cw))
        li = jnp.broadcast_to(lbi_ref[:, cols], (nb, cw))

        h_re, h_im = hr_ref[:, cols], hi_ref[:, cols]
        for t in range(tt):
            rows = slice(t * nb, (t + 1) * nb)
            h_re, h_im = (lr * h_re - li * h_im + bur_ref[rows, cols],
                          lr * h_im + li * h_re + bui_ref[rows, cols])
            bur_ref[rows, cols] = h_re
            bui_ref[rows, cols] = h_im
        hr_ref[:, cols] = h_re
        hi_ref[:, cols] = h_im

    zs = []
    for kb in range(nkb):
        hrb = bur_ref[:, kb * sb:(kb + 1) * sb].astype(BF16)
        hib = bui_ref[:, kb * sb:(kb + 1) * sb].astype(BF16)
        ch = slice(kb * MXU_DIM, (kb + 1) * MXU_DIM)
        y = _dot(hrb, wcr_ref[kb]) - _dot(hib, wci_ref[kb]) + dsk_ref[:, ch] * u_b[:, ch]
        zs.append(jax.nn.gelu(y))
    z = jnp.concatenate(zs, axis=1)
    return z * jax.nn.sigmoid(_dot(z.astype(BF16), wg_ref[...]) + bg_ref[...])


def _mixer0_kernel(*refs, tt, nb, pos0, prompt):
    if prompt:
        x_ref, tail_ref, perm_ref, permt_ref = refs[:4]
        refs = refs[4:]
    else:
        x_ref = refs[0]
        refs = refs[1:]
    (g_ref, win_ref, pinit_ref, wp_ref, scale_ref, h0r_ref, h0i_ref, lbr_ref, lbi_ref, wbr_ref, wbi_ref,
     wcr_ref, wci_ref, dsk_ref, wg_ref, bg_ref, wout_ref,
     h_ref, pst_ref, str_ref, sti_ref, ext_ref, bur_ref, bui_ref, hr_ref, hi_ref) = refs
    i = pl.program_id(0)
    rows = tt * nb
    d = x_ref.shape[-1]
    dp = ext_ref.shape[1]

    @pl.when(i == 0)
    def _():
        ext_ref[0:(POOL_BUF + 1) * nb, :] = pinit_ref[...]
        hr_ref[...] = h0r_ref[...]
        hi_ref[...] = h0i_ref[...]

    x = x_ref[...]
    if prompt:
        x = jnp.where(i == 0, jnp.broadcast_to(tail_ref[...][None], x.shape), x)
    x2 = x.reshape(rows, d)
    xn = _rms(x2, g_ref[...]).astype(BF16)
    if prompt:
        xn = _dot(perm_ref[...], xn).astype(BF16)
    u = _dot(xn, win_ref[...])
    ya, pool_tail = _pool_block(u[:, :dp], i, ext_ref, wp_ref, scale_ref, tt=tt, nb=nb, pos0=pos0)
    yb = _s5_block(u[:, dp:], lbr_ref, lbi_ref, wbr_ref, wbi_ref, wcr_ref, wci_ref, dsk_ref, wg_ref, bg_ref,
                   bur_ref, bui_ref, hr_ref, hi_ref, tt=tt, nb=nb)
    y = jnp.concatenate([ya, yb], axis=1).astype(BF16)
    if prompt:
        y = _dot(permt_ref[...], y).astype(BF16)
    h_ref[...] = x + _dot(y, wout_ref[...]).reshape(x.shape)

    @pl.when(i == pl.num_programs(0) - 1)
    def _():
        pst_ref[...] = pool_tail
        str_ref[...] = hr_ref[...]
        sti_ref[...] = hi_ref[...]


def _mixer0(x, tail, g, w_in, pool_init, wp, scale, h0r, h0i, s5, w_out, *, tt, nb, pos0):
    prompt = tail is not None
    d = x.shape[-1]
    dp = wp.shape[0] * wp.shape[1]
    gn = h0r.shape[1]
    rows = tt * nb
    halo = (POOL_BUF + 1) * nb
    lbr, lbi, wbr, wbi, wcr, wci, dsk, wg, bg = s5
    if prompt:
        nblk = x.shape[1] // tt + 1
        out_rows = x.shape[1] + tt
        r = jnp.arange(rows)
        perm = (r[:, None] % nb * tt + r[:, None] // nb == r[None, :]).astype(BF16)
        lead = [x, tail, perm, perm.T]
        lead_specs = [pl.BlockSpec((nb, tt, d), lambda i: (0, jnp.maximum(i - 1, 0), 0)),
                      _full((tt, d)), _full((rows, rows)), _full((rows, rows))]
        h_spec = pl.BlockSpec((nb, tt, d), lambda i: (0, (i + nblk - 1) % nblk, 0))
        h_shape = (nb, out_rows, d)
    else:
        nblk = 1
        lead = [x]
        lead_specs = [_full(x.shape)]
        h_spec = _full(x.shape)
        h_shape = x.shape
    args = lead + [g, w_in, pool_init, wp, scale, h0r, h0i, lbr, lbi, wbr, wbi, wcr, wci, dsk, wg, bg, w_out]
    in_specs = lead_specs + [_full(a.shape) for a in args[len(lead):]]
    return pl.pallas_call(
        functools.partial(_mixer0_kernel, tt=tt, nb=nb, pos0=pos0, prompt=prompt),
        grid=(nblk,),
        in_specs=in_specs,
        out_specs=[h_spec, _full((POOL_BUF * nb, dp)), _full((nb, gn)), _full((nb, gn))],
        out_shape=[jax.ShapeDtypeStruct(h_shape, F32), jax.ShapeDtypeStruct((POOL_BUF * nb, dp), F32),
                   jax.ShapeDtypeStruct((nb, gn), F32), jax.ShapeDtypeStruct((nb, gn), F32)],
        scratch_shapes=[pltpu.VMEM((halo + rows, dp), F32), pltpu.VMEM((rows, gn), F32),
                        pltpu.VMEM((rows, gn), F32), pltpu.VMEM((nb, gn), F32), pltpu.VMEM((nb, gn), F32)],
        compiler_params=_cp("arbitrary"),
        name="mixer0",
    )(*args)


def _ffn_kernel(h_ref, g_ref, w1_ref, w3_ref, w2_ref, o_ref, *, fc):
    h = h_ref[...]
    xn = _rms(h, g_ref[...]).astype(BF16)
    acc = h
    for c in range(w1_ref.shape[1] // fc):
        cs = slice(c * fc, (c + 1) * fc)
        a = _dot(xn, w1_ref[:, cs])
        b = _dot(xn, w3_ref[:, cs])
        acc = acc + _dot((jax.nn.silu(a) * b).astype(BF16), w2_ref[cs, :])
    o_ref[...] = acc


def _ffn(h, g, w1, w3, w2, tm):
    r, d = h.shape
    f = w1.shape[1]
    fc = MXU_DIM if f % MXU_DIM == 0 else f
    return pl.pallas_call(
        functools.partial(_ffn_kernel, fc=fc),
        grid=(r // tm,),
        in_specs=[pl.BlockSpec((tm, d), lambda i: (i, 0)), _full((1, d)),
                  _resident((d, f)), _resident((d, f)), _resident((f, d))],
        out_specs=pl.BlockSpec((tm, d), lambda i: (i, 0)),
        out_shape=jax.ShapeDtypeStruct((r, d), F32),
        compiler_params=_cp("parallel"),
        name="ffn",
    )(h, g, w1, w3, w2)


def _gla_proj_kernel(h_ref, g_ref, win_ref, wg1_ref, wg2_ref, bgk_ref,
                     q_ref, k_ref, gk_ref, v_ref, gt_ref, *, q_scale):
    xn = _rms(h_ref[...], g_ref[...]).astype(BF16)
    kd = q_ref.shape[1]
    vd = v_ref.shape[1]
    q_ref[...] = _dot(xn, win_ref[:, 0:kd]) * q_scale
    k_ref[...] = _dot(xn, win_ref[:, kd:2 * kd])
    v_ref[...] = _dot(xn, win_ref[:, 2 * kd:2 * kd + vd]).astype(v_ref.dtype)
    gt_ref[...] = _dot(xn, win_ref[:, 2 * kd + vd:]).astype(gt_ref.dtype)
    low = _dot(xn, wg1_ref[...]).astype(BF16)
    z = _dot(low, wg2_ref[...]) + bgk_ref[...]
    log_sig = jnp.minimum(z, 0.0) - jnp.log1p(jnp.exp(-jnp.abs(z)))
    gk_ref[...] = log_sig / GLA_GATE_NORM


def _gla_proj(h, g, w_in, wg1, wg2, bgk, kd, vd, tm):
    r, d = h.shape
    rows = lambda n: pl.BlockSpec((tm, n), lambda i: (i, 0))
    return pl.pallas_call(
        functools.partial(_gla_proj_kernel, q_scale=float((kd // GLA_HEADS) ** -0.5)),
        grid=(r // tm,),
        in_specs=[rows(d), _full((1, d)), _resident(w_in.shape), _full(wg1.shape), _full(wg2.shape),
                  _full((1, kd))],
        out_specs=[rows(kd), rows(kd), rows(kd), rows(vd), rows(vd)],
        out_shape=[jax.ShapeDtypeStruct((r, kd), F32)] * 3 + [jax.ShapeDtypeStruct((r, vd), BF16)] * 2,
        compiler_params=_cp("parallel"),
        name="gla_proj",
    )(h, g, w_in, wg1, wg2, bgk)


def _split3(x):
    a = x.astype(BF16)
    r = x - a.astype(F32)
    b = r.astype(BF16)
    c = (r - b.astype(F32)).astype(BF16)
    return a, b, c


def _gla_chunk_kernel(q_ref, k_ref, gk_ref, v_ref, o_ref, s_ref, st_ref, *, nchunk, rot, unroll, group):
    c_rows = CHUNK
    dk = q_ref.shape[2]
    nsub = c_rows // SUB
    row_i = lax.broadcasted_iota(jnp.int32, (c_rows, c_rows), 0)
    col_i = lax.broadcasted_iota(jnp.int32, (c_rows, c_rows), 1)
    tri = (row_i >= col_i).astype(BF16)
    sub_row = lax.broadcasted_iota(jnp.int32, (SUB, c_rows), 0)
    sub_col = lax.broadcasted_iota(jnp.int32, (SUB, c_rows), 1)
    half_col = lax.broadcasted_iota(jnp.int32, (SUB // 2, c_rows), 1)

    def cum_decay(rs):
        g1, g2, g3 = _split3(gk_ref[0, rs, :])
        return (_dot(tri, g1) + _dot(tri, g2) + _dot(tri, g3)) * LOG2_E

    def intra_fast(spans, bcs):
        rows = group * c_rows
        load = lambda ref: jnp.concatenate([ref[0, rs, :] for rs in spans], axis=0)
        q, k, vb = load(q_ref), load(k_ref), load(v_ref)
        run, pieces = jnp.zeros((1, dk), F32), []
        for bc in bcs:
            pieces.append(bc + run)
            run = run + bc[c_rows - 1:c_rows, :]
        bc = jnp.concatenate(pieces, axis=0)
        q_in = (q * jnp.exp2(bc)).astype(BF16)
        k_up = (k * jnp.exp2(-bc)).astype(BF16)
        causal = (lax.broadcasted_iota(jnp.int32, (rows, rows), 0)
                  >= lax.broadcasted_iota(jnp.int32, (rows, rows), 1))
        att = jnp.where(causal, _dot_nt(q_in, k_up), 0.0)
        o_intra = _dot(att.astype(BF16), vb)
        k_dec = (k * jnp.exp2(run - bc)).astype(BF16)
        return o_intra, q_in, k_dec, jnp.exp2(run), vb

    def intra_safe(rs, bc):
        q = q_ref[0, rs, :]
        k = k_ref[0, rs, :]
        vb = v_ref[0, rs, :]
        blast = bc[c_rows - 1:c_rows, :]
        q_sub, k_hat = [], []
        att_rows = []
        half = SUB // 2
        for i in range(nsub):
            lo = i * SUB
            bs = bc[lo - 1:lo, :] if i > 0 else jnp.zeros((1, dk), F32)
            bc_i = bc[lo:lo + SUB, :]
            q_i = q[lo:lo + SUB, :]
            k_i = k[lo:lo + SUB, :]
            q_sub.append(q_i * jnp.exp2(bc_i - bs))
            if i > 0:
                k_hat.append(jnp.concatenate(
                    [k[:lo, :] * jnp.exp2(bs - bc[:lo, :]), jnp.zeros((c_rows - lo, dk), F32)], axis=0))
            top = jnp.zeros((half, c_rows), F32)
            bot = jnp.zeros((half, c_rows), F32)
            for s in range(SUB):
                r0 = 0 if s < half else half
                e = jnp.exp2(jnp.minimum(bc_i[r0:, :] - bc_i[s:s + 1, :], 0.0))
                col = jnp.sum(q_i[r0:, :] * k_i[s:s + 1, :] * e, axis=-1, keepdims=True)
                if s < half:
                    top = jnp.where(half_col == lo + s, col[:half, :], top)
                bot = jnp.where(half_col == lo + s, col[half - r0:, :], bot)
            diag = jnp.concatenate([top, bot], axis=0)
            att_rows.append(jnp.where(sub_row + lo >= sub_col, diag, 0.0))
        att = jnp.concatenate(att_rows, axis=0)
        zero = jnp.zeros((SUB, dk), F32)
        lhs = jnp.concatenate(
            [jnp.concatenate([q_sub[i] if j == i else zero for j in range(1, nsub)], axis=1)
             for i in range(nsub)], axis=0)
        rhs = jnp.concatenate(k_hat, axis=1)
        att = att + _dot_nt(lhs.astype(BF16), rhs.astype(BF16))
        o_intra = _dot(att.astype(BF16), vb)
        q_in = (q * jnp.exp2(bc)).astype(BF16)
        k_dec = (k * jnp.exp2(blast - bc)).astype(BF16)
        return o_intra, q_in, k_dec, jnp.exp2(blast), vb

    st_ref[...] = jnp.zeros_like(st_ref)
    chunk_rows = lambda c: pl.ds(((c + rot) % nchunk) * c_rows, c_rows)
    blocks = [[chunk_rows(b * group + j) for j in range(group)] for b in range(nchunk // group)]
    all_bcs = [[cum_decay(rs) for rs in spans] for spans in blocks]
    totals = [-functools.reduce(jnp.add, [bc[c_rows - 1:c_rows, :] for bc in bcs]) for bcs in all_bcs]
    mild = jnp.max(functools.reduce(jnp.maximum, totals)) <= GLA_FAST_RANGE

    @pl.when(mild)
    def _():
        parts = [intra_fast(spans, bcs) for spans, bcs in zip(blocks, all_bcs)]
        st = st_ref[...]
        for spans, (o_intra, q_in, k_dec, decay, vb) in zip(blocks, parts):
            o = (o_intra + _dot_nt(q_in, st.astype(BF16))).astype(o_ref.dtype)
            for j, rs in enumerate(spans):
                o_ref[0, rs, :] = o[j * c_rows:(j + 1) * c_rows, :]
            st = st * decay + _dot_tn(vb, k_dec)
        st_ref[...] = st

    @pl.when(jnp.logical_not(mild))
    def _():
        def body(it, carry):
            spans = []
            for j in range(unroll):
                mem = lax.rem(it * unroll + j + rot, nchunk)
                spans.append(pl.ds(pl.multiple_of(mem * c_rows, c_rows), c_rows))
            parts = [intra_safe(rs, cum_decay(rs)) for rs in spans]
            st = st_ref[...]
            for rs, (o_intra, q_in, k_dec, decay, vb) in zip(spans, parts):
                o_ref[0, rs, :] = (o_intra + _dot_nt(q_in, st.astype(BF16))).astype(o_ref.dtype)
                st = st * decay + _dot_tn(vb, k_dec)
            st_ref[...] = st
            return carry

        lax.fori_loop(0, nchunk // unroll, body, 0)

    s_ref[0, 0] = st_ref[...].T


def _gla_chunked(q, k, gk, v, *, rot):
    b, l, kd = q.shape
    vd = v.shape[2]
    dk, dv = kd // GLA_HEADS, vd // GLA_HEADS
    seq = lambda n: pl.BlockSpec((1, l, n), lambda bi, hi: (bi, 0, hi))
    nchunk = l // CHUNK
    unroll = _first_divisor(nchunk, GLA_UNROLLS)
    return pl.pallas_call(
        functools.partial(_gla_chunk_kernel, nchunk=nchunk, rot=rot, unroll=unroll,
                          group=_first_divisor(nchunk, GLA_GROUPS)),
        grid=(b, GLA_HEADS),
        in_specs=[seq(dk), seq(dk), seq(dk), seq(dv)],
        out_specs=[seq(dv), pl.BlockSpec((1, 1, dk, dv), lambda bi, hi: (bi, hi, 0, 0))],
        out_shape=[jax.ShapeDtypeStruct((b, l, vd), BF16),
                   jax.ShapeDtypeStruct((b, GLA_HEADS, dk, dv), F32)],
        scratch_shapes=[pltpu.VMEM((dv, dk), F32)],
        compiler_params=_cp("parallel", "parallel"),
        name="gla_chunked",
    )(q, k, gk, v)


def _gla_step_kernel(qt_ref, kt_ref, gt_ref, v_ref, s0_ref, o_ref, s_ref):
    bt = v_ref.shape[1]
    dec = jnp.exp(gt_ref[0])
    kt = kt_ref[0]
    qt = qt_ref[0]
    vf = v_ref[0].astype(F32)
    for j in range(bt):
        v_row = vf[j:j + 1, :]
        s_new = dec[:, j:j + 1] * s0_ref[j, 0] + kt[:, j:j + 1] * v_row
        s_ref[j, 0] = s_new
        o_ref[0, j:j + 1, :] = jnp.sum(qt[:, j:j + 1] * s_new, axis=0, keepdims=True)


def _gla_step(q, k, gk, v, s0, bt=32):
    b, kd = q.shape
    vd = v.shape[1]
    dk, dv = kd // GLA_HEADS, vd // GLA_HEADS
    nt = b // bt
    cols = lambda t: t.reshape(nt, bt, kd).transpose(0, 2, 1)
    col_spec = pl.BlockSpec((1, dk, bt), lambda ti, hi: (ti, hi, 0))
    v_spec = pl.BlockSpec((1, bt, dv), lambda ti, hi: (ti, 0, hi))
    s_spec = pl.BlockSpec((bt, 1, dk, dv), lambda ti, hi: (ti, hi, 0, 0))
    o, s = pl.pallas_call(
        _gla_step_kernel,
        grid=(nt, GLA_HEADS),
        in_specs=[col_spec, col_spec, col_spec, v_spec, s_spec],
        out_specs=[v_spec, s_spec],
        out_shape=[jax.ShapeDtypeStruct((nt, bt, vd), F32), jax.ShapeDtypeStruct(s0.shape, F32)],
        compiler_params=_cp("parallel", "parallel"),
        name="gla_step",
    )(cols(q), cols(k), cols(gk), v.reshape(nt, bt, vd), s0)
    return o.reshape(b, vd), s


def _gla_out_kernel(o_ref, gt_ref, h_ref, gn_ref, w_ref, out_ref):
    dv = gn_ref.shape[1]
    parts = []
    for hd in range(o_ref.shape[2] // dv):
        sl = slice(hd * dv, (hd + 1) * dv)
        o_h = o_ref[0, :, sl].astype(F32)
        g_h = gt_ref[0, :, sl].astype(F32)
        parts.append((_rms(o_h, gn_ref[...]) * jax.nn.silu(g_h)).astype(BF16))
    out_ref[0] = h_ref[0] + _dot(jnp.concatenate(parts, axis=1), w_ref[...])


def _gla_out(o, gt, h, gnorm, w, n_rows, tm):
    b, _, d = h.shape
    vd = o.shape[2]
    rows = lambda n: pl.BlockSpec((1, tm, n), lambda bi, i: (bi, i, 0))
    return pl.pallas_call(
        _gla_out_kernel,
        grid=(b, n_rows // tm),
        in_specs=[rows(vd), rows(vd), rows(d), _full(gnorm.shape), _resident(w.shape)],
        out_specs=rows(d),
        out_shape=jax.ShapeDtypeStruct((b, n_rows, d), F32),
        compiler_params=_cp("parallel", "parallel"),
        name="gla_out",
    )(o, gt, h, gnorm, w)


def _moe_kernel(h_ref, g_ref, wr_ref, tri_ref, w1_ref, w3_ref, w2_ref, gf_ref, o_ref,
                pt_ref, sl_ref, gg_ref, meta_ref, *, ne):
    e = pl.program_id(2)
    parts, nchunk_max, _, t = pt_ref.shape

    def route(part, h):
        xn = _rms(h, g_ref[...])
        xb = xn.astype(BF16)
        nsub = wr_ref.shape[0]
        logits = _dot_nt(wr_ref[...], xb)
        sub = lax.broadcasted_iota(jnp.int32, logits.shape, 0)
        valid = sub < ne
        logits = jnp.where(valid, logits, -jnp.inf)
        p = jnp.exp(logits - jnp.max(logits, axis=0, keepdims=True))
        p = p / jnp.sum(p, axis=0, keepdims=True)
        p = jnp.where(valid, p, -1.0)
        m1 = jnp.max(p, axis=0, keepdims=True)
        i1 = jnp.min(jnp.where(p == m1, sub, nsub), axis=0, keepdims=True)
        rest = jnp.where(sub == i1, -1.0, p)
        m2 = jnp.max(rest, axis=0, keepdims=True)
        i2 = jnp.min(jnp.where(rest == m2, sub, nsub), axis=0, keepdims=True)
        tot = m1 + m2
        sel = (sub == i1) | (sub == i2)
        gates = jnp.where(sub == i1, m1 / tot, jnp.where(sub == i2, m2 / tot, 0.0))

        incl = _dot(jnp.where(sel, 1.0, 0.0).astype(BF16), tri_ref[...])
        blocks = jnp.floor((incl[:, t - 1:t] + (MOE_BLK - 1)) * (1.0 / MOE_BLK))
        before = (lax.broadcasted_iota(jnp.int32, (nsub, nsub), 1)
                  < lax.broadcasted_iota(jnp.int32, (nsub, nsub), 0)).astype(BF16)
        off = _dot(before, jnp.broadcast_to(blocks, (nsub, LANES)).astype(BF16))[:, 0:1]
        slot = jnp.where(sel, off * MOE_BLK + incl - 1.0, -1.0)
        s1 = jnp.sum(jnp.where(sub == i1, slot, 0.0), axis=0, keepdims=True)
        s2 = jnp.sum(jnp.where(sub == i2, slot, 0.0), axis=0, keepdims=True)
        g_src = jnp.concatenate([*_split3(gates), jnp.zeros((LANES - 3 * nsub, t), BF16)], axis=0)
        per_chunk = MOE_CHUNK // MOE_BLK
        nchunks = jnp.floor((jnp.sum(blocks) + (per_chunk - 1.0)) * (1.0 / per_chunk)).astype(jnp.int32)
        for c in range(nchunk_max):
            @pl.when(c < nchunks)
            def _(c=c):
                ids = (lax.broadcasted_iota(jnp.int32, (MOE_CHUNK, t), 0) + c * MOE_CHUNK).astype(F32)
                pc = (jnp.where(s1 == ids, 1.0, 0.0) + jnp.where(s2 == ids, 1.0, 0.0)).astype(BF16)
                rs = slice(c * MOE_CHUNK, (c + 1) * MOE_CHUNK)
                pt_ref[part, c] = pc
                sl_ref[part, rs, :] = _dot(pc, xb).astype(BF16)
                gg_ref[part, rs, :] = _dot_nt(pc, g_src)
        for ee in range(ne):
            meta_ref[3 * part, ee] = blocks[ee, 0].astype(jnp.int32)
            meta_ref[3 * part + 1, ee] = off[ee, 0].astype(jnp.int32)
        meta_ref[3 * part + 2, 0] = nchunks

    @pl.when(e == 0)
    def _():
        for part in range(parts):
            route(part, h_ref[0, part * t:(part + 1) * t, :])
        o_ref[0] = h_ref[0]

    def expert_rows(part, first_blk, nblk):
        rows = nblk * MOE_BLK
        rs = pl.ds(pl.multiple_of(first_blk * MOE_BLK, MOE_BLK), rows)
        xg = sl_ref[part, rs, :]
        a = _dot(xg, w1_ref[0])
        b = _dot(xg, w3_ref[0])
        y = _dot((jax.nn.silu(a) * b).astype(BF16), w2_ref[0])
        lane_b = lax.broadcasted_iota(jnp.int32, (rows, LANES), 1)
        mine = jnp.bitwise_and(lane_b, wr_ref.shape[0] - 1) == e
        gate = jnp.sum(jnp.where(mine, gg_ref[part, rs, :], 0.0), axis=-1, keepdims=True)
        sl_ref[part, rs, :] = (gate * y).astype(BF16)

    def expert_group(part, carry):
        nblk = meta_ref[3 * part, e]
        first = meta_ref[3 * part + 1, e]
        rem = lax.rem(nblk, MOE_STRIDE)
        lead = jnp.where(jnp.logical_and(rem > 0, nblk > MOE_STRIDE), rem + MOE_STRIDE, rem)
        for n in list(range(1, MOE_STRIDE)) + list(range(MOE_STRIDE + 1, 2 * MOE_STRIDE)):
            @pl.when(lead == n)
            def _(n=n):
                expert_rows(part, first, n)

        def stride(js, carry):
            expert_rows(part, first + lead + MOE_STRIDE * js, MOE_STRIDE)
            return carry

        lax.fori_loop(0, lax.div(nblk - lead, MOE_STRIDE), stride, 0)
        return carry

    lax.fori_loop(0, parts, expert_group, 0)

    @pl.when(e == ne - 1)
    def _():
        def combine_part(part, carry):
            rows = pl.ds(pl.multiple_of(part * t, t), t)

            def combine(c, carry):
                rs = pl.ds(pl.multiple_of(c * MOE_CHUNK, MOE_CHUNK), MOE_CHUNK)
                o_ref[0, rows, :] += _dot_tn(pt_ref[part, c], sl_ref[part, rs, :])
                return carry

            lax.fori_loop(0, meta_ref[3 * part + 2, 0], combine, 0)
            return carry

        lax.fori_loop(0, parts, combine_part, 0)
        o_ref[0] = _rms(o_ref[0], gf_ref[...])


def _moe_final(h, g, w_router, w1, w3, w2, g_final, tm):
    b, l, d = h.shape
    ne, _, f = w1.shape
    tp = MOE_PART_ROWS if tm % MOE_PART_ROWS == 0 else tm
    parts = tm // tp
    slots = -(-(2 * tp + ne * MOE_BLK) // MOE_CHUNK) * MOE_CHUNK
    r = jnp.arange(tp)
    tri = (r[:, None] <= r[None, :]).astype(BF16)
    rows = pl.BlockSpec((1, tm, d), lambda bi, i, e: (bi, i, 0))
    return pl.pallas_call(
        functools.partial(_moe_kernel, ne=ne),
        grid=(b, l // tm, ne),
        in_specs=[rows, _full((1, d)), _full(w_router.shape), _resident((tp, tp)),
                  pl.BlockSpec((1, d, f), lambda bi, i, e: (e, 0, 0)),
                  pl.BlockSpec((1, d, f), lambda bi, i, e: (e, 0, 0)),
                  pl.BlockSpec((1, f, d), lambda bi, i, e: (e, 0, 0)), _full((1, d))],
        out_specs=rows,
        out_shape=jax.ShapeDtypeStruct((b, l, d), F32),
        scratch_shapes=[pltpu.VMEM((parts, slots // MOE_CHUNK, MOE_CHUNK, tp), BF16),
                        pltpu.VMEM((parts, slots, d), BF16), pltpu.VMEM((parts, slots, LANES), F32),
                        pltpu.SMEM((3 * parts, ne), jnp.int32)],
        compiler_params=_cp("parallel", "parallel", "arbitrary"),
        name="moe_final",
    )(h, g, w_router, tri, w1, w3, w2, g_final)


def kernel(x_prompt, x_sample, state_pool, state_s5_re, state_s5_im, state_gla, meta_tokens, norm_mix_e, w_in_e, w_pool, pool_scale, s5_a_re, s5_a_im, s5_log_dt, s5_b_re, s5_b_im, s5_c_re, s5_c_im, s5_d, w_glu, b_glu, w_out_e, norm_ffn_e, ffn_w1, ffn_w3, ffn_w2, norm_mix_o, w_in_o, w_gk1, w_gk2, b_gk, gla_norm, w_out_o, norm_ffn_o, w_router, moe_w1, moe_w3, moe_w2, norm_final):
    bp, seq, d = x_prompt.shape
    bs = x_sample.shape[0]
    lp = seq + CHUNK
    dp = w_pool.shape[1] * w_pool.shape[2]
    g_ssm, n_ssm, p_ssm = s5_b_re.shape[1:]
    gn = g_ssm * n_ssm
    kd = w_gk2.shape[2]
    vd = w_out_o.shape[1]
    row = lambda t: t.reshape(1, -1)
    bf = lambda t: t.astype(BF16)

    j = 0
    w_in_e_b, w_out_e_b = bf(w_in_e[j]), bf(w_out_e[j])
    w_pool_b, w_glu_b = bf(w_pool[j]), bf(w_glu[j])
    w1_b, w3_b, w2_b = bf(ffn_w1[j]), bf(ffn_w3[j]), bf(ffn_w2[j])
    lbr, lbi, bbr, bbi = _s5_prep(s5_a_re[j], s5_a_im[j], s5_log_dt[j], s5_b_re[j], s5_b_im[j])
    s5_args = (lbr, lbi, bf(_block_diag_in(bbr, g_ssm, n_ssm, p_ssm)), bf(_block_diag_in(bbi, g_ssm, n_ssm, p_ssm)),
               bf(_block_diag_out(s5_c_re[j])), bf(_block_diag_out(s5_c_im[j])),
               row(s5_d[j]), w_glu_b, row(b_glu[j]))
    w_in_o_b, w_out_o_b = bf(w_in_o[j]), bf(w_out_o[j])
    rank = w_gk1.shape[2]
    wg1 = bf(jnp.pad(w_gk1[j], ((0, 0), (0, LANES - rank))))
    wg2 = bf(jnp.pad(w_gk2[j], ((0, LANES - rank), (0, 0))))
    w_router_b = bf(jnp.pad(w_router[j].T, ((0, MOE_ROUTE_ROWS - w_router.shape[2]), (0, 0))))
    mw1, mw3, mw2 = bf(moe_w1[j]), bf(moe_w3[j]), bf(moe_w2[j])
    mix_w = (row(norm_mix_e[j]), w_in_e_b)
    pool_w = (w_pool_b, row(pool_scale[j]))

    tail = jnp.concatenate([jnp.zeros((CHUNK - N_META, d), F32), meta_tokens.astype(F32)], axis=0)
    zero_state = jnp.zeros((bp, gn), F32)
    h1, pool_p, re_p, im_p = _mixer0(
        x_prompt, tail, *mix_w, jnp.zeros(((POOL_BUF + 1) * bp, dp), F32), *pool_w,
        zero_state, zero_state, s5_args, w_out_e_b, tt=CHUNK, nb=bp, pos0=-(CHUNK - N_META))
    tm = _first_divisor(bp * lp, ROW_TILES)
    h2 = _ffn(h1.reshape(bp * lp, d), row(norm_ffn_e[j]), w1_b, w3_b, w2_b, tm)
    q, k, gk, v, gt = _gla_proj(h2, row(norm_mix_o[j]), w_in_o_b, wg1, wg2, row(b_gk[j]), kd, vd, tm)
    seq3 = lambda t: t.reshape(bp, lp, t.shape[1])
    o, gla_p = _gla_chunked(seq3(q), seq3(k), seq3(gk), seq3(v), rot=lp // CHUNK - 1)
    ts = _first_divisor(seq, ROW_TILES)
    h3 = _gla_out(o, seq3(gt), seq3(h2), row(gla_norm[j]), w_out_o_b, seq, ts)
    y_prompt = _moe_final(h3, row(norm_ffn_o[j]), w_router_b, mw1, mw3, mw2, row(norm_final), ts)

    pool_init = jnp.pad(state_pool[j].transpose(1, 0, 2), ((1, 0), (0, 0), (0, 0))).reshape((POOL_BUF + 1) * bs, dp)
    h1s, pool_s, re_s, im_s = _mixer0(
        x_sample.reshape(1, bs, d), None, *mix_w, pool_init, *pool_w,
        state_s5_re[j].reshape(bs, gn), state_s5_im[j].reshape(bs, gn), s5_args, w_out_e_b,
        tt=1, nb=bs, pos0=PAST_LEN)
    h2s = _ffn(h1s.reshape(bs, d), row(norm_ffn_e[j]), w1_b, w3_b, w2_b, bs)
    qs, ks, gks, vs, gts = _gla_proj(h2s, row(norm_mix_o[j]), w_in_o_b, wg1, wg2, row(b_gk[j]), kd, vd, bs)
    os_, gla_s = _gla_step(qs, ks, gks, vs, state_gla[j])
    h3s = _gla_out(os_[None], gts[None], h2s[None], row(gla_norm[j]), w_out_o_b, bs, bs)
    y_sample = _moe_final(h3s, row(norm_ffn_o[j]), w_router_b, mw1, mw3, mw2, row(norm_final), bs)

    tb = lambda t, nb: t.reshape(POOL_BUF, nb, dp).transpose(1, 0, 2)[None]
    ssm = lambda t, nb: t.reshape(1, nb, g_ssm, n_ssm)
    return (y_prompt, y_sample.reshape(bs, 1, d), tb(pool_p, bp), tb(pool_s, bs),
            ssm(re_p, bp), ssm(re_s, bs), ssm(im_p, bp), ssm(im_s, bs), gla_p[None], gla_s[None])
```

```python
import functools

import jax
import jax.numpy as jnp
from jax import lax
from jax.experimental import pallas as pl
from jax.experimental.pallas import tpu as pltpu

F32 = jnp.float32
BF16 = jnp.bfloat16

EPS = 1e-6
LOG2_E = 1.4426950408889634
N_META = 16
PAST_LEN = 16384
POOL_WINDOWS = (2, 4, 8, 16)
POOL_BUF = max(POOL_WINDOWS) - 1
GLA_HEADS = 4
GLA_GATE_NORM = 16.0
CHUNK = 64
SUB = 16
GLA_FAST_RANGE = 64.0
GLA_GROUPS = (3, 2, 1)
GLA_UNROLLS = (11, 3, 2, 1)
MXU_DIM = 256
LANES = 128
MOE_BLK = 64
MOE_STRIDE = MXU_DIM // MOE_BLK
MOE_CHUNK = MXU_DIM
MOE_ROUTE_ROWS = 16
S5_COLS = 4 * LANES
ROW_TILES = (1024, 768, 512, 256, 128, 8)
V7X_VMEM_BYTES = 64 * 1024 * 1024
VMEM_LIMIT = V7X_VMEM_BYTES // 8 * 7


def _cp(*sem, vmem=VMEM_LIMIT):
    return pltpu.CompilerParams(dimension_semantics=sem, vmem_limit_bytes=vmem)


def _rms(x, g):
    return x * lax.rsqrt(jnp.mean(x * x, axis=-1, keepdims=True) + EPS) * g


def _dot(a, b):
    return jnp.dot(a, b, preferred_element_type=F32)


def _dot_nt(a, b):
    return lax.dot_general(a, b, (((1,), (1,)), ((), ())), preferred_element_type=F32)


def _dot_tn(a, b):
    return lax.dot_general(a, b, (((0,), (0,)), ((), ())), preferred_element_type=F32)


def _full(shape):
    return pl.BlockSpec(shape, lambda *_: (0,) * len(shape))


def _resident(shape):
    return pl.BlockSpec(shape, lambda *_: (0,) * len(shape), pipeline_mode=pl.Buffered(1))


def _first_divisor(n, candidates):
    return next(c for c in candidates if n % c == 0)


def _s5_prep_kernel(ar_ref, ai_ref, ldt_ref, br_ref, bi_ref, lbr_ref, lbi_ref, bbr_ref, bbi_ref):
    dt = jnp.exp(ldt_ref[...])
    ar = ar_ref[...]
    ai = ai_ref[...]
    mag = jnp.exp(ar * dt)
    lb_re = mag * jnp.cos(ai * dt)
    lb_im = mag * jnp.sin(ai * dt)
    den = ar * ar + ai * ai
    nr = lb_re - 1.0
    f_re = (nr * ar + lb_im * ai) / den
    f_im = (lb_im * ar - nr * ai) / den
    lbr_ref[...] = lb_re
    lbi_ref[...] = lb_im
    br = br_ref[...]
    bi = bi_ref[...]
    bbr_ref[...] = f_re * br - f_im * bi
    bbi_ref[...] = f_re * bi + f_im * br


def _s5_prep(a_re, a_im, log_dt, b_re, b_im):
    g, n, p = b_re.shape
    gn = g * n
    row = lambda t: t.reshape(1, gn)
    to_pgn = lambda t: t.transpose(2, 0, 1).reshape(p, gn)
    ldt = jnp.broadcast_to(log_dt[:, None], (g, n))
    return pl.pallas_call(
        _s5_prep_kernel,
        out_shape=[jax.ShapeDtypeStruct((1, gn), F32)] * 2 + [jax.ShapeDtypeStruct((p, gn), F32)] * 2,
        name="s5_prep",
    )(row(a_re), row(a_im), row(ldt), to_pgn(b_re), to_pgn(b_im))


def _block_diag_in(bb_pgn, g, n, p):
    bb = bb_pgn.reshape(p, g, n).transpose(1, 0, 2)
    eye = jnp.eye(g, dtype=bb.dtype)
    full = (bb[:, :, None, :] * eye[:, None, :, None]).reshape(g * p, g * n)
    nblk = g * p // MXU_DIM
    cols = g * n // nblk
    return jnp.stack([full[k * MXU_DIM:(k + 1) * MXU_DIM, k * cols:(k + 1) * cols] for k in range(nblk)])


def _block_diag_out(c_gpn):
    g, p, n = c_gpn.shape
    eye = jnp.eye(g, dtype=c_gpn.dtype)
    full = (c_gpn.transpose(0, 2, 1)[:, :, None, :] * eye[:, None, :, None]).reshape(g * n, g * p)
    nblk = g * p // MXU_DIM
    rows = g * n // nblk
    return jnp.stack([full[k * rows:(k + 1) * rows, k * MXU_DIM:(k + 1) * MXU_DIM] for k in range(nblk)])


def _pool_block(u_a, i, ext_ref, wp_ref, scale_ref, *, tt, nb, pos0):
    rows = tt * nb
    halo = (POOL_BUF + 1) * nb
    shift = nb.bit_length() - 1
    ext_ref[halo:halo + rows, :] = u_a
    t_in = lax.shift_right_logical(lax.broadcasted_iota(jnp.int32, (rows, LANES), 0), shift)
    pos1 = t_in + (pos0 + 1 + i * tt)
    group = u_a.shape[1] // len(POOL_WINDOWS)
    ys = []
    for gi, w in enumerate(POOL_WINDOWS):
        lo = gi * group
        s = ext_ref[(POOL_BUF + 2 - w) * nb:halo + rows, lo:lo + group]
        k = 1
        while k < w:
            n = s.shape[0]
            s = s[k * nb:, :] + s[:n - k * nb, :]
            k *= 2
        cnt = jnp.clip(pos1, 1, w).astype(F32)
        d = s / cnt - u_a[:, lo:lo + group]
        ys.append(_dot(d.astype(BF16), wp_ref[gi]) * scale_ref[:, lo:lo + group])
    tail = ext_ref[rows:rows + halo, :]
    ext_ref[0:halo, :] = tail
    return jnp.concatenate(ys, axis=1), tail[nb:, :]


def _s5_block(u_b, lbr_ref, lbi_ref, wbr_ref, wbi_ref, wcr_ref, wci_ref, dsk_ref, wg_ref, bg_ref,
              bur_ref, bui_ref, hr_ref, hi_ref, *, tt, nb):
    gn = bur_ref.shape[1]
    nkb = wbr_ref.shape[0]
    sb = gn // nkb
    ub = u_b.astype(BF16)
    for kb in range(nkb):
        uk = ub[:, kb * MXU_DIM:(kb + 1) * MXU_DIM]
        bur_ref[:, kb * sb:(kb + 1) * sb] = _dot(uk, wbr_ref[kb])
        bui_ref[:, kb * sb:(kb + 1) * sb] = _dot(uk, wbi_ref[kb])

    cw = S5_COLS
    for c in range(gn // cw):
        cols = slice(c * cw, (c + 1) * cw)
        lr = jnp.broadcast_to(lbr_ref[:, cols], (nb, cw))
        li = jnp.broadcast_to(lbi_ref[:, cols], (nb, cw))

        h_re, h_im = hr_ref[:, cols], hi_ref[:, cols]
        for t in range(tt):
            rows = slice(t * nb, (t + 1) * nb)
            h_re, h_im = (lr * h_re - li * h_im + bur_ref[rows, cols],
                          lr * h_im + li * h_re + bui_ref[rows, cols])
            bur_ref[rows, cols] = h_re
            bui_ref[rows, cols] = h_im
        hr_ref[:, cols] = h_re
        hi_ref[:, cols] = h_im

    zs = []
    for kb in range(nkb):
        hrb = bur_ref[:, kb * sb:(kb + 1) * sb].astype(BF16)
        hib = bui_ref[:, kb * sb:(kb + 1) * sb].astype(BF16)
        ch = slice(kb * MXU_DIM, (kb + 1) * MXU_DIM)
        y = _dot(hrb, wcr_ref[kb]) - _dot(hib, wci_ref[kb]) + dsk_ref[:, ch] * u_b[:, ch]
        zs.append(jax.nn.gelu(y))
    z = jnp.concatenate(zs, axis=1)
    return z * jax.nn.sigmoid(_dot(z.astype(BF16), wg_ref[...]) + bg_ref[...])


def _mixer0_kernel(*refs, tt, nb, pos0, prompt):
    if prompt:
        x_ref, tail_ref, perm_ref, permt_ref = refs[:4]
        refs = refs[4:]
    else:
        x_ref = refs[0]
        refs = refs[1:]
    (g_ref, wif_ref, pinit_ref, wp_ref, scale_ref, h0r_ref, h0i_ref, lbr_ref, lbi_ref, wbr_ref, wbi_ref,
     wcr_ref, wci_ref, dsk_ref, wg_ref, bg_ref, wof_ref,
     h_ref, pst_ref, str_ref, sti_ref, ext_ref, bur_ref, bui_ref, hr_ref, hi_ref, win_ref, wout_ref) = refs
    i = pl.program_id(0)
    rows = tt * nb
    d = x_ref.shape[-1]
    dp = ext_ref.shape[1]

    @pl.when(i == 0)
    def _():
        ext_ref[0:(POOL_BUF + 1) * nb, :] = pinit_ref[...]
        hr_ref[...] = h0r_ref[...]
        hi_ref[...] = h0i_ref[...]
        win_ref[...] = wif_ref[...].astype(BF16)
        wout_ref[...] = wof_ref[...].astype(BF16)

    x = x_ref[...]
    if prompt:
        x = jnp.where(i == 0, jnp.broadcast_to(tail_ref[...][None], x.shape), x)
    x2 = x.reshape(rows, d)
    xn = _rms(x2, g_ref[...]).astype(BF16)
    if prompt:
        xn = _dot(perm_ref[...], xn).astype(BF16)
    u = _dot(xn, win_ref[...])
    ya, pool_tail = _pool_block(u[:, :dp], i, ext_ref, wp_ref, scale_ref, tt=tt, nb=nb, pos0=pos0)
    yb = _s5_block(u[:, dp:], lbr_ref, lbi_ref, wbr_ref, wbi_ref, wcr_ref, wci_ref, dsk_ref, wg_ref, bg_ref,
                   bur_ref, bui_ref, hr_ref, hi_ref, tt=tt, nb=nb)
    y = jnp.concatenate([ya, yb], axis=1).astype(BF16)
    if prompt:
        y = _dot(permt_ref[...], y).astype(BF16)
    h_ref[...] = x + _dot(y, wout_ref[...]).reshape(x.shape)

    @pl.when(i == pl.num_programs(0) - 1)
    def _():
        pst_ref[...] = pool_tail
        str_ref[...] = hr_ref[...]
        sti_ref[...] = hi_ref[...]


def _mixer0(x, tail, g, w_in, pool_init, wp, scale, h0r, h0i, s5, w_out, *, tt, nb, pos0):
    prompt = tail is not None
    d = x.shape[-1]
    dp = wp.shape[0] * wp.shape[1]
    gn = h0r.shape[1]
    rows = tt * nb
    halo = (POOL_BUF + 1) * nb
    lbr, lbi, wbr, wbi, wcr, wci, dsk, wg, bg = s5
    if prompt:
        nblk = x.shape[1] // tt + 1
        out_rows = x.shape[1] + tt
        r = jnp.arange(rows)
        perm = (r[:, None] % nb * tt + r[:, None] // nb == r[None, :]).astype(BF16)
        lead = [x, tail, perm, perm.T]
        lead_specs = [pl.BlockSpec((nb, tt, d), lambda i: (0, jnp.maximum(i - 1, 0), 0)),
                      _full((tt, d)), _full((rows, rows)), _full((rows, rows))]
        h_spec = pl.BlockSpec((nb, tt, d), lambda i: (0, (i + nblk - 1) % nblk, 0))
        h_shape = (nb, out_rows, d)
    else:
        nblk = 1
        lead = [x]
        lead_specs = [_full(x.shape)]
        h_spec = _full(x.shape)
        h_shape = x.shape
    args = lead + [g, w_in, pool_init, wp, scale, h0r, h0i, lbr, lbi, wbr, wbi, wcr, wci, dsk, wg, bg, w_out]
    in_specs = lead_specs + [_full(a.shape) for a in args[len(lead):]]
    return pl.pallas_call(
        functools.partial(_mixer0_kernel, tt=tt, nb=nb, pos0=pos0, prompt=prompt),
        grid=(nblk,),
        in_specs=in_specs,
        out_specs=[h_spec, _full((POOL_BUF * nb, dp)), _full((nb, gn)), _full((nb, gn))],
        out_shape=[jax.ShapeDtypeStruct(h_shape, F32), jax.ShapeDtypeStruct((POOL_BUF * nb, dp), F32),
                   jax.ShapeDtypeStruct((nb, gn), F32), jax.ShapeDtypeStruct((nb, gn), F32)],
        scratch_shapes=[pltpu.VMEM((halo + rows, dp), F32), pltpu.VMEM((rows, gn), F32),
                        pltpu.VMEM((rows, gn), F32), pltpu.VMEM((nb, gn), F32), pltpu.VMEM((nb, gn), F32),
                        pltpu.VMEM(w_in.shape, BF16), pltpu.VMEM(w_out.shape, BF16)],
        compiler_params=_cp("arbitrary"),
        name="mixer0",
    )(*args)


def _ffn_kernel(h_ref, g_ref, w1_ref, w3_ref, w2_ref, o_ref, *, fc):
    h = h_ref[...]
    xn = _rms(h, g_ref[...]).astype(BF16)
    acc = h
    for c in range(w1_ref.shape[1] // fc):
        cs = slice(c * fc, (c + 1) * fc)
        a = _dot(xn, w1_ref[:, cs])
        b = _dot(xn, w3_ref[:, cs])
        acc = acc + _dot((jax.nn.silu(a) * b).astype(BF16), w2_ref[cs, :])
    o_ref[...] = acc


def _ffn(h, g, w1, w3, w2, tm):
    r, d = h.shape
    f = w1.shape[1]
    fc = MXU_DIM if f % MXU_DIM == 0 else f
    return pl.pallas_call(
        functools.partial(_ffn_kernel, fc=fc),
        grid=(r // tm,),
        in_specs=[pl.BlockSpec((tm, d), lambda i: (i, 0)), _full((1, d)),
                  _resident((d, f)), _resident((d, f)), _resident((f, d))],
        out_specs=pl.BlockSpec((tm, d), lambda i: (i, 0)),
        out_shape=jax.ShapeDtypeStruct((r, d), F32),
        compiler_params=_cp("parallel"),
        name="ffn",
    )(h, g, w1, w3, w2)


def _gla_proj_kernel(h_ref, g_ref, wf_ref, wg1_ref, wg2_ref, bgk_ref,
                     q_ref, k_ref, gk_ref, v_ref, gt_ref, win_ref, *, q_scale):
    @pl.when(pl.program_id(0) == 0)
    def _():
        win_ref[...] = wf_ref[...].astype(BF16)

    xn = _rms(h_ref[...], g_ref[...]).astype(BF16)
    kd = q_ref.shape[1]
    vd = v_ref.shape[1]
    q_ref[...] = _dot(xn, win_ref[:, 0:kd]) * q_scale
    k_ref[...] = _dot(xn, win_ref[:, kd:2 * kd])
    v_ref[...] = _dot(xn, win_ref[:, 2 * kd:2 * kd + vd]).astype(v_ref.dtype)
    gt_ref[...] = _dot(xn, win_ref[:, 2 * kd + vd:]).astype(gt_ref.dtype)
    low = _dot(xn, wg1_ref[...]).astype(BF16)
    z = _dot(low, wg2_ref[...]) + bgk_ref[...]
    log_sig = jnp.minimum(z, 0.0) - jnp.log1p(jnp.exp(-jnp.abs(z)))
    gk_ref[...] = log_sig / GLA_GATE_NORM


def _gla_proj(h, g, w_in, wg1, wg2, bgk, kd, vd, tm):
    r, d = h.shape
    rows = lambda n: pl.BlockSpec((tm, n), lambda i: (i, 0))
    return pl.pallas_call(
        functools.partial(_gla_proj_kernel, q_scale=float((kd // GLA_HEADS) ** -0.5)),
        grid=(r // tm,),
        in_specs=[rows(d), _full((1, d)), _resident(w_in.shape), _full(wg1.shape), _full(wg2.shape),
                  _full((1, kd))],
        out_specs=[rows(kd), rows(kd), rows(kd), rows(vd), rows(vd)],
        out_shape=[jax.ShapeDtypeStruct((r, kd), F32)] * 3 + [jax.ShapeDtypeStruct((r, vd), BF16)] * 2,
        scratch_shapes=[pltpu.VMEM(w_in.shape, BF16)],
        compiler_params=_cp("arbitrary"),
        name="gla_proj",
    )(h, g, w_in, wg1, wg2, bgk)


def _split3(x):
    a = x.astype(BF16)
    r = x - a.astype(F32)
    b = r.astype(BF16)
    c = (r - b.astype(F32)).astype(BF16)
    return a, b, c


def _gla_chunk_kernel(q_ref, k_ref, gk_ref, v_ref, o_ref, s_ref, st_ref, *, nchunk, rot, unroll, group):
    c_rows = CHUNK
    dk = q_ref.shape[2]
    nsub = c_rows // SUB
    row_i = lax.broadcasted_iota(jnp.int32, (c_rows, c_rows), 0)
    col_i = lax.broadcasted_iota(jnp.int32, (c_rows, c_rows), 1)
    tri = (row_i >= col_i).astype(BF16)
    sub_row = lax.broadcasted_iota(jnp.int32, (SUB, c_rows), 0)
    sub_col = lax.broadcasted_iota(jnp.int32, (SUB, c_rows), 1)
    half_col = lax.broadcasted_iota(jnp.int32, (SUB // 2, c_rows), 1)

    def cum_decay(rs):
        g1, g2, g3 = _split3(gk_ref[0, rs, :])
        return (_dot(tri, g1) + _dot(tri, g2) + _dot(tri, g3)) * LOG2_E

    def intra_fast(spans, bcs):
        rows = group * c_rows
        load = lambda ref: jnp.concatenate([ref[0, rs, :] for rs in spans], axis=0)
        q, k, vb = load(q_ref), load(k_ref), load(v_ref)
        run, pieces = jnp.zeros((1, dk), F32), []
        for bc in bcs:
            pieces.append(bc + run)
            run = run + bc[c_rows - 1:c_rows, :]
        bc = jnp.concatenate(pieces, axis=0)
        q_in = (q * jnp.exp2(bc)).astype(BF16)
        k_up = (k * jnp.exp2(-bc)).astype(BF16)
        causal = (lax.broadcasted_iota(jnp.int32, (rows, rows), 0)
                  >= lax.broadcasted_iota(jnp.int32, (rows, rows), 1))
        att = jnp.where(causal, _dot_nt(q_in, k_up), 0.0)
        o_intra = _dot(att.astype(BF16), vb)
        k_dec = (k * jnp.exp2(run - bc)).astype(BF16)
        return o_intra, q_in, k_dec, jnp.exp2(run), vb

    def intra_safe(rs, bc):
        q = q_ref[0, rs, :]
        k = k_ref[0, rs, :]
        vb = v_ref[0, rs, :]
        blast = bc[c_rows - 1:c_rows, :]
        q_sub, k_hat = [], []
        att_rows = []
        half = SUB // 2
        for i in range(nsub):
            lo = i * SUB
            bs = bc[lo - 1:lo, :] if i > 0 else jnp.zeros((1, dk), F32)
            bc_i = bc[lo:lo + SUB, :]
            q_i = q[lo:lo + SUB, :]
            k_i = k[lo:lo + SUB, :]
            q_sub.append(q_i * jnp.exp2(bc_i - bs))
            if i > 0:
                k_hat.append(jnp.concatenate(
                    [k[:lo, :] * jnp.exp2(bs - bc[:lo, :]), jnp.zeros((c_rows - lo, dk), F32)], axis=0))
            top = jnp.zeros((half, c_rows), F32)
            bot = jnp.zeros((half, c_rows), F32)
            for s in range(SUB):
                r0 = 0 if s < half else half
                e = jnp.exp2(jnp.minimum(bc_i[r0:, :] - bc_i[s:s + 1, :], 0.0))
                col = jnp.sum(q_i[r0:, :] * k_i[s:s + 1, :] * e, axis=-1, keepdims=True)
                if s < half:
                    top = jnp.where(half_col == lo + s, col[:half, :], top)
                bot = jnp.where(half_col == lo + s, col[half - r0:, :], bot)
            diag = jnp.concatenate([top, bot], axis=0)
            att_rows.append(jnp.where(sub_row + lo >= sub_col, diag, 0.0))
        att = jnp.concatenate(att_rows, axis=0)
        zero = jnp.zeros((SUB, dk), F32)
        lhs = jnp.concatenate(
            [jnp.concatenate([q_sub[i] if j == i else zero for j in range(1, nsub)], axis=1)
             for i in range(nsub)], axis=0)
        rhs = jnp.concatenate(k_hat, axis=1)
        att = att + _dot_nt(lhs.astype(BF16), rhs.astype(BF16))
        o_intra = _dot(att.astype(BF16), vb)
        q_in = (q * jnp.exp2(bc)).astype(BF16)
        k_dec = (k * jnp.exp2(blast - bc)).astype(BF16)
        return o_intra, q_in, k_dec, jnp.exp2(blast), vb

    st_ref[...] = jnp.zeros_like(st_ref)
    chunk_rows = lambda c: pl.ds(((c + rot) % nchunk) * c_rows, c_rows)
    blocks = [[chunk_rows(b * group + j) for j in range(group)] for b in range(nchunk // group)]
    all_bcs = [[cum_decay(rs) for rs in spans] for spans in blocks]
    totals = [-functools.reduce(jnp.add, [bc[c_rows - 1:c_rows, :] for bc in bcs]) for bcs in all_bcs]
    mild = jnp.max(functools.reduce(jnp.maximum, totals)) <= GLA_FAST_RANGE

    @pl.when(mild)
    def _():
        parts = [intra_fast(spans, bcs) for spans, bcs in zip(blocks, all_bcs)]
        st = st_ref[...]
        for spans, (o_intra, q_in, k_dec, decay, vb) in zip(blocks, parts):
            o = (o_intra + _dot_nt(q_in, st.astype(BF16))).astype(o_ref.dtype)
            for j, rs in enumerate(spans):
                o_ref[0, rs, :] = o[j * c_rows:(j + 1) * c_rows, :]
            st = st * decay + _dot_tn(vb, k_dec)
        st_ref[...] = st

    @pl.when(jnp.logical_not(mild))
    def _():
        def body(it, carry):
            spans = []
            for j in range(unroll):
                mem = lax.rem(it * unroll + j + rot, nchunk)
                spans.append(pl.ds(pl.multiple_of(mem * c_rows, c_rows), c_rows))
            parts = [intra_safe(rs, cum_decay(rs)) for rs in spans]
            st = st_ref[...]
            for rs, (o_intra, q_in, k_dec, decay, vb) in zip(spans, parts):
                o_ref[0, rs, :] = (o_intra + _dot_nt(q_in, st.astype(BF16))).astype(o_ref.dtype)
                st = st * decay + _dot_tn(vb, k_dec)
            st_ref[...] = st
            return carry

        lax.fori_loop(0, nchunk // unroll, body, 0)

    s_ref[0, 0] = st_ref[...].T


def _gla_chunked(q, k, gk, v, *, rot):
    b, l, kd = q.shape
    vd = v.shape[2]
    dk, dv = kd // GLA_HEADS, vd // GLA_HEADS
    seq = lambda n: pl.BlockSpec((1, l, n), lambda bi, hi: (bi, 0, hi))
    nchunk = l // CHUNK
    unroll = _first_divisor(nchunk, GLA_UNROLLS)
    return pl.pallas_call(
        functools.partial(_gla_chunk_kernel, nchunk=nchunk, rot=rot, unroll=unroll,
                          group=_first_divisor(nchunk, GLA_GROUPS)),
        grid=(b, GLA_HEADS),
        in_specs=[seq(dk), seq(dk), seq(dk), seq(dv)],
        out_specs=[seq(dv), pl.BlockSpec((1, 1, dk, dv), lambda bi, hi: (bi, hi, 0, 0))],
        out_shape=[jax.ShapeDtypeStruct((b, l, vd), BF16),
                   jax.ShapeDtypeStruct((b, GLA_HEADS, dk, dv), F32)],
        scratch_shapes=[pltpu.VMEM((dv, dk), F32)],
        compiler_params=_cp("parallel", "parallel"),
        name="gla_chunked",
    )(q, k, gk, v)


def _gla_step_kernel(qt_ref, kt_ref, gt_ref, v_ref, s0_ref, o_ref, s_ref):
    bt = v_ref.shape[1]
    dec = jnp.exp(gt_ref[0])
    kt = kt_ref[0]
    qt = qt_ref[0]
    vf = v_ref[0].astype(F32)
    for j in range(bt):
        v_row = vf[j:j + 1, :]
        s_new = dec[:, j:j + 1] * s0_ref[j, 0] + kt[:, j:j + 1] * v_row
        s_ref[j, 0] = s_new
        o_ref[0, j:j + 1, :] = jnp.sum(qt[:, j:j + 1] * s_new, axis=0, keepdims=True)


def _gla_step(q, k, gk, v, s0, bt=32):
    b, kd = q.shape
    vd = v.shape[1]
    dk, dv = kd // GLA_HEADS, vd // GLA_HEADS
    nt = b // bt
    cols = lambda t: t.reshape(nt, bt, kd).transpose(0, 2, 1)
    col_spec = pl.BlockSpec((1, dk, bt), lambda ti, hi: (ti, hi, 0))
    v_spec = pl.BlockSpec((1, bt, dv), lambda ti, hi: (ti, 0, hi))
    s_spec = pl.BlockSpec((bt, 1, dk, dv), lambda ti, hi: (ti, hi, 0, 0))
    o, s = pl.pallas_call(
        _gla_step_kernel,
        grid=(nt, GLA_HEADS),
        in_specs=[col_spec, col_spec, col_spec, v_spec, s_spec],
        out_specs=[v_spec, s_spec],
        out_shape=[jax.ShapeDtypeStruct((nt, bt, vd), F32), jax.ShapeDtypeStruct(s0.shape, F32)],
        compiler_params=_cp("parallel", "parallel"),
        name="gla_step",
    )(cols(q), cols(k), cols(gk), v.reshape(nt, bt, vd), s0)
    return o.reshape(b, vd), s


def _gla_out_kernel(o_ref, gt_ref, h_ref, gn_ref, wf_ref, out_ref, w_ref):
    @pl.when(jnp.logical_and(pl.program_id(0) == 0, pl.program_id(1) == 0))
    def _():
        w_ref[...] = wf_ref[...].astype(BF16)

    dv = gn_ref.shape[1]
    parts = []
    for hd in range(o_ref.shape[2] // dv):
        sl = slice(hd * dv, (hd + 1) * dv)
        o_h = o_ref[0, :, sl].astype(F32)
        g_h = gt_ref[0, :, sl].astype(F32)
        parts.append((_rms(o_h, gn_ref[...]) * jax.nn.silu(g_h)).astype(BF16))
    out_ref[0] = h_ref[0] + _dot(jnp.concatenate(parts, axis=1), w_ref[...])


def _gla_out(o, gt, h, gnorm, w, n_rows, tm):
    b, _, d = h.shape
    vd = o.shape[2]
    rows = lambda n: pl.BlockSpec((1, tm, n), lambda bi, i: (bi, i, 0))
    return pl.pallas_call(
        _gla_out_kernel,
        grid=(b, n_rows // tm),
        in_specs=[rows(vd), rows(vd), rows(d), _full(gnorm.shape), _resident(w.shape)],
        out_specs=rows(d),
        out_shape=jax.ShapeDtypeStruct((b, n_rows, d), F32),
        scratch_shapes=[pltpu.VMEM(w.shape, BF16)],
        compiler_params=_cp("arbitrary", "arbitrary"),
        name="gla_out",
    )(o, gt, h, gnorm, w)


def _moe_kernel(h_ref, g_ref, wr_ref, tri_ref, w1_ref, w3_ref, w2_ref, gf_ref, o_ref,
                pt_ref, sl_ref, gg_ref, meta_ref, *, ne):
    e = pl.program_id(2)
    t = h_ref.shape[1]
    nchunk_max = pt_ref.shape[0]

    @pl.when(e == 0)
    def _():
        h = h_ref[0]
        xn = _rms(h, g_ref[...])
        xb = xn.astype(BF16)
        nsub = wr_ref.shape[0]
        logits = _dot_nt(wr_ref[...], xb)
        sub = lax.broadcasted_iota(jnp.int32, logits.shape, 0)
        valid = sub < ne
        logits = jnp.where(valid, logits, -jnp.inf)
        p = jnp.exp(logits - jnp.max(logits, axis=0, keepdims=True))
        p = p / jnp.sum(p, axis=0, keepdims=True)
        p = jnp.where(valid, p, -1.0)
        m1 = jnp.max(p, axis=0, keepdims=True)
        i1 = jnp.min(jnp.where(p == m1, sub, nsub), axis=0, keepdims=True)
        rest = jnp.where(sub == i1, -1.0, p)
        m2 = jnp.max(rest, axis=0, keepdims=True)
        i2 = jnp.min(jnp.where(rest == m2, sub, nsub), axis=0, keepdims=True)
        tot = m1 + m2
        sel = (sub == i1) | (sub == i2)
        gates = jnp.where(sub == i1, m1 / tot, jnp.where(sub == i2, m2 / tot, 0.0))

        incl = _dot(jnp.where(sel, 1.0, 0.0).astype(BF16), tri_ref[...])
        blocks = jnp.floor((incl[:, t - 1:t] + (MOE_BLK - 1)) * (1.0 / MOE_BLK))
        before = (lax.broadcasted_iota(jnp.int32, (nsub, nsub), 1)
                  < lax.broadcasted_iota(jnp.int32, (nsub, nsub), 0)).astype(BF16)
        off = _dot(before, jnp.broadcast_to(blocks, (nsub, LANES)).astype(BF16))[:, 0:1]
        slot = jnp.where(sel, off * MOE_BLK + incl - 1.0, -1.0)
        s1 = jnp.sum(jnp.where(sub == i1, slot, 0.0), axis=0, keepdims=True)
        s2 = jnp.sum(jnp.where(sub == i2, slot, 0.0), axis=0, keepdims=True)
        g_src = jnp.concatenate([*_split3(gates), jnp.zeros((LANES - 3 * nsub, t), BF16)], axis=0)
        per_chunk = MOE_CHUNK // MOE_BLK
        nchunks = jnp.floor((jnp.sum(blocks) + (per_chunk - 1.0)) * (1.0 / per_chunk)).astype(jnp.int32)
        for c in range(nchunk_max):
            @pl.when(c < nchunks)
            def _(c=c):
                ids = (lax.broadcasted_iota(jnp.int32, (MOE_CHUNK, t), 0) + c * MOE_CHUNK).astype(F32)
                pc = (jnp.where(s1 == ids, 1.0, 0.0) + jnp.where(s2 == ids, 1.0, 0.0)).astype(BF16)
                rs = slice(c * MOE_CHUNK, (c + 1) * MOE_CHUNK)
                pt_ref[c] = pc
                sl_ref[rs, :] = _dot(pc, xb).astype(BF16)
                gg_ref[rs, :] = _dot_nt(pc, g_src)
        o_ref[0] = h
        for ee in range(ne):
            meta_ref[0, ee] = blocks[ee, 0].astype(jnp.int32)
            meta_ref[1, ee] = off[ee, 0].astype(jnp.int32)
        meta_ref[2, 0] = nchunks

    def expert_rows(first_blk, nblk):
        rows = nblk * MOE_BLK
        rs = pl.ds(pl.multiple_of(first_blk * MOE_BLK, MOE_BLK), rows)
        xg = sl_ref[rs, :]
        a = _dot(xg, w1_ref[0])
        b = _dot(xg, w3_ref[0])
        y = _dot((jax.nn.silu(a) * b).astype(BF16), w2_ref[0])
        lane_b = lax.broadcasted_iota(jnp.int32, (rows, LANES), 1)
        mine = jnp.bitwise_and(lane_b, wr_ref.shape[0] - 1) == e
        gate = jnp.sum(jnp.where(mine, gg_ref[rs, :], 0.0), axis=-1, keepdims=True)
        sl_ref[rs, :] = (gate * y).astype(BF16)

    nblk = meta_ref[0, e]
    first = meta_ref[1, e]
    rem = lax.rem(nblk, MOE_STRIDE)
    lead = jnp.where(jnp.logical_and(rem > 0, nblk > MOE_STRIDE), rem + MOE_STRIDE, rem)
    for n in list(range(1, MOE_STRIDE)) + list(range(MOE_STRIDE + 1, 2 * MOE_STRIDE)):
        @pl.when(lead == n)
        def _(n=n):
            expert_rows(first, n)

    def stride(js, carry):
        expert_rows(first + lead + MOE_STRIDE * js, MOE_STRIDE)
        return carry

    lax.fori_loop(0, lax.div(nblk - lead, MOE_STRIDE), stride, 0)

    @pl.when(e == ne - 1)
    def _():
        def combine(c, carry):
            rs = pl.ds(pl.multiple_of(c * MOE_CHUNK, MOE_CHUNK), MOE_CHUNK)
            o_ref[0] += _dot_tn(pt_ref[c], sl_ref[rs, :])
            return carry

        lax.fori_loop(0, meta_ref[2, 0], combine, 0)
        o_ref[0] = _rms(o_ref[0], gf_ref[...])


def _moe_final(h, g, w_router, w1, w3, w2, g_final, tm):
    b, l, d = h.shape
    ne, _, f = w1.shape
    slots = -(-(2 * tm + ne * MOE_BLK) // MOE_CHUNK) * MOE_CHUNK
    r = jnp.arange(tm)
    tri = (r[:, None] <= r[None, :]).astype(BF16)
    rows = pl.BlockSpec((1, tm, d), lambda bi, i, e: (bi, i, 0))
    return pl.pallas_call(
        functools.partial(_moe_kernel, ne=ne),
        grid=(b, l // tm, ne),
        in_specs=[rows, _full((1, d)), _full(w_router.shape), _resident((tm, tm)),
                  pl.BlockSpec((1, d, f), lambda bi, i, e: (e, 0, 0)),
                  pl.BlockSpec((1, d, f), lambda bi, i, e: (e, 0, 0)),
                  pl.BlockSpec((1, f, d), lambda bi, i, e: (e, 0, 0)), _full((1, d))],
        out_specs=rows,
        out_shape=jax.ShapeDtypeStruct((b, l, d), F32),
        scratch_shapes=[pltpu.VMEM((slots // MOE_CHUNK, MOE_CHUNK, tm), BF16), pltpu.VMEM((slots, d), BF16),
                        pltpu.VMEM((slots, LANES), F32),
                        pltpu.SMEM((3, ne), jnp.int32)],
        compiler_params=_cp("parallel", "parallel", "arbitrary"),
        name="moe_final",
    )(h, g, w_router, tri, w1, w3, w2, g_final)


def kernel(x_prompt, x_sample, state_pool, state_s5_re, state_s5_im, state_gla, meta_tokens, norm_mix_e, w_in_e, w_pool, pool_scale, s5_a_re, s5_a_im, s5_log_dt, s5_b_re, s5_b_im, s5_c_re, s5_c_im, s5_d, w_glu, b_glu, w_out_e, norm_ffn_e, ffn_w1, ffn_w3, ffn_w2, norm_mix_o, w_in_o, w_gk1, w_gk2, b_gk, gla_norm, w_out_o, norm_ffn_o, w_router, moe_w1, moe_w3, moe_w2, norm_final):
    bp, seq, d = x_prompt.shape
    bs = x_sample.shape[0]
    lp = seq + CHUNK
    dp = w_pool.shape[1] * w_pool.shape[2]
    g_ssm, n_ssm, p_ssm = s5_b_re.shape[1:]
    gn = g_ssm * n_ssm
    kd = w_gk2.shape[2]
    vd = w_out_o.shape[1]
    row = lambda t: t.reshape(1, -1)
    bf = lambda t: t.astype(BF16)

    j = 0
    w_in_e_b, w_out_e_b = w_in_e[j], w_out_e[j]
    w_pool_b, w_glu_b = bf(w_pool[j]), bf(w_glu[j])
    w1_b, w3_b, w2_b = bf(ffn_w1[j]), bf(ffn_w3[j]), bf(ffn_w2[j])
    lbr, lbi, bbr, bbi = _s5_prep(s5_a_re[j], s5_a_im[j], s5_log_dt[j], s5_b_re[j], s5_b_im[j])
    s5_args = (lbr, lbi, bf(_block_diag_in(bbr, g_ssm, n_ssm, p_ssm)), bf(_block_diag_in(bbi, g_ssm, n_ssm, p_ssm)),
               bf(_block_diag_out(s5_c_re[j])), bf(_block_diag_out(s5_c_im[j])),
               row(s5_d[j]), w_glu_b, row(b_glu[j]))
    w_in_o_b, w_out_o_b = w_in_o[j], w_out_o[j]
    rank = w_gk1.shape[2]
    wg1 = bf(jnp.pad(w_gk1[j], ((0, 0), (0, LANES - rank))))
    wg2 = bf(jnp.pad(w_gk2[j], ((0, LANES - rank), (0, 0))))
    w_router_b = bf(jnp.pad(w_router[j].T, ((0, MOE_ROUTE_ROWS - w_router.shape[2]), (0, 0))))
    mw1, mw3, mw2 = bf(moe_w1[j]), bf(moe_w3[j]), bf(moe_w2[j])
    mix_w = (row(norm_mix_e[j]), w_in_e_b)
    pool_w = (w_pool_b, row(pool_scale[j]))

    tail = jnp.concatenate([jnp.zeros((CHUNK - N_META, d), F32), meta_tokens.astype(F32)], axis=0)
    zero_state = jnp.zeros((bp, gn), F32)
    h1, pool_p, re_p, im_p = _mixer0(
        x_prompt, tail, *mix_w, jnp.zeros(((POOL_BUF + 1) * bp, dp), F32), *pool_w,
        zero_state, zero_state, s5_args, w_out_e_b, tt=CHUNK, nb=bp, pos0=-(CHUNK - N_META))
    tm = _first_divisor(bp * lp, ROW_TILES)
    h2 = _ffn(h1.reshape(bp * lp, d), row(norm_ffn_e[j]), w1_b, w3_b, w2_b, tm)
    q, k, gk, v, gt = _gla_proj(h2, row(norm_mix_o[j]), w_in_o_b, wg1, wg2, row(b_gk[j]), kd, vd, tm)
    seq3 = lambda t: t.reshape(bp, lp, t.shape[1])
    o, gla_p = _gla_chunked(seq3(q), seq3(k), seq3(gk), seq3(v), rot=lp // CHUNK - 1)
    ts = _first_divisor(seq, ROW_TILES)
    h3 = _gla_out(o, seq3(gt), seq3(h2), row(gla_norm[j]), w_out_o_b, seq, ts)
    y_prompt = _moe_final(h3, row(norm_ffn_o[j]), w_router_b, mw1, mw3, mw2, row(norm_final), ts)

    pool_init = jnp.pad(state_pool[j].transpose(1, 0, 2), ((1, 0), (0, 0), (0, 0))).reshape((POOL_BUF + 1) * bs, dp)
    h1s, pool_s, re_s, im_s = _mixer0(
        x_sample.reshape(1, bs, d), None, *mix_w, pool_init, *pool_w,
        state_s5_re[j].reshape(bs, gn), state_s5_im[j].reshape(bs, gn), s5_args, w_out_e_b,
        tt=1, nb=bs, pos0=PAST_LEN)
    h2s = _ffn(h1s.reshape(bs, d), row(norm_ffn_e[j]), w1_b, w3_b, w2_b, bs)
    qs, ks, gks, vs, gts = _gla_proj(h2s, row(norm_mix_o[j]), w_in_o_b, wg1, wg2, row(b_gk[j]), kd, vd, bs)
    os_, gla_s = _gla_step(qs, ks, gks, vs, state_gla[j])
    h3s = _gla_out(os_[None], gts[None], h2s[None], row(gla_norm[j]), w_out_o_b, bs, bs)
    y_sample = _moe_final(h3s, row(norm_ffn_o[j]), w_router_b, mw1, mw3, mw2, row(norm_final), bs)

    tb = lambda t, nb: t.reshape(POOL_BUF, nb, dp).transpose(1, 0, 2)[None]
    ssm = lambda t, nb: t.reshape(1, nb, g_ssm, n_ssm)
    return (y_prompt, y_sample.reshape(bs, 1, d), tb(pool_p, bp), tb(pool_s, bs),
            ssm(re_p, bp), ssm(re_s, bs), ssm(im_p, bp), ssm(im_s, bs), gla_p[None], gla_s[None])
```

```python
import functools

import jax
import jax.numpy as jnp
from jax import lax
from jax.experimental import pallas as pl
from jax.experimental.pallas import tpu as pltpu

F32 = jnp.float32
BF16 = jnp.bfloat16

EPS = 1e-6
LOG2_E = 1.4426950408889634
N_META = 16
PAST_LEN = 16384
POOL_WINDOWS = (2, 4, 8, 16)
POOL_BUF = max(POOL_WINDOWS) - 1
GLA_HEADS = 4
GLA_GATE_NORM = 16.0
CHUNK = 64
SUB = 16
GLA_FAST_RANGE = 64.0
GLA_GROUPS = (3, 2, 1)
GLA_UNROLLS = (11, 3, 2, 1)
MXU_DIM = 256
LANES = 128
MOE_BLK = 64
MOE_STRIDE = MXU_DIM // MOE_BLK
MOE_CHUNK = MXU_DIM
MOE_ROUTE_ROWS = 16
S5_COLS = 4 * LANES
ROW_TILES = (1024, 768, 512, 256, 128, 8)
V7X_VMEM_BYTES = 64 * 1024 * 1024
VMEM_LIMIT = V7X_VMEM_BYTES // 8 * 7


def _cp(*sem, vmem=VMEM_LIMIT):
    return pltpu.CompilerParams(dimension_semantics=sem, vmem_limit_bytes=vmem)


def _rms(x, g):
    return x * lax.rsqrt(jnp.mean(x * x, axis=-1, keepdims=True) + EPS) * g


def _dot(a, b):
    return jnp.dot(a, b, preferred_element_type=F32)


def _dot_nt(a, b):
    return lax.dot_general(a, b, (((1,), (1,)), ((), ())), preferred_element_type=F32)


def _dot_tn(a, b):
    return lax.dot_general(a, b, (((0,), (0,)), ((), ())), preferred_element_type=F32)


def _full(shape):
    return pl.BlockSpec(shape, lambda *_: (0,) * len(shape))


def _resident(shape):
    return pl.BlockSpec(shape, lambda *_: (0,) * len(shape), pipeline_mode=pl.Buffered(1))


def _first_divisor(n, candidates):
    return next(c for c in candidates if n % c == 0)


def _s5_prep_kernel(ar_ref, ai_ref, ldt_ref, br_ref, bi_ref, lbr_ref, lbi_ref, bbr_ref, bbi_ref):
    dt = jnp.exp(ldt_ref[...])
    ar = ar_ref[...]
    ai = ai_ref[...]
    mag = jnp.exp(ar * dt)
    lb_re = mag * jnp.cos(ai * dt)
    lb_im = mag * jnp.sin(ai * dt)
    den = ar * ar + ai * ai
    nr = lb_re - 1.0
    f_re = (nr * ar + lb_im * ai) / den
    f_im = (lb_im * ar - nr * ai) / den
    lbr_ref[...] = lb_re
    lbi_ref[...] = lb_im
    br = br_ref[...]
    bi = bi_ref[...]
    bbr_ref[...] = f_re * br - f_im * bi
    bbi_ref[...] = f_re * bi + f_im * br


def _s5_prep(a_re, a_im, log_dt, b_re, b_im):
    g, n, p = b_re.shape
    gn = g * n
    row = lambda t: t.reshape(1, gn)
    to_pgn = lambda t: t.transpose(2, 0, 1).reshape(p, gn)
    ldt = jnp.broadcast_to(log_dt[:, None], (g, n))
    return pl.pallas_call(
        _s5_prep_kernel,
        out_shape=[jax.ShapeDtypeStruct((1, gn), F32)] * 2 + [jax.ShapeDtypeStruct((p, gn), F32)] * 2,
        name="s5_prep",
    )(row(a_re), row(a_im), row(ldt), to_pgn(b_re), to_pgn(b_im))


def _block_diag_in(bb_pgn, g, n, p):
    bb = bb_pgn.reshape(p, g, n).transpose(1, 0, 2)
    eye = jnp.eye(g, dtype=bb.dtype)
    full = (bb[:, :, None, :] * eye[:, None, :, None]).reshape(g * p, g * n)
    nblk = g * p // MXU_DIM
    cols = g * n // nblk
    return jnp.stack([full[k * MXU_DIM:(k + 1) * MXU_DIM, k * cols:(k + 1) * cols] for k in range(nblk)])


def _block_diag_out(c_gpn):
    g, p, n = c_gpn.shape
    eye = jnp.eye(g, dtype=c_gpn.dtype)
    full = (c_gpn.transpose(0, 2, 1)[:, :, None, :] * eye[:, None, :, None]).reshape(g * n, g * p)
    nblk = g * p // MXU_DIM
    rows = g * n // nblk
    return jnp.stack([full[k * rows:(k + 1) * rows, k * MXU_DIM:(k + 1) * MXU_DIM] for k in range(nblk)])


def _pool_block(u_a, i, ext_ref, wp_ref, scale_ref, *, tt, nb, pos0):
    rows = tt * nb
    halo = (POOL_BUF + 1) * nb
    shift = nb.bit_length() - 1
    ext_ref[halo:halo + rows, :] = u_a
    t_in = lax.shift_right_logical(lax.broadcasted_iota(jnp.int32, (rows, LANES), 0), shift)
    pos1 = t_in + (pos0 + 1 + i * tt)
    group = u_a.shape[1] // len(POOL_WINDOWS)
    ys = []
    for gi, w in enumerate(POOL_WINDOWS):
        lo = gi * group
        s = ext_ref[(POOL_BUF + 2 - w) * nb:halo + rows, lo:lo + group]
        k = 1
        while k < w:
            n = s.shape[0]
            s = s[k * nb:, :] + s[:n - k * nb, :]
            k *= 2
        cnt = jnp.clip(pos1, 1, w).astype(F32)
        d = s / cnt - u_a[:, lo:lo + group]
        ys.append(_dot(d.astype(BF16), wp_ref[gi]) * scale_ref[:, lo:lo + group])
    tail = ext_ref[rows:rows + halo, :]
    ext_ref[0:halo, :] = tail
    return jnp.concatenate(ys, axis=1), tail[nb:, :]


def _s5_block(u_b, lbr_ref, lbi_ref, wbr_ref, wbi_ref, wcr_ref, wci_ref, dsk_ref, wg_ref, bg_ref,
              bur_ref, bui_ref, hr_ref, hi_ref, *, tt, nb):
    gn = bur_ref.shape[1]
    nkb = wbr_ref.shape[0]
    sb = gn // nkb
    ub = u_b.astype(BF16)
    for kb in range(nkb):
        uk = ub[:, kb * MXU_DIM:(kb + 1) * MXU_DIM]
        bur_ref[:, kb * sb:(kb + 1) * sb] = _dot(uk, wbr_ref[kb])
        bui_ref[:, kb * sb:(kb + 1) * sb] = _dot(uk, wbi_ref[kb])

    cw = S5_COLS
    for c in range(gn // cw):
        cols = slice(c * cw, (c + 1) * cw)
        lr = jnp.broadcast_to(lbr_ref[:, cols], (nb, cw))
        li = jnp.broadcast_to(lbi_ref[:, cols], (nb, cw))

        h_re, h_im = hr_ref[:, cols], hi_ref[:, cols]
        for t in range(tt):
            rows = slice(t * nb, (t + 1) * nb)
            h_re, h_im = (lr * h_re - li * h_im + bur_ref[rows, cols],
                          lr * h_im + li * h_re + bui_ref[rows, cols])
            bur_ref[rows, cols] = h_re
            bui_ref[rows, cols] = h_im
        hr_ref[:, cols] = h_re
        hi_ref[:, cols] = h_im

    zs = []
    for kb in range(nkb):
        hrb = bur_ref[:, kb * sb:(kb + 1) * sb].astype(BF16)
        hib = bui_ref[:, kb * sb:(kb + 1) * sb].astype(BF16)
        ch = slice(kb * MXU_DIM, (kb + 1) * MXU_DIM)
        y = _dot(hrb, wcr_ref[kb]) - _dot(hib, wci_ref[kb]) + dsk_ref[:, ch] * u_b[:, ch]
        zs.append(jax.nn.gelu(y))
    z = jnp.concatenate(zs, axis=1)
    return z * jax.nn.sigmoid(_dot(z.astype(BF16), wg_ref[...]) + bg_ref[...])


def _mixer0_kernel(*refs, tt, nb, pos0, prompt):
    if prompt:
        x_ref, tail_ref, perm_ref, permt_ref = refs[:4]
        refs = refs[4:]
    else:
        x_ref = refs[0]
        refs = refs[1:]
    (g_ref, win_ref, pinit_ref, wp_ref, scale_ref, h0r_ref, h0i_ref, lbr_ref, lbi_ref, wbr_ref, wbi_ref,
     wcr_ref, wci_ref, dsk_ref, wg_ref, bg_ref, wout_ref,
     h_ref, pst_ref, str_ref, sti_ref, ext_ref, bur_ref, bui_ref, hr_ref, hi_ref) = refs
    i = pl.program_id(0)
    rows = tt * nb
    d = x_ref.shape[-1]
    dp = ext_ref.shape[1]

    @pl.when(i == 0)
    def _():
        ext_ref[0:(POOL_BUF + 1) * nb, :] = pinit_ref[...]
        hr_ref[...] = h0r_ref[...]
        hi_ref[...] = h0i_ref[...]

    x = x_ref[...]
    if prompt:
        x = jnp.where(i == 0, jnp.broadcast_to(tail_ref[...][None], x.shape), x)
    x2 = x.reshape(rows, d)
    xn = _rms(x2, g_ref[...]).astype(BF16)
    if prompt:
        xn = _dot(perm_ref[...], xn).astype(BF16)
    u = _dot(xn, win_ref[...])
    ya, pool_tail = _pool_block(u[:, :dp], i, ext_ref, wp_ref, scale_ref, tt=tt, nb=nb, pos0=pos0)
    yb = _s5_block(u[:, dp:], lbr_ref, lbi_ref, wbr_ref, wbi_ref, wcr_ref, wci_ref, dsk_ref, wg_ref, bg_ref,
                   bur_ref, bui_ref, hr_ref, hi_ref, tt=tt, nb=nb)
    y = jnp.concatenate([ya, yb], axis=1).astype(BF16)
    if prompt:
        y = _dot(permt_ref[...], y).astype(BF16)
    h_ref[...] = x + _dot(y, wout_ref[...]).reshape(x.shape)

    @pl.when(i == pl.num_programs(0) - 1)
    def _():
        pst_ref[...] = pool_tail
        str_ref[...] = hr_ref[...]
        sti_ref[...] = hi_ref[...]


def _mixer0(x, tail, g, w_in, pool_init, wp, scale, h0r, h0i, s5, w_out, *, tt, nb, pos0):
    prompt = tail is not None
    d = x.shape[-1]
    dp = wp.shape[0] * wp.shape[1]
    gn = h0r.shape[1]
    rows = tt * nb
    halo = (POOL_BUF + 1) * nb
    lbr, lbi, wbr, wbi, wcr, wci, dsk, wg, bg = s5
    if prompt:
        nblk = x.shape[1] // tt + 1
        out_rows = x.shape[1] + tt
        r = jnp.arange(rows)
        perm = (r[:, None] % nb * tt + r[:, None] // nb == r[None, :]).astype(BF16)
        lead = [x, tail, perm, perm.T]
        lead_specs = [pl.BlockSpec((nb, tt, d), lambda i: (0, jnp.maximum(i - 1, 0), 0)),
                      _full((tt, d)), _full((rows, rows)), _full((rows, rows))]
        h_spec = pl.BlockSpec((nb, tt, d), lambda i: (0, (i + nblk - 1) % nblk, 0))
        h_shape = (nb, out_rows, d)
    else:
        nblk = 1
        lead = [x]
        lead_specs = [_full(x.shape)]
        h_spec = _full(x.shape)
        h_shape = x.shape
    args = lead + [g, w_in, pool_init, wp, scale, h0r, h0i, lbr, lbi, wbr, wbi, wcr, wci, dsk, wg, bg, w_out]
    in_specs = lead_specs + [_full(a.shape) for a in args[len(lead):]]
    return pl.pallas_call(
        functools.partial(_mixer0_kernel, tt=tt, nb=nb, pos0=pos0, prompt=prompt),
        grid=(nblk,),
        in_specs=in_specs,
        out_specs=[h_spec, _full((POOL_BUF * nb, dp)), _full((nb, gn)), _full((nb, gn))],
        out_shape=[jax.ShapeDtypeStruct(h_shape, F32), jax.ShapeDtypeStruct((POOL_BUF * nb, dp), F32),
                   jax.ShapeDtypeStruct((nb, gn), F32), jax.ShapeDtypeStruct((nb, gn), F32)],
        scratch_shapes=[pltpu.VMEM((halo + rows, dp), F32), pltpu.VMEM((rows, gn), F32),
                        pltpu.VMEM((rows, gn), F32), pltpu.VMEM((nb, gn), F32), pltpu.VMEM((nb, gn), F32)],
        compiler_params=_cp("arbitrary"),
        name="mixer0",
    )(*args)


def _ffn_kernel(h_ref, g_ref, w1_ref, w3_ref, w2_ref, o_ref, *, fc):
    h = h_ref[...]
    xn = _rms(h, g_ref[...]).astype(BF16)
    acc = h
    for c in range(w1_ref.shape[1] // fc):
        cs = slice(c * fc, (c + 1) * fc)
        a = _dot(xn, w1_ref[:, cs])
        b = _dot(xn, w3_ref[:, cs])
        acc = acc + _dot((jax.nn.silu(a) * b).astype(BF16), w2_ref[cs, :])
    o_ref[...] = acc


def _ffn(h, g, w1, w3, w2, tm):
    r, d = h.shape
    f = w1.shape[1]
    fc = MXU_DIM if f % MXU_DIM == 0 else f
    return pl.pallas_call(
        functools.partial(_ffn_kernel, fc=fc),
        grid=(r // tm,),
        in_specs=[pl.BlockSpec((tm, d), lambda i: (i, 0)), _full((1, d)),
                  _resident((d, f)), _resident((d, f)), _resident((f, d))],
        out_specs=pl.BlockSpec((tm, d), lambda i: (i, 0)),
        out_shape=jax.ShapeDtypeStruct((r, d), F32),
        compiler_params=_cp("parallel"),
        name="ffn",
    )(h, g, w1, w3, w2)


def _gla_proj_kernel(h_ref, g_ref, win_ref, wg1_ref, wg2_ref, bgk_ref,
                     q_ref, k_ref, gk_ref, v_ref, gt_ref, *, q_scale):
    xn = _rms(h_ref[...], g_ref[...]).astype(BF16)
    kd = q_ref.shape[1]
    vd = v_ref.shape[1]
    q_ref[...] = _dot(xn, win_ref[:, 0:kd]) * q_scale
    k_ref[...] = _dot(xn, win_ref[:, kd:2 * kd])
    v_ref[...] = _dot(xn, win_ref[:, 2 * kd:2 * kd + vd]).astype(v_ref.dtype)
    gt_ref[...] = _dot(xn, win_ref[:, 2 * kd + vd:]).astype(gt_ref.dtype)
    low = _dot(xn, wg1_ref[...]).astype(BF16)
    z = _dot(low, wg2_ref[...]) + bgk_ref[...]
    log_sig = jnp.minimum(z, 0.0) - jnp.log1p(jnp.exp(-jnp.abs(z)))
    gk_ref[...] = log_sig / GLA_GATE_NORM


def _gla_proj(h, g, w_in, wg1, wg2, bgk, kd, vd, tm):
    r, d = h.shape
    rows = lambda n: pl.BlockSpec((tm, n), lambda i: (i, 0))
    return pl.pallas_call(
        functools.partial(_gla_proj_kernel, q_scale=float((kd // GLA_HEADS) ** -0.5)),
        grid=(r // tm,),
        in_specs=[rows(d), _full((1, d)), _resident(w_in.shape), _full(wg1.shape), _full(wg2.shape),
                  _full((1, kd))],
        out_specs=[rows(kd), rows(kd), rows(kd), rows(vd), rows(vd)],
        out_shape=[jax.ShapeDtypeStruct((r, kd), F32)] * 3 + [jax.ShapeDtypeStruct((r, vd), BF16)] * 2,
        compiler_params=_cp("parallel"),
        name="gla_proj",
    )(h, g, w_in, wg1, wg2, bgk)


def _split3(x):
    a = x.astype(BF16)
    r = x - a.astype(F32)
    b = r.astype(BF16)
    c = (r - b.astype(F32)).astype(BF16)
    return a, b, c


def _gla_chunk_kernel(q_ref, k_ref, gk_ref, v_ref, o_ref, s_ref, st_ref, *, nchunk, rot, unroll, group):
    c_rows = CHUNK
    dk = q_ref.shape[2]
    nsub = c_rows // SUB
    row_i = lax.broadcasted_iota(jnp.int32, (c_rows, c_rows), 0)
    col_i = lax.broadcasted_iota(jnp.int32, (c_rows, c_rows), 1)
    tri = (row_i >= col_i).astype(BF16)
    sub_row = lax.broadcasted_iota(jnp.int32, (SUB, c_rows), 0)
    sub_col = lax.broadcasted_iota(jnp.int32, (SUB, c_rows), 1)
    half_col = lax.broadcasted_iota(jnp.int32, (SUB // 2, c_rows), 1)

    def cum_decay(rs):
        g1, g2, g3 = _split3(gk_ref[0, rs, :])
        return (_dot(tri, g1) + _dot(tri, g2) + _dot(tri, g3)) * LOG2_E

    def intra_fast(spans, bcs):
        rows = group * c_rows
        load = lambda ref: jnp.concatenate([ref[0, rs, :] for rs in spans], axis=0)
        q, k, vb = load(q_ref), load(k_ref), load(v_ref)
        run, pieces = jnp.zeros((1, dk), F32), []
        for bc in bcs:
            pieces.append(bc + run)
            run = run + bc[c_rows - 1:c_rows, :]
        bc = jnp.concatenate(pieces, axis=0)
        q_in = (q * jnp.exp2(bc)).astype(BF16)
        k_up = (k * jnp.exp2(-bc)).astype(BF16)
        causal = (lax.broadcasted_iota(jnp.int32, (rows, rows), 0)
                  >= lax.broadcasted_iota(jnp.int32, (rows, rows), 1))
        att = jnp.where(causal, _dot_nt(q_in, k_up), 0.0)
        o_intra = _dot(att.astype(BF16), vb)
        k_dec = (k * jnp.exp2(run - bc)).astype(BF16)
        return o_intra, q_in, k_dec, jnp.exp2(run), vb

    def intra_safe(rs, bc):
        q = q_ref[0, rs, :]
        k = k_ref[0, rs, :]
        vb = v_ref[0, rs, :]
        blast = bc[c_rows - 1:c_rows, :]
        q_sub, k_hat = [], []
        att_rows = []
        half = SUB // 2
        for i in range(nsub):
            lo = i * SUB
            bs = bc[lo - 1:lo, :] if i > 0 else jnp.zeros((1, dk), F32)
            bc_i = bc[lo:lo + SUB, :]
            q_i = q[lo:lo + SUB, :]
            k_i = k[lo:lo + SUB, :]
            q_sub.append(q_i * jnp.exp2(bc_i - bs))
            if i > 0:
                k_hat.append(jnp.concatenate(
                    [k[:lo, :] * jnp.exp2(bs - bc[:lo, :]), jnp.zeros((c_rows - lo, dk), F32)], axis=0))
            top = jnp.zeros((half, c_rows), F32)
            bot = jnp.zeros((half, c_rows), F32)
            for s in range(SUB):
                r0 = 0 if s < half else half
                e = jnp.exp2(jnp.minimum(bc_i[r0:, :] - bc_i[s:s + 1, :], 0.0))
                col = jnp.sum(q_i[r0:, :] * k_i[s:s + 1, :] * e, axis=-1, keepdims=True)
                if s < half:
                    top = jnp.where(half_col == lo + s, col[:half, :], top)
                bot = jnp.where(half_col == lo + s, col[half - r0:, :], bot)
            diag = jnp.concatenate([top, bot], axis=0)
            att_rows.append(jnp.where(sub_row + lo >= sub_col, diag, 0.0))
        att = jnp.concatenate(att_rows, axis=0)
        zero = jnp.zeros((SUB, dk), F32)
        lhs = jnp.concatenate(
            [jnp.concatenate([q_sub[i] if j == i else zero for j in range(1, nsub)], axis=1)
             for i in range(nsub)], axis=0)
        rhs = jnp.concatenate(k_hat, axis=1)
        att = att + _dot_nt(lhs.astype(BF16), rhs.astype(BF16))
        o_intra = _dot(att.astype(BF16), vb)
        q_in = (q * jnp.exp2(bc)).astype(BF16)
        k_dec = (k * jnp.exp2(blast - bc)).astype(BF16)
        return o_intra, q_in, k_dec, jnp.exp2(blast), vb

    st_ref[...] = jnp.zeros_like(st_ref)
    chunk_rows = lambda c: pl.ds(((c + rot) % nchunk) * c_rows, c_rows)
    blocks = [[chunk_rows(b * group + j) for j in range(group)] for b in range(nchunk // group)]
    all_bcs = [[cum_decay(rs) for rs in spans] for spans in blocks]
    totals = [-functools.reduce(jnp.add, [bc[c_rows - 1:c_rows, :] for bc in bcs]) for bcs in all_bcs]
    mild = jnp.max(functools.reduce(jnp.maximum, totals)) <= GLA_FAST_RANGE

    @pl.when(mild)
    def _():
        parts = [intra_fast(spans, bcs) for spans, bcs in zip(blocks, all_bcs)]
        st = st_ref[...]
        for spans, (o_intra, q_in, k_dec, decay, vb) in zip(blocks, parts):
            o = (o_intra + _dot_nt(q_in, st.astype(BF16))).astype(o_ref.dtype)
            for j, rs in enumerate(spans):
                o_ref[0, rs, :] = o[j * c_rows:(j + 1) * c_rows, :]
            st = st * decay + _dot_tn(vb, k_dec)
        st_ref[...] = st

    @pl.when(jnp.logical_not(mild))
    def _():
        def body(it, carry):
            spans = []
            for j in range(unroll):
                mem = lax.rem(it * unroll + j + rot, nchunk)
                spans.append(pl.ds(pl.multiple_of(mem * c_rows, c_rows), c_rows))
            parts = [intra_safe(rs, cum_decay(rs)) for rs in spans]
            st = st_ref[...]
            for rs, (o_intra, q_in, k_dec, decay, vb) in zip(spans, parts):
                o_ref[0, rs, :] = (o_intra + _dot_nt(q_in, st.astype(BF16))).astype(o_ref.dtype)
                st = st * decay + _dot_tn(vb, k_dec)
            st_ref[...] = st
            return carry

        lax.fori_loop(0, nchunk // unroll, body, 0)

    s_ref[0, 0] = st_ref[...].T


def _gla_chunked(q, k, gk, v, *, rot):
    b, l, kd = q.shape
    vd = v.shape[2]
    dk, dv = kd // GLA_HEADS, vd // GLA_HEADS
    seq = lambda n: pl.BlockSpec((1, l, n), lambda bi, hi: (bi, 0, hi))
    nchunk = l // CHUNK
    unroll = _first_divisor(nchunk, GLA_UNROLLS)
    return pl.pallas_call(
        functools.partial(_gla_chunk_kernel, nchunk=nchunk, rot=rot, unroll=unroll,
                          group=_first_divisor(nchunk, GLA_GROUPS)),
        grid=(b, GLA_HEADS),
        in_specs=[seq(dk), seq(dk), seq(dk), seq(dv)],
        out_specs=[seq(dv), pl.BlockSpec((1, 1, dk, dv), lambda bi, hi: (bi, hi, 0, 0))],
        out_shape=[jax.ShapeDtypeStruct((b, l, vd), BF16),
                   jax.ShapeDtypeStruct((b, GLA_HEADS, dk, dv), F32)],
        scratch_shapes=[pltpu.VMEM((dv, dk), F32)],
        compiler_params=_cp("parallel", "parallel"),
        name="gla_chunked",
    )(q, k, gk, v)


def _gla_step_kernel(qt_ref, kt_ref, gt_ref, v_ref, s0_ref, o_ref, s_ref):
    bt = v_ref.shape[1]
    dec = jnp.exp(gt_ref[0])
    kt = kt_ref[0]
    qt = qt_ref[0]
    vf = v_ref[0].astype(F32)
    for j in range(bt):
        v_row = vf[j:j + 1, :]
        s_new = dec[:, j:j + 1] * s0_ref[j, 0] + kt[:, j:j + 1] * v_row
        s_ref[j, 0] = s_new
        o_ref[0, j:j + 1, :] = jnp.sum(qt[:, j:j + 1] * s_new, axis=0, keepdims=True)


def _gla_step(q, k, gk, v, s0, bt=32):
    b, kd = q.shape
    vd = v.shape[1]
    dk, dv = kd // GLA_HEADS, vd // GLA_HEADS
    nt = b // bt
    cols = lambda t: t.reshape(nt, bt, kd).transpose(0, 2, 1)
    col_spec = pl.BlockSpec((1, dk, bt), lambda ti, hi: (ti, hi, 0))
    v_spec = pl.BlockSpec((1, bt, dv), lambda ti, hi: (ti, 0, hi))
    s_spec = pl.BlockSpec((bt, 1, dk, dv), lambda ti, hi: (ti, hi, 0, 0))
    o, s = pl.pallas_call(
        _gla_step_kernel,
        grid=(nt, GLA_HEADS),
        in_specs=[col_spec, col_spec, col_spec, v_spec, s_spec],
        out_specs=[v_spec, s_spec],
        out_shape=[jax.ShapeDtypeStruct((nt, bt, vd), F32), jax.ShapeDtypeStruct(s0.shape, F32)],
        compiler_params=_cp("parallel", "parallel"),
        name="gla_step",
    )(cols(q), cols(k), cols(gk), v.reshape(nt, bt, vd), s0)
    return o.reshape(b, vd), s


def _ffn_gla_step_kernel(h_ref, g_ref, w1_ref, w3_ref, w2_ref, qt_ref, kt_ref, gt_ref, v_ref, s0_ref,
                         o_ref, os_ref, s_ref, *, fc):
    _gla_step_kernel(qt_ref, kt_ref, gt_ref, v_ref, s0_ref, os_ref, s_ref)
    _ffn_kernel(h_ref, g_ref, w1_ref, w3_ref, w2_ref, o_ref, fc=fc)


def _ffn_with_decode_gla(h, g, w1, w3, w2, q, k, gk, v, s0, bt=16):
    r, d = h.shape
    f = w1.shape[1]
    fc = MXU_DIM if f % MXU_DIM == 0 else f
    b, kd = q.shape
    vd = v.shape[1]
    dk, dv = kd // GLA_HEADS, vd // GLA_HEADS
    nt = b // bt
    steps = nt * GLA_HEADS
    tm = r // steps
    assert r % steps == 0 and tm % 16 == 0, (r, steps)
    cols = lambda t: t.reshape(nt, bt, kd).transpose(0, 2, 1)
    col_spec = pl.BlockSpec((1, dk, bt), lambda i: (i // GLA_HEADS, i % GLA_HEADS, 0))
    v_spec = pl.BlockSpec((1, bt, dv), lambda i: (i // GLA_HEADS, 0, i % GLA_HEADS))
    s_spec = pl.BlockSpec((bt, 1, dk, dv), lambda i: (i // GLA_HEADS, i % GLA_HEADS, 0, 0))
    rows = pl.BlockSpec((tm, d), lambda i: (i, 0))
    out, o, s = pl.pallas_call(
        functools.partial(_ffn_gla_step_kernel, fc=fc),
        grid=(steps,),
        in_specs=[rows, _full((1, d)), _resident((d, f)), _resident((d, f)), _resident((f, d)),
                  col_spec, col_spec, col_spec, v_spec, s_spec],
        out_specs=[rows, v_spec, s_spec],
        out_shape=[jax.ShapeDtypeStruct((r, d), F32), jax.ShapeDtypeStruct((nt, bt, vd), F32),
                   jax.ShapeDtypeStruct(s0.shape, F32)],
        compiler_params=_cp("parallel"),
        name="ffn_gla_step",
    )(h, g, w1, w3, w2, cols(q), cols(k), cols(gk), v.reshape(nt, bt, vd), s0)
    return out, o.reshape(b, vd), s


def _gla_out_kernel(o_ref, gt_ref, h_ref, gn_ref, w_ref, out_ref):
    dv = gn_ref.shape[1]
    parts = []
    for hd in range(o_ref.shape[2] // dv):
        sl = slice(hd * dv, (hd + 1) * dv)
        o_h = o_ref[0, :, sl].astype(F32)
        g_h = gt_ref[0, :, sl].astype(F32)
        parts.append((_rms(o_h, gn_ref[...]) * jax.nn.silu(g_h)).astype(BF16))
    out_ref[0] = h_ref[0] + _dot(jnp.concatenate(parts, axis=1), w_ref[...])


def _gla_out(o, gt, h, gnorm, w, n_rows, tm):
    b, _, d = h.shape
    vd = o.shape[2]
    rows = lambda n: pl.BlockSpec((1, tm, n), lambda bi, i: (bi, i, 0))
    return pl.pallas_call(
        _gla_out_kernel,
        grid=(b, n_rows // tm),
        in_specs=[rows(vd), rows(vd), rows(d), _full(gnorm.shape), _resident(w.shape)],
        out_specs=rows(d),
        out_shape=jax.ShapeDtypeStruct((b, n_rows, d), F32),
        compiler_params=_cp("parallel", "parallel"),
        name="gla_out",
    )(o, gt, h, gnorm, w)


def _moe_kernel(h_ref, g_ref, wr_ref, tri_ref, w1_ref, w3_ref, w2_ref, gf_ref, o_ref,
                pt_ref, sl_ref, gg_ref, meta_ref, *, ne):
    e = pl.program_id(2)
    t = h_ref.shape[1]
    nchunk_max = pt_ref.shape[0]

    @pl.when(e == 0)
    def _():
        h = h_ref[0]
        xn = _rms(h, g_ref[...])
        xb = xn.astype(BF16)
        nsub = wr_ref.shape[0]
        logits = _dot_nt(wr_ref[...], xb)
        sub = lax.broadcasted_iota(jnp.int32, logits.shape, 0)
        valid = sub < ne
        logits = jnp.where(valid, logits, -jnp.inf)
        p = jnp.exp(logits - jnp.max(logits, axis=0, keepdims=True))
        p = p / jnp.sum(p, axis=0, keepdims=True)
        p = jnp.where(valid, p, -1.0)
        m1 = jnp.max(p, axis=0, keepdims=True)
        i1 = jnp.min(jnp.where(p == m1, sub, nsub), axis=0, keepdims=True)
        rest = jnp.where(sub == i1, -1.0, p)
        m2 = jnp.max(rest, axis=0, keepdims=True)
        i2 = jnp.min(jnp.where(rest == m2, sub, nsub), axis=0, keepdims=True)
        tot = m1 + m2
        sel = (sub == i1) | (sub == i2)
        gates = jnp.where(sub == i1, m1 / tot, jnp.where(sub == i2, m2 / tot, 0.0))

        incl = _dot(jnp.where(sel, 1.0, 0.0).astype(BF16), tri_ref[...])
        blocks = jnp.floor((incl[:, t - 1:t] + (MOE_BLK - 1)) * (1.0 / MOE_BLK))
        before = (lax.broadcasted_iota(jnp.int32, (nsub, nsub), 1)
                  < lax.broadcasted_iota(jnp.int32, (nsub, nsub), 0)).astype(BF16)
        off = _dot(before, jnp.broadcast_to(blocks, (nsub, LANES)).astype(BF16))[:, 0:1]
        slot = jnp.where(sel, off * MOE_BLK + incl - 1.0, -1.0)
        s1 = jnp.sum(jnp.where(sub == i1, slot, 0.0), axis=0, keepdims=True)
        s2 = jnp.sum(jnp.where(sub == i2, slot, 0.0), axis=0, keepdims=True)
        g_src = jnp.concatenate([*_split3(gates), jnp.zeros((LANES - 3 * nsub, t), BF16)], axis=0)
        per_chunk = MOE_CHUNK // MOE_BLK
        nchunks = jnp.floor((jnp.sum(blocks) + (per_chunk - 1.0)) * (1.0 / per_chunk)).astype(jnp.int32)
        for c in range(nchunk_max):
            @pl.when(c < nchunks)
            def _(c=c):
                ids = (lax.broadcasted_iota(jnp.int32, (MOE_CHUNK, t), 0) + c * MOE_CHUNK).astype(F32)
                pc = (jnp.where(s1 == ids, 1.0, 0.0) + jnp.where(s2 == ids, 1.0, 0.0)).astype(BF16)
                rs = slice(c * MOE_CHUNK, (c + 1) * MOE_CHUNK)
                pt_ref[c] = pc
                sl_ref[rs, :] = _dot(pc, xb).astype(BF16)
                gg_ref[rs, :] = _dot_nt(pc, g_src)
        o_ref[0] = h
        for ee in range(ne):
            meta_ref[0, ee] = blocks[ee, 0].astype(jnp.int32)
            meta_ref[1, ee] = off[ee, 0].astype(jnp.int32)
        meta_ref[2, 0] = nchunks

    def expert_rows(first_blk, nblk):
        rows = nblk * MOE_BLK
        rs = pl.ds(pl.multiple_of(first_blk * MOE_BLK, MOE_BLK), rows)
        xg = sl_ref[rs, :]
        a = _dot(xg, w1_ref[0])
        b = _dot(xg, w3_ref[0])
        y = _dot((jax.nn.silu(a) * b).astype(BF16), w2_ref[0])
        lane_b = lax.broadcasted_iota(jnp.int32, (rows, LANES), 1)
        mine = jnp.bitwise_and(lane_b, wr_ref.shape[0] - 1) == e
        gate = jnp.sum(jnp.where(mine, gg_ref[rs, :], 0.0), axis=-1, keepdims=True)
        sl_ref[rs, :] = (gate * y).astype(BF16)

    nblk = meta_ref[0, e]
    first = meta_ref[1, e]
    rem = lax.rem(nblk, MOE_STRIDE)
    lead = jnp.where(jnp.logical_and(rem > 0, nblk > MOE_STRIDE), rem + MOE_STRIDE, rem)
    for n in list(range(1, MOE_STRIDE)) + list(range(MOE_STRIDE + 1, 2 * MOE_STRIDE)):
        @pl.when(lead == n)
        def _(n=n):
            expert_rows(first, n)

    def stride(js, carry):
        expert_rows(first + lead + MOE_STRIDE * js, MOE_STRIDE)
        return carry

    lax.fori_loop(0, lax.div(nblk - lead, MOE_STRIDE), stride, 0)

    @pl.when(e == ne - 1)
    def _():
        def combine(c, carry):
            rs = pl.ds(pl.multiple_of(c * MOE_CHUNK, MOE_CHUNK), MOE_CHUNK)
            o_ref[0] += _dot_tn(pt_ref[c], sl_ref[rs, :])
            return carry

        lax.fori_loop(0, meta_ref[2, 0], combine, 0)
        o_ref[0] = _rms(o_ref[0], gf_ref[...])


def _moe_final(h, g, w_router, w1, w3, w2, g_final, tm):
    b, l, d = h.shape
    ne, _, f = w1.shape
    slots = -(-(2 * tm + ne * MOE_BLK) // MOE_CHUNK) * MOE_CHUNK
    r = jnp.arange(tm)
    tri = (r[:, None] <= r[None, :]).astype(BF16)
    rows = pl.BlockSpec((1, tm, d), lambda bi, i, e: (bi, i, 0))
    return pl.pallas_call(
        functools.partial(_moe_kernel, ne=ne),
        grid=(b, l // tm, ne),
        in_specs=[rows, _full((1, d)), _full(w_router.shape), _resident((tm, tm)),
                  pl.BlockSpec((1, d, f), lambda bi, i, e: (e, 0, 0)),
                  pl.BlockSpec((1, d, f), lambda bi, i, e: (e, 0, 0)),
                  pl.BlockSpec((1, f, d), lambda bi, i, e: (e, 0, 0)), _full((1, d))],
        out_specs=rows,
        out_shape=jax.ShapeDtypeStruct((b, l, d), F32),
        scratch_shapes=[pltpu.VMEM((slots // MOE_CHUNK, MOE_CHUNK, tm), BF16), pltpu.VMEM((slots, d), BF16),
                        pltpu.VMEM((slots, LANES), F32),
                        pltpu.SMEM((3, ne), jnp.int32)],
        compiler_params=_cp("parallel", "parallel", "arbitrary"),
        name="moe_final",
    )(h, g, w_router, tri, w1, w3, w2, g_final)


def kernel(x_prompt, x_sample, state_pool, state_s5_re, state_s5_im, state_gla, meta_tokens, norm_mix_e, w_in_e, w_pool, pool_scale, s5_a_re, s5_a_im, s5_log_dt, s5_b_re, s5_b_im, s5_c_re, s5_c_im, s5_d, w_glu, b_glu, w_out_e, norm_ffn_e, ffn_w1, ffn_w3, ffn_w2, norm_mix_o, w_in_o, w_gk1, w_gk2, b_gk, gla_norm, w_out_o, norm_ffn_o, w_router, moe_w1, moe_w3, moe_w2, norm_final):
    bp, seq, d = x_prompt.shape
    bs = x_sample.shape[0]
    lp = seq + CHUNK
    dp = w_pool.shape[1] * w_pool.shape[2]
    g_ssm, n_ssm, p_ssm = s5_b_re.shape[1:]
    gn = g_ssm * n_ssm
    kd = w_gk2.shape[2]
    vd = w_out_o.shape[1]
    row = lambda t: t.reshape(1, -1)
    bf = lambda t: t.astype(BF16)

    j = 0
    w_in_e_b, w_out_e_b = bf(w_in_e[j]), bf(w_out_e[j])
    w_pool_b, w_glu_b = bf(w_pool[j]), bf(w_glu[j])
    w1_b, w3_b, w2_b = bf(ffn_w1[j]), bf(ffn_w3[j]), bf(ffn_w2[j])
    lbr, lbi, bbr, bbi = _s5_prep(s5_a_re[j], s5_a_im[j], s5_log_dt[j], s5_b_re[j], s5_b_im[j])
    s5_args = (lbr, lbi, bf(_block_diag_in(bbr, g_ssm, n_ssm, p_ssm)), bf(_block_diag_in(bbi, g_ssm, n_ssm, p_ssm)),
               bf(_block_diag_out(s5_c_re[j])), bf(_block_diag_out(s5_c_im[j])),
               row(s5_d[j]), w_glu_b, row(b_glu[j]))
    w_in_o_b, w_out_o_b = bf(w_in_o[j]), bf(w_out_o[j])
    rank = w_gk1.shape[2]
    wg1 = bf(jnp.pad(w_gk1[j], ((0, 0), (0, LANES - rank))))
    wg2 = bf(jnp.pad(w_gk2[j], ((0, LANES - rank), (0, 0))))
    w_router_b = bf(jnp.pad(w_router[j].T, ((0, MOE_ROUTE_ROWS - w_router.shape[2]), (0, 0))))
    mw1, mw3, mw2 = bf(moe_w1[j]), bf(moe_w3[j]), bf(moe_w2[j])
    mix_w = (row(norm_mix_e[j]), w_in_e_b)
    pool_w = (w_pool_b, row(pool_scale[j]))

    pool_init = jnp.pad(state_pool[j].transpose(1, 0, 2), ((1, 0), (0, 0), (0, 0))).reshape((POOL_BUF + 1) * bs, dp)
    h1s, pool_s, re_s, im_s = _mixer0(
        x_sample.reshape(1, bs, d), None, *mix_w, pool_init, *pool_w,
        state_s5_re[j].reshape(bs, gn), state_s5_im[j].reshape(bs, gn), s5_args, w_out_e_b,
        tt=1, nb=bs, pos0=PAST_LEN)
    h2s = _ffn(h1s.reshape(bs, d), row(norm_ffn_e[j]), w1_b, w3_b, w2_b, bs)
    qs, ks, gks, vs, gts = _gla_proj(h2s, row(norm_mix_o[j]), w_in_o_b, wg1, wg2, row(b_gk[j]), kd, vd, bs)

    tail = jnp.concatenate([jnp.zeros((CHUNK - N_META, d), F32), meta_tokens.astype(F32)], axis=0)
    zero_state = jnp.zeros((bp, gn), F32)
    h1, pool_p, re_p, im_p = _mixer0(
        x_prompt, tail, *mix_w, jnp.zeros(((POOL_BUF + 1) * bp, dp), F32), *pool_w,
        zero_state, zero_state, s5_args, w_out_e_b, tt=CHUNK, nb=bp, pos0=-(CHUNK - N_META))
    tm = _first_divisor(bp * lp, ROW_TILES)
    h2, os_, gla_s = _ffn_with_decode_gla(h1.reshape(bp * lp, d), row(norm_ffn_e[j]), w1_b, w3_b, w2_b,
                                          qs, ks, gks, vs, state_gla[j])
    q, k, gk, v, gt = _gla_proj(h2, row(norm_mix_o[j]), w_in_o_b, wg1, wg2, row(b_gk[j]), kd, vd, tm)
    seq3 = lambda t: t.reshape(bp, lp, t.shape[1])
    o, gla_p = _gla_chunked(seq3(q), seq3(k), seq3(gk), seq3(v), rot=lp // CHUNK - 1)
    ts = _first_divisor(seq, ROW_TILES)
    h3 = _gla_out(o, seq3(gt), seq3(h2), row(gla_norm[j]), w_out_o_b, seq, ts)
    y_prompt = _moe_final(h3, row(norm_ffn_o[j]), w_router_b, mw1, mw3, mw2, row(norm_final), ts)

    h3s = _gla_out(os_[None], gts[None], h2s[None], row(gla_norm[j]), w_out_o_b, bs, bs)
    y_sample = _moe_final(h3s, row(norm_ffn_o[j]), w_router_b, mw1, mw3, mw2, row(norm_final), bs)

    tb = lambda t, nb: t.reshape(POOL_BUF, nb, dp).transpose(1, 0, 2)[None]
    ssm = lambda t, nb: t.reshape(1, nb, g_ssm, n_ssm)
    return (y_prompt, y_sample.reshape(bs, 1, d), tb(pool_p, bp), tb(pool_s, bs),
            ssm(re_p, bp), ssm(re_s, bs), ssm(im_p, bp), ssm(im_s, bs), gla_p[None], gla_s[None])
```
